```python
import jax, jax.numpy as jnp
from jax import lax
import numpy as np

D_MODEL = 2048
BATCH = 8
SEQ = 4096
DEPTH = 2

GRID_W = 64
CTX_LEN = 256
N_MIXERS = 2
POOL_WINDOWS = (2, 4, 8, 16)
POOL_GROUPS = 4
POOL_GROUP_DIM = D_MODEL // POOL_GROUPS
HGRN_HEADS = 16
HGRN_EXPAND = D_MODEL // HGRN_HEADS
HGRN_HEAD_V = D_MODEL // HGRN_HEADS
HGRN_FDIM = HGRN_HEADS * HGRN_EXPAND
CHUNK = 32
N_GROUPS = 4
EXPERTS_PER_GROUP = 8
N_EXPERTS = N_GROUPS * EXPERTS_PER_GROUP
TOP_K = 2
D_EXPERT = 1024
MOE_BLOCK = 128
EPS = 1e-6

kernel_name = "hybrid_pool_hgrn2_hmoe_prefix_dit"


def rms_norm(x, g):
    xf = x.astype(jnp.float32)
    y = xf * lax.rsqrt(jnp.mean(xf * xf, axis=-1, keepdims=True) + EPS)
    return (y * g.astype(jnp.float32)).astype(x.dtype)


def modulate(h, shift, scale):
    return h * (1 + scale) + shift


def adaln(cond, w, b):
    m = (jax.nn.silu(cond) @ w + b)[:, None, :]
    return jnp.split(m, 6, axis=-1)


def centred_window_mean(x, k, axis):
    xf = x.astype(jnp.float32)
    L = x.shape[axis]
    cs = jnp.cumsum(xf, axis=axis)
    pad = [(0, 0)] * x.ndim
    pad[axis] = (1, 0)
    cs = jnp.pad(cs, pad)
    pos = np.arange(L)
    lo = np.clip(pos - k // 2, 0, L - 1)
    hi = np.clip(pos + (k - k // 2 - 1), 0, L - 1)
    s = jnp.take(cs, hi + 1, axis=axis) - jnp.take(cs, lo, axis=axis)
    shape = [1] * x.ndim
    shape[axis] = L
    cnt = (hi - lo + 1).astype(np.float32).reshape(shape)
    return s / cnt


def pool_mixer(h, w_pool, scale, on_grid):
    B, T, _ = h.shape
    outs = []
    for j, k in enumerate(POOL_WINDOWS):
        hg = h[..., j * POOL_GROUP_DIM:(j + 1) * POOL_GROUP_DIM]
        if on_grid:
            rows = T // GRID_W
            g = hg.reshape(B, rows, GRID_W, POOL_GROUP_DIM)
            m = centred_window_mean(centred_window_mean(g, k, 1), k, 2).reshape(B, T, POOL_GROUP_DIM)
        else:
            m = centred_window_mean(hg, k, 1)
        d = (m - hg.astype(jnp.float32)).astype(h.dtype)
        outs.append(d @ w_pool[j])
    return jnp.concatenate(outs, axis=-1) * scale


def forget_gate(z, lb):
    z = z.astype(jnp.float32)
    logf = jnp.logaddexp(jnp.log(lb), jnp.log1p(-lb) + jax.nn.log_sigmoid(z))
    return logf, -jnp.expm1(logf)


def gla_chunked(q, k, v, logf, s0):
    B, L, H, _ = q.shape
    n = L // CHUNK

    def to_chunks(a):
        return a.reshape(B, n, CHUNK, H, a.shape[-1]).transpose(1, 0, 3, 2, 4)

    causal = jnp.tril(jnp.ones((CHUNK, CHUNK), dtype=bool))[:, :, None]

    def step(S, inp):
        qc, kc, vc, lc = inp
        b = jnp.cumsum(lc, axis=2)
        diff = b[:, :, :, None, :] - b[:, :, None, :, :]
        decay = jnp.exp(jnp.where(causal, diff, -jnp.inf))
        A = jnp.einsum('bhtd,bhsd,bhtsd->bhts', qc, kc, decay)
        o = jnp.einsum('bhts,bhsv->bhtv', A, vc) + jnp.einsum('bhtd,bhdv->bhtv', qc * jnp.exp(b), S)
        bl = b[:, :, -1:, :]
        S = jnp.exp(bl[:, :, 0, :])[..., None] * S + jnp.einsum('bhsd,bhsv->bhdv', kc * jnp.exp(bl - b), vc)
        return S, o

    S, o = lax.scan(step, s0, (to_chunks(q), to_chunks(k), to_chunks(v), to_chunks(logf)))
    o = o.transpose(1, 0, 3, 2, 4).reshape(B, L, H, v.shape[-1])
    return o, S


def gla_final_state(k, v, logf):
    b = jnp.cumsum(logf, axis=1)
    return jnp.einsum('blhd,blhv->bhdv', k * jnp.exp(b[:, -1:] - b), v)


def hgrn2_mixer(hx, hc, w_in, norm_g, w_out, lb, ctx_out):
    Dm = D_MODEL

    def heads(a):
        return a.reshape(a.shape[0], a.shape[1], HGRN_HEADS, -1)

    def flip(a):
        return jnp.flip(a, axis=1)

    def read(o, g):
        on = o * lax.rsqrt(jnp.mean(o * o, axis=-1, keepdims=True) + EPS) * norm_g.astype(jnp.float32)
        y = on * jax.nn.silu(heads(g.astype(jnp.float32)))
        return y.reshape(o.shape[0], o.shape[1], Dm).astype(hx.dtype) @ w_out

    if ctx_out:
        qc, gc, zfc, zbc, ic = jnp.split(hc @ w_in, 5, axis=-1)
    else:
        zfc, zbc, ic = jnp.split(hc @ w_in[:, 2 * Dm:], 3, axis=-1)
    lfc, kfc = forget_gate(zfc, lb[0])
    lbc, kbc = forget_gate(zbc, lb[1])
    lfc, kfc, lbc, kbc = heads(lfc), heads(kfc), heads(lbc), heads(kbc)
    vc = heads(ic.astype(jnp.float32))
    if ctx_out:
        qch = heads(jax.nn.silu(qc.astype(jnp.float32)))
        s0 = jnp.zeros((hc.shape[0], HGRN_HEADS, HGRN_EXPAND, HGRN_HEAD_V), jnp.float32)
        oc_f, s_f = gla_chunked(qch, kfc, vc, lfc, s0)
        oc_b, s_b = gla_chunked(flip(qch), flip(kbc), flip(vc), flip(lbc), s0)
        yc = read(oc_f + flip(oc_b), gc)
    else:
        s_f = gla_final_state(kfc, vc, lfc)
        s_b = gla_final_state(flip(kbc), flip(vc), flip(lbc))
        yc = None

    qx, gx, zfx, zbx, ix = jnp.split(hx @ w_in, 5, axis=-1)
    lfx, kfx = forget_gate(zfx, lb[0])
    lbx, kbx = forget_gate(zbx, lb[1])
    qxh = heads(jax.nn.silu(qx.astype(jnp.float32)))
    vx = heads(ix.astype(jnp.float32))
    ox_f, _ = gla_chunked(qxh, heads(kfx), vx, heads(lfx), s_f)
    ox_b, _ = gla_chunked(flip(qxh), flip(heads(kbx)), flip(vx), flip(heads(lbx)), s_b)
    yx = read(ox_f + flip(ox_b), gx)
    return yx, yc


def moe_ffn(h, w_grp, b_grp, w_exp, b_exp, w_gu, w_down):
    N, Dm = h.shape
    hf = h.astype(jnp.float32)
    p_grp = jax.nn.softmax(hf @ w_grp.astype(jnp.float32) + b_grp.astype(jnp.float32), axis=-1)
    g_idx = jnp.argmax(p_grp, axis=-1)
    p_g = jnp.max(p_grp, axis=-1)
    le = (hf @ w_exp.astype(jnp.float32) + b_exp.astype(jnp.float32)).reshape(N, N_GROUPS, EXPERTS_PER_GROUP)
    le = jnp.take_along_axis(le, g_idx[:, None, None], axis=1)[:, 0]
    vals, e_in = lax.top_k(jax.nn.softmax(le, axis=-1), TOP_K)
    wts = p_g[:, None] * vals / jnp.sum(vals, axis=-1, keepdims=True)
    eid = g_idx[:, None] * EXPERTS_PER_GROUP + e_in

    S = N * TOP_K
    e_flat = eid.reshape(-1)
    tok = jnp.repeat(jnp.arange(N), TOP_K)
    order = jnp.argsort(e_flat)
    e_s, tok_s, w_s = e_flat[order], tok[order], wts.reshape(-1)[order]
    counts = jnp.bincount(e_flat, length=N_EXPERTS)
    starts = jnp.cumsum(counts) - counts
    padded = ((counts + MOE_BLOCK - 1) // MOE_BLOCK) * MOE_BLOCK
    pends = jnp.cumsum(padded)
    pstarts = pends - padded
    dest = pstarts[e_s] + (jnp.arange(S) - starts[e_s])
    n_blocks = -(-(S + N_EXPERTS * (MOE_BLOCK - 1)) // MOE_BLOCK)
    P = n_blocks * MOE_BLOCK
    xs = jnp.zeros((P, Dm), h.dtype).at[dest].set(h[tok_s])
    block_e = jnp.clip(jnp.searchsorted(pends, jnp.arange(n_blocks) * MOE_BLOCK, side='right'), 0, N_EXPERTS - 1)

    def block_fn(args):
        xb, e = args
        a, b = jnp.split(xb @ w_gu[e], 2, axis=-1)
        return (jax.nn.silu(a) * b) @ w_down[e]

    ys = lax.map(block_fn, (xs.reshape(n_blocks, MOE_BLOCK, Dm), block_e)).reshape(P, Dm)
    out = jnp.zeros((N, Dm), jnp.float32).at[tok_s].add(w_s[:, None] * ys[dest].astype(jnp.float32))
    return out.astype(h.dtype)


def setup_inputs(seed: int = 0) -> dict:
    key = jax.random.key(seed)
    ks = jax.random.split(key, 24)
    n_pool = (DEPTH + N_MIXERS - 1) // N_MIXERS
    n_hgrn = DEPTH // N_MIXERS
    D = D_MODEL
    nrm = jax.random.normal
    f32 = jnp.float32
    return {
        "x": nrm(ks[0], (BATCH, SEQ, D), f32),
        "c": nrm(ks[1], (BATCH, D), f32),
        "ctx": nrm(ks[2], (BATCH, CTX_LEN, D), f32),
        "c_ctx": nrm(ks[3], (D,), f32),
        "norm_mix": 1.0 + 0.05 * nrm(ks[4], (DEPTH, D), f32),
        "norm_ffn": 1.0 + 0.05 * nrm(ks[5], (DEPTH, D), f32),
        "w_ada": 0.5 * D ** -0.5 * nrm(ks[6], (DEPTH, D, 6 * D), f32),
        "b_ada": 0.01 * nrm(ks[7], (DEPTH, 6 * D), f32),
        "pool_w": POOL_GROUP_DIM ** -0.5 * nrm(ks[8], (n_pool, POOL_GROUPS, POOL_GROUP_DIM, POOL_GROUP_DIM), f32),
        "pool_scale": 1.0 + 0.1 * nrm(ks[9], (n_pool, D), f32),
        "hgrn_w_in": D ** -0.5 * nrm(ks[10], (n_hgrn, D, 5 * D), f32),
        "hgrn_norm": 1.0 + 0.05 * nrm(ks[11], (n_hgrn, HGRN_HEAD_V), f32),
        "hgrn_w_out": D ** -0.5 * nrm(ks[12], (n_hgrn, D, D), f32),
        "hgrn_lb_logits": nrm(ks[13], (DEPTH, 2, HGRN_FDIM), f32),
        "router_w_group": D ** -0.5 * nrm(ks[14], (DEPTH, D, N_GROUPS), f32),
        "router_b_group": 0.01 * nrm(ks[15], (DEPTH, N_GROUPS), f32),
        "router_w_expert": D ** -0.5 * nrm(ks[16], (DEPTH, D, N_EXPERTS), f32),
        "router_b_expert": 0.01 * nrm(ks[17], (DEPTH, N_EXPERTS), f32),
        "moe_w_gate_up": D ** -0.5 * nrm(ks[18], (DEPTH, N_EXPERTS, D, 2 * D_EXPERT), f32),
        "moe_w_down": D_EXPERT ** -0.5 * nrm(ks[19], (DEPTH, N_EXPERTS, D_EXPERT, D), f32),
        "norm_final": 1.0 + 0.05 * nrm(ks[20], (D,), f32),
    }


def reference(x, c, ctx, c_ctx, norm_mix, norm_ffn, w_ada, b_ada, pool_w, pool_scale, hgrn_w_in, hgrn_norm,
              hgrn_w_out, hgrn_lb_logits, router_w_group, router_b_group, router_w_expert, router_b_expert,
              moe_w_gate_up, moe_w_down, norm_final):
    B, T, D = x.shape
    p_lb = jax.nn.softmax(hgrn_lb_logits.astype(jnp.float32), axis=0)
    lb_all = jnp.cumsum(p_lb, axis=0) - p_lb[0]
    for i in range(DEPTH):
        ctx_needed = i < DEPTH - 1
        j = i // N_MIXERS
        sh_m, sc_m, gt_m, sh_f, sc_f, gt_f = adaln(c, w_ada[i], b_ada[i])
        csh_m, csc_m, cgt_m, csh_f, csc_f, cgt_f = adaln(c_ctx[None, :], w_ada[i], b_ada[i])
        hx = modulate(rms_norm(x, norm_mix[i]), sh_m, sc_m)
        if i % N_MIXERS == 0:
            yx = pool_mixer(hx, pool_w[j], pool_scale[j], on_grid=True)
            if ctx_needed:
                hc = modulate(rms_norm(ctx, norm_mix[i]), csh_m, csc_m)
                yc = pool_mixer(hc, pool_w[j], pool_scale[j], on_grid=False)
        else:
            hc = modulate(rms_norm(ctx, norm_mix[i]), csh_m, csc_m)
            yx, yc = hgrn2_mixer(hx, hc, hgrn_w_in[j], hgrn_norm[j], hgrn_w_out[j], lb_all[i], ctx_needed)
        x = x + gt_m * yx
        if ctx_needed:
            ctx = ctx + cgt_m * yc
        fx = modulate(rms_norm(x, norm_ffn[i]), sh_f, sc_f).reshape(B * T, D)
        moe_args = (router_w_group[i], router_b_group[i], router_w_expert[i], router_b_expert[i],
                    moe_w_gate_up[i], moe_w_down[i])
        if ctx_needed:
            L = ctx.shape[1]
            fc = modulate(rms_norm(ctx, norm_ffn[i]), csh_f, csc_f).reshape(B * L, D)
            y_all = moe_ffn(jnp.concatenate([fx, fc], axis=0), *moe_args)
            x = x + gt_f * y_all[:B * T].reshape(B, T, D)
            ctx = ctx + cgt_f * y_all[B * T:].reshape(B, L, D)
        else:
            x = x + gt_f * moe_ffn(fx, *moe_args).reshape(B, T, D)
    return rms_norm(x, norm_final)
```

```python
import functools
from typing import NamedTuple

import numpy as np
import jax
import jax.numpy as jnp
from jax import lax
from jax.experimental import pallas as pl
from jax.experimental.pallas import tpu as pltpu

F32 = jnp.float32
BF16 = jnp.bfloat16
HIGHEST = lax.Precision.HIGHEST
EPS = 1e-6

LANES_V7X = 128
SUBLANES_V7X = 8
VMEM_BYTES_V7X = 64 * 1024 * 1024


class Cfg(NamedTuple):
    grid_w: int = 64
    pool_windows: tuple = (2, 4, 8, 16)
    head_dim: int = LANES_V7X
    n_groups: int = 4
    experts_per_group: int = 8
    top_k: int = 2
    moe_rows: int = 256
    gla_chunk: int = 64
    tok_tile: int = 256
    fast_decay_limit: float = 60.0


def _vmem_limit(nbytes):
    return int(min(max(nbytes * 5 // 4, 16 * 1024 * 1024), VMEM_BYTES_V7X - 6 * 1024 * 1024))


def _cparams(sem, vmem_bytes):
    return pltpu.CompilerParams(dimension_semantics=sem, vmem_limit_bytes=_vmem_limit(vmem_bytes))


def _rms_mod(x, g, shift, scale):
    y = x * lax.rsqrt(jnp.mean(x * x, axis=-1, keepdims=True) + EPS) * g
    return y * (1.0 + scale) + shift


def _silu(x):
    return x * jax.nn.sigmoid(x)


def _adaln_kernel(cond_ref, w_ref, b_ref, o_ref):
    s = _silu(cond_ref[...])
    o_ref[0] = jnp.dot(s, w_ref[0], precision=HIGHEST, preferred_element_type=F32) + b_ref[0]


def _adaln(cond, w_ada, b_ada):
    depth, d, n = w_ada.shape
    rows = cond.shape[0]
    tn = 1024 if n % 1024 == 0 else n
    return pl.pallas_call(
        _adaln_kernel,
        grid=(depth, n // tn),
        in_specs=[pl.BlockSpec((rows, d), lambda l, j: (0, 0)),
                  pl.BlockSpec((1, d, tn), lambda l, j: (l, 0, j)),
                  pl.BlockSpec((1, 1, tn), lambda l, j: (l, 0, j))],
        out_specs=pl.BlockSpec((1, rows, tn), lambda l, j: (l, 0, j)),
        out_shape=jax.ShapeDtypeStruct((depth, rows, n), F32),
        compiler_params=_cparams(("arbitrary", "arbitrary"), 2 * d * tn * 4 + 4 * rows * (d + tn) * 4),
        name="adaln",
    )(cond, w_ada, b_ada.reshape(depth, 1, n))


def _norm_mod_kernel(x_ref, g_ref, sh_ref, sc_ref, o_ref):
    o_ref[0] = _rms_mod(x_ref[0], g_ref[...], sh_ref[0], sc_ref[0]).astype(o_ref.dtype)


def _norm_mod(x, g, sh, sc, out_dtype, tm):
    b, t, d = x.shape
    return pl.pallas_call(
        _norm_mod_kernel,
        grid=(b, t // tm),
        in_specs=[pl.BlockSpec((1, tm, d), lambda i, j: (i, j, 0)),
                  pl.BlockSpec((1, d), lambda i, j: (0, 0)),
                  pl.BlockSpec((1, 1, d), lambda i, j: (i, 0, 0)),
                  pl.BlockSpec((1, 1, d), lambda i, j: (i, 0, 0))],
        out_specs=pl.BlockSpec((1, tm, d), lambda i, j: (i, j, 0)),
        out_shape=jax.ShapeDtypeStruct((b, t, d), out_dtype),
        compiler_params=_cparams(("parallel", "parallel"), 6 * tm * d * 4),
        name="norm_mod",
    )(x, g.reshape(1, d), sh, sc)


def _norm_mod_seq_kernel(n_ctx_tiles, c_ref, x_ref, g_ref, sh_ref, sc_ref, o_ref):
    j = pl.program_id(1)
    src = jnp.where(j < n_ctx_tiles, c_ref[0], x_ref[0])
    o_ref[0] = _rms_mod(src, g_ref[...], sh_ref[0], sc_ref[0]).astype(o_ref.dtype)


def _norm_mod_seq(ctx, x, g, sh_all, sc_all, out_dtype, tm):
    b, t, d = x.shape
    l = ctx.shape[1]
    nc, nx = l // tm, t // tm
    return pl.pallas_call(
        functools.partial(_norm_mod_seq_kernel, nc),
        grid=(b, nc + nx),
        in_specs=[pl.BlockSpec((1, tm, d), lambda i, j: (i, jnp.minimum(j, nc - 1), 0)),
                  pl.BlockSpec((1, tm, d), lambda i, j: (i, jnp.maximum(j - nc, 0), 0)),
                  pl.BlockSpec((1, d), lambda i, j: (0, 0)),
                  pl.BlockSpec((1, 1, d), lambda i, j: (jnp.where(j < nc, b, i), 0, 0)),
                  pl.BlockSpec((1, 1, d), lambda i, j: (jnp.where(j < nc, b, i), 0, 0))],
        out_specs=pl.BlockSpec((1, tm, d), lambda i, j: (i, j, 0)),
        out_shape=jax.ShapeDtypeStruct((b, l + t, d), out_dtype),
        compiler_params=_cparams(("parallel", "arbitrary"), 8 * tm * d * 4),
        name="norm_mod_seq",
    )(ctx, x, g.reshape(1, d), sh_all, sc_all)


def _win_matrix(length, k):
    pos = np.arange(length)
    lo = np.clip(pos - k // 2, 0, length - 1)
    hi = np.clip(pos + (k - k // 2 - 1), 0, length - 1)
    m = np.zeros((length, length), np.float32)
    for p in range(length):
        m[p, lo[p]:hi[p] + 1] = 1.0 / float(hi[p] - lo[p] + 1)
    return m, (1.0 / (hi - lo + 1)).astype(np.float32)


def _pool_tokens_kernel(k, w, slab, vertical, inv_ref, ph_ref, h_ref, o_ref, pad_ref):
    t, tc = h_ref.shape[1], h_ref.shape[2]
    if not vertical:
        for s in range(t // slab):
            hs = h_ref[0, s * slab:(s + 1) * slab, :]
            o_ref[0, s * slab:(s + 1) * slab, :] = (
                jnp.dot(ph_ref[...], hs, precision=HIGHEST, preferred_element_type=F32) - hs)
        return
    top = (k // 2) * w
    bot = (k - k // 2 - 1) * w
    pad_ref[0:top, :] = jnp.zeros((top, tc), F32)
    if bot:
        pad_ref[top + t:top + t + bot, :] = jnp.zeros((bot, tc), F32)
    for s in range(t // slab):
        hs = h_ref[0, s * slab:(s + 1) * slab, :]
        pad_ref[top + s * slab:top + (s + 1) * slab, :] = jnp.dot(
            ph_ref[...], hs, precision=HIGHEST, preferred_element_type=F32)

    def row(r, carry):
        base = pl.multiple_of(r * w, w)
        acc = pad_ref[pl.ds(base, w), :]
        for dr in range(1, k):
            acc = acc + pad_ref[pl.ds(pl.multiple_of(base + dr * w, w), w), :]
        o_ref[0, pl.ds(base, w), :] = acc * inv_ref[r] - h_ref[0, pl.ds(base, w), :]
        return carry

    lax.fori_loop(0, t // w, row, 0)


def _pool_tokens(h, group, gdim, k, grid_w, vertical):
    b, t, d = h.shape
    tc = min(gdim, 256)
    per = gdim // tc
    if vertical:
        slab = max(grid_w, min(256, t))
        mh, _ = _win_matrix(grid_w, k)
        ph = np.kron(np.eye(slab // grid_w, dtype=np.float32), mh)
        _, inv_v = _win_matrix(t // grid_w, k)
    else:
        slab = t
        ph, _ = _win_matrix(t, k)
        inv_v = np.ones((1,), np.float32)
    pad_rows = t + (k - 1) * grid_w if vertical else SUBLANES_V7X
    return pl.pallas_call(
        functools.partial(_pool_tokens_kernel, k, grid_w, slab, vertical),
        grid=(b, per),
        in_specs=[pl.BlockSpec(memory_space=pltpu.SMEM),
                  pl.BlockSpec((slab, slab), lambda i, j: (0, 0)),
                  pl.BlockSpec((1, t, tc), lambda i, j: (i, 0, group * per + j))],
        out_specs=pl.BlockSpec((1, t, tc), lambda i, j: (i, 0, j)),
        out_shape=jax.ShapeDtypeStruct((b, t, gdim), F32),
        scratch_shapes=[pltpu.VMEM((pad_rows, tc), F32)],
        compiler_params=_cparams(("parallel", "parallel"), (4 * t + pad_rows) * tc * 4 + 2 * slab * slab * 4),
        name=f"pool_tokens_k{k}",
    )(jnp.asarray(inv_v), jnp.asarray(ph), h)


def _pool_out_kernel(n_groups, *refs):
    d_refs = refs[:n_groups]
    x_ref, w_ref, ps_ref, gt_ref, o_ref = refs[n_groups:]
    ys = [jnp.dot(d_refs[j][0].astype(BF16), w_ref[j], preferred_element_type=F32) for j in range(n_groups)]
    y = jnp.concatenate(ys, axis=-1) * ps_ref[...]
    o_ref[0] = x_ref[0] + gt_ref[0] * y


def _pool_out(ds, x, w_pool_bf16, pool_scale, gate, tm):
    b, t, d = x.shape
    ng, gdim, _ = w_pool_bf16.shape
    return pl.pallas_call(
        functools.partial(_pool_out_kernel, ng),
        grid=(b, t // tm),
        in_specs=[pl.BlockSpec((1, tm, gdim), lambda i, j: (i, j, 0)) for _ in range(ng)] + [
            pl.BlockSpec((1, tm, d), lambda i, j: (i, j, 0)),
            pl.BlockSpec((ng, gdim, gdim), lambda i, j: (0, 0, 0)),
            pl.BlockSpec((1, d), lambda i, j: (0, 0)),
            pl.BlockSpec((1, 1, d), lambda i, j: (i, 0, 0))],
        out_specs=pl.BlockSpec((1, tm, d), lambda i, j: (i, j, 0)),
        out_shape=jax.ShapeDtypeStruct((b, t, d), F32),
        compiler_params=_cparams(("parallel", "parallel"), 8 * tm * d * 4 + 2 * ng * gdim * gdim * 2),
        name="pool_out",
    )(*ds, x, w_pool_bf16, pool_scale.reshape(1, d), gate)


def _pool_mixer(cfg, x, g, sh, sc, gate, w_pool_bf16, pool_scale, on_grid):
    b, t, d = x.shape
    tm = min(cfg.tok_tile, t)
    h = _norm_mod(x, g, sh, sc, F32, tm)
    gdim = d // len(cfg.pool_windows)
    ds = [_pool_tokens(h, j, gdim, k, cfg.grid_w, on_grid) for j, k in enumerate(cfg.pool_windows)]
    return _pool_out(ds, x, w_pool_bf16, pool_scale, gate, tm)


def _router_kernel(nx_tiles, has_ctx, *refs):
    if has_ctx:
        x_ref, c_ref, g_ref, sh_ref, sc_ref, wr_ref, br_ref, fx_ref, lg_ref = refs
        src = jnp.where(pl.program_id(0) < nx_tiles, x_ref[0], c_ref[0])
    else:
        x_ref, g_ref, sh_ref, sc_ref, wr_ref, br_ref, fx_ref, lg_ref = refs
        src = x_ref[0]
    h = _rms_mod(src, g_ref[...], sh_ref[0], sc_ref[0])
    fx_ref[...] = h
    lg_ref[...] = jnp.dot(h, wr_ref[...], precision=HIGHEST, preferred_element_type=F32) + br_ref[...]


def _router(x, ctx, g, sh_all, sc_all, w_r, b_r, tm):
    b, t, d = x.shape
    per_b = t // tm
    nx = b * per_b
    has_ctx = ctx is not None
    nc = b * (ctx.shape[1] // tm) if has_ctx else 0
    per_c = (ctx.shape[1] // tm) if has_ctx else 1
    n_all = (nx + nc) * tm
    lanes = w_r.shape[1]

    def mod_row(i):
        return jnp.where(i < nx, jnp.minimum(i, nx - 1) // per_b, b) if has_ctx else i // per_b

    in_specs = [pl.BlockSpec((1, tm, d), lambda i: (jnp.minimum(i, nx - 1) // per_b, jnp.minimum(i, nx - 1) % per_b, 0))]
    args = [x]
    if has_ctx:
        in_specs.append(pl.BlockSpec(
            (1, tm, d), lambda i: (jnp.maximum(i - nx, 0) // per_c, jnp.maximum(i - nx, 0) % per_c, 0)))
        args.append(ctx)
    in_specs += [pl.BlockSpec((1, d), lambda i: (0, 0)),
                 pl.BlockSpec((1, 1, d), lambda i: (mod_row(i), 0, 0)),
                 pl.BlockSpec((1, 1, d), lambda i: (mod_row(i), 0, 0)),
                 pl.BlockSpec((d, lanes), lambda i: (0, 0)),
                 pl.BlockSpec((1, lanes), lambda i: (0, 0))]
    args += [g.reshape(1, d), sh_all, sc_all, w_r, b_r]
    return pl.pallas_call(
        functools.partial(_router_kernel, nx, has_ctx),
        grid=(nx + nc,),
        in_specs=in_specs,
        out_specs=[pl.BlockSpec((tm, d), lambda i: (i, 0)), pl.BlockSpec((tm, lanes), lambda i: (i, 0))],
        out_shape=[jax.ShapeDtypeStruct((n_all, d), F32), jax.ShapeDtypeStruct((n_all, lanes), F32)],
        compiler_params=_cparams(("arbitrary",), 8 * tm * d * 4 + 2 * d * lanes * 4),
        name="moe_router",
    )(*args)


def _expert_kernel(rows, n_all, be_ref, nu_ref, nv_ref, code_ref, fx_hbm, wgu_ref, wdn_ref, rw_ref, out_hbm,
                   xbuf, ybuf, gsem, ssem):
    i = pl.program_id(0)
    n_used = nu_ref[0]
    slot = i % 2
    n_slots = n_all * 2

    def gather_row(blk, sl, r):
        v = code_ref[blk * rows + r]
        src = jnp.where(v >= n_slots, 0, v >> 1)
        return pltpu.make_async_copy(fx_hbm.at[pl.ds(src, 1)], xbuf.at[sl, pl.ds(r, 1)], gsem.at[sl])

    def scatter_row(blk, sl, r):
        v = code_ref[blk * rows + r]
        dst = (v & 1) * n_all + (v >> 1)
        return pltpu.make_async_copy(ybuf.at[sl, pl.ds(r, 1)], out_hbm.at[pl.ds(dst, 1)], ssem.at[sl])

    def start_gather(blk, sl):
        def body(r, c):
            gather_row(blk, sl, r).start()
            return c
        lax.fori_loop(0, rows, body, 0)

    def wait_scatter(blk, sl):
        n = nv_ref[blk]
        p = 1
        while p <= rows:
            @pl.when((n & p) != 0)
            def _(p=p):
                pltpu.make_async_copy(ybuf.at[sl, pl.ds(0, p)], out_hbm.at[pl.ds(0, p)], ssem.at[sl]).wait()
            p *= 2

    @pl.when(jnp.logical_and(i == 0, n_used > 0))
    def _():
        start_gather(0, 0)

    @pl.when(i + 1 < n_used)
    def _():
        start_gather(i + 1, 1 - slot)

    @pl.when(i < n_used)
    def _():
        pltpu.make_async_copy(fx_hbm.at[pl.ds(0, rows)], xbuf.at[slot], gsem.at[slot]).wait()

        @pl.when(i >= 2)
        def _():
            wait_scatter(i - 2, slot)

        xb = xbuf[slot].astype(BF16)
        hgu = jnp.dot(xb, wgu_ref[0], preferred_element_type=F32)
        f = hgu.shape[1] // 2
        act = (_silu(hgu[:, :f]) * hgu[:, f:]).astype(BF16)
        y = jnp.dot(act, wdn_ref[0], preferred_element_type=F32)
        ybuf[slot] = y * rw_ref[0]

        def body(r, c):
            scatter_row(i, slot, r).start()
            return c
        lax.fori_loop(0, nv_ref[i], body, 0)

        @pl.when(i == n_used - 1)
        def _():
            wait_scatter(i, slot)

            @pl.when(i >= 1)
            def _():
                wait_scatter(i - 1, 1 - slot)


def _experts(cfg, fx, block_e, n_used, n_valid, row_code, row_w, w_gu_bf16, w_dn_bf16):
    n_all, d = fx.shape
    e, _, f2 = w_gu_bf16.shape
    f = f2 // 2
    rows = cfg.moe_rows
    n_blocks = block_e.shape[0]
    grid_spec = pltpu.PrefetchScalarGridSpec(
        num_scalar_prefetch=4,
        grid=(n_blocks,),
        in_specs=[pl.BlockSpec(memory_space=pl.ANY),
                  pl.BlockSpec((1, d, f2), lambda i, be, nu, nv, cd: (be[i], 0, 0)),
                  pl.BlockSpec((1, f, d), lambda i, be, nu, nv, cd: (be[i], 0, 0)),
                  pl.BlockSpec((1, rows, 1), lambda i, be, nu, nv, cd: (i, 0, 0))],
        out_specs=pl.BlockSpec(memory_space=pl.ANY),
        scratch_shapes=[pltpu.VMEM((2, rows, d), F32), pltpu.VMEM((2, rows, d), F32),
                        pltpu.SemaphoreType.DMA((2,)), pltpu.SemaphoreType.DMA((2,))])
    vmem = 2 * (d * f2 + f * d) * 2 + 4 * rows * d * 4 + 3 * rows * f2 * 4 + 2 * rows * LANES_V7X * 4
    return pl.pallas_call(
        functools.partial(_expert_kernel, rows, n_all),
        grid_spec=grid_spec,
        out_shape=jax.ShapeDtypeStruct((cfg.top_k * n_all, d), F32),
        compiler_params=_cparams(("arbitrary",), vmem),
        name="moe_experts",
    )(block_e, n_used, n_valid, row_code, fx, w_gu_bf16, w_dn_bf16, row_w)


def _combine_kernel(final, *refs):
    if final:
        x_ref, a_ref, b_ref, gt_ref, gf_ref, o_ref = refs
    else:
        x_ref, a_ref, b_ref, gt_ref, o_ref = refs
    y = x_ref[0] + gt_ref[0] * (a_ref[...] + b_ref[...])
    if final:
        y = y * lax.rsqrt(jnp.mean(y * y, axis=-1, keepdims=True) + EPS) * gf_ref[...]
    o_ref[0] = y


def _combine(x, slots, tok_off, n_all, gate, tm, norm_final=None):
    b, t, d = x.shape
    per_b = t // tm
    off0 = tok_off // tm
    off1 = (n_all + tok_off) // tm
    final = norm_final is not None
    in_specs = [pl.BlockSpec((1, tm, d), lambda i, j: (i, j, 0)),
                pl.BlockSpec((tm, d), lambda i, j: (off0 + i * per_b + j, 0)),
                pl.BlockSpec((tm, d), lambda i, j: (off1 + i * per_b + j, 0)),
                pl.BlockSpec((1, 1, d), lambda i, j: (i, 0, 0))]
    args = [x, slots, slots, gate]
    if final:
        in_specs.append(pl.BlockSpec((1, d), lambda i, j: (0, 0)))
        args.append(norm_final.reshape(1, d))
    return pl.pallas_call(
        functools.partial(_combine_kernel, final),
        grid=(b, per_b),
        in_specs=in_specs,
        out_specs=pl.BlockSpec((1, tm, d), lambda i, j: (i, j, 0)),
        out_shape=jax.ShapeDtypeStruct((b, t, d), F32),
        compiler_params=_cparams(("parallel", "parallel"), 10 * tm * d * 4),
        name="moe_combine",
    )(*args)


def _route(cfg, logits):
    n_all = logits.shape[0]
    ng, eg, k = cfg.n_groups, cfg.experts_per_group, cfg.top_k
    assert k == 2, "row codes pack (token, choice) as 2 * token + choice"
    n_exp = ng * eg
    rows = cfg.moe_rows
    p_grp = jax.nn.softmax(logits[:, :ng], axis=-1)
    g_idx = jnp.argmax(p_grp, axis=-1)
    p_g = jnp.max(p_grp, axis=-1)
    le = logits[:, ng:ng + n_exp].reshape(n_all, ng, eg)
    le = jnp.take_along_axis(le, g_idx[:, None, None], axis=1)[:, 0]
    vals, e_in = lax.top_k(jax.nn.softmax(le, axis=-1), k)
    wts = p_g[:, None] * vals / jnp.sum(vals, axis=-1, keepdims=True)
    eid = (g_idx[:, None] * eg + e_in).astype(jnp.int32)

    s = n_all * k
    e_flat = eid.reshape(-1)
    onehot = (e_flat[:, None] == jnp.arange(n_exp, dtype=jnp.int32)[None, :]).astype(jnp.int32)
    csum = jnp.cumsum(onehot, axis=0)
    rank = jnp.sum((csum - onehot) * onehot, axis=1)
    counts = csum[-1]
    padded = ((counts + rows - 1) // rows) * rows
    pends = jnp.cumsum(padded)
    pstarts = pends - padded
    dest = pstarts[e_flat] + rank
    n_blocks = -(-(s + n_exp * (rows - 1)) // rows)
    p = n_blocks * rows
    row_code = jnp.full((p,), s, jnp.int32).at[dest].set(jnp.arange(s, dtype=jnp.int32))
    row_w = jnp.zeros((p,), F32).at[dest].set(wts.reshape(-1)).reshape(n_blocks, rows, 1)
    block_e = jnp.clip(jnp.searchsorted(pends, jnp.arange(n_blocks, dtype=jnp.int32) * rows, side="right"),
                       0, n_exp - 1).astype(jnp.int32)
    n_used = (pends[-1] // rows).astype(jnp.int32).reshape(1)
    blk_start = jnp.arange(n_blocks, dtype=jnp.int32) * rows
    n_valid = jnp.clip(counts[block_e] - (blk_start - pstarts[block_e]), 0, rows)
    n_valid = jnp.where(blk_start < pends[-1], n_valid, 0).astype(jnp.int32)
    return block_e, n_used, n_valid, row_code, row_w


def _moe(cfg, x, ctx, g, sh_all, sc_all, w_r, b_r, w_gu_bf16, w_dn_bf16):
    b, t, d = x.shape
    tm = min(cfg.tok_tile, t) if ctx is None else min(cfg.tok_tile, t, ctx.shape[1])
    fx, logits = _router(x, ctx, g, sh_all, sc_all, w_r, b_r, tm)
    n_all = fx.shape[0]
    plan = _route(cfg, logits)
    slots = _experts(cfg, fx, *plan, w_gu_bf16, w_dn_bf16)
    return slots, n_all


def _proj_kernel(kind, n_heads, hd, *refs):
    if kind == "gate":
        h_ref, w_ref, la_ref, lc_ref, o_ref = refs
    else:
        h_ref, w_ref, o_ref = refs
    z = jnp.dot(h_ref[0], w_ref[...], preferred_element_type=F32)
    if kind == "silu":
        z = _silu(z)
    elif kind == "gate":
        ls = jnp.minimum(z, 0.0) - jnp.log1p(jnp.exp(-jnp.abs(z)))
        c = lc_ref[...] + ls
        a = jnp.broadcast_to(la_ref[...], c.shape)
        m = jnp.maximum(a, c)
        z = m + jnp.log1p(jnp.exp(-jnp.abs(a - c)))
    for h in range(n_heads):
        o_ref[0, h] = z[:, h * hd:(h + 1) * hd].astype(o_ref.dtype)


def _proj(cfg, h_all, w_bf16, kind, out_dtype, log_lb=None, log1m_lb=None):
    b, s, d = h_all.shape
    n = w_bf16.shape[1]
    hd = cfg.head_dim
    nh = n // hd
    tm = cfg.tok_tile
    in_specs = [pl.BlockSpec((1, tm, d), lambda i, j: (i, j, 0)),
                pl.BlockSpec((d, n), lambda i, j: (0, 0))]
    args = [h_all, w_bf16]
    if kind == "gate":
        in_specs += [pl.BlockSpec((1, n), lambda i, j: (0, 0)), pl.BlockSpec((1, n), lambda i, j: (0, 0))]
        args += [log_lb.reshape(1, n), log1m_lb.reshape(1, n)]
    return pl.pallas_call(
        functools.partial(_proj_kernel, kind, nh, hd),
        grid=(b, s // tm),
        in_specs=in_specs,
        out_specs=pl.BlockSpec((1, nh, tm, hd), lambda i, j: (i, 0, j, 0)),
        out_shape=jax.ShapeDtypeStruct((b, nh, s, hd), out_dtype),
        compiler_params=_cparams(("parallel", "parallel"), 2 * d * n * 2 + 2 * tm * d * 2 + 6 * tm * n * 4),
        name=f"hgrn_proj_{kind}",
    )(*args)


def _gla_kernel(chunk, reverse, readout, limit, *refs):
    if readout:
        q_ref, lf_ref, v_ref, of_ref, gs_ref, ng_ref, o_ref, st_ref, b_ref, q32_ref, at_ref = refs
    else:
        q_ref, lf_ref, v_ref, o_ref, st_ref, b_ref, q32_ref, at_ref = refs
    hg, tb, hd = q_ref.shape[1], q_ref.shape[2], q_ref.shape[3]
    n_chunks = tb // chunk
    shift = chunk.bit_length() - 1

    @pl.when(pl.program_id(2) == 0)
    def _():
        st_ref[...] = jnp.zeros(st_ref.shape, F32)

    rr = lax.broadcasted_iota(jnp.int32, (tb, tb), 0)
    cc = lax.broadcasted_iota(jnp.int32, (tb, tb), 1)
    same = (rr >> shift) == (cc >> shift)
    tri = jnp.where(jnp.logical_and(same, (cc >= rr) if reverse else (cc <= rr)), 1.0, 0.0).astype(F32)
    ar = lax.broadcasted_iota(jnp.int32, (chunk, chunk), 0)
    ac = lax.broadcasted_iota(jnp.int32, (chunk, chunk), 1)
    at_mask = (ar >= ac) if reverse else (ar <= ac)
    row_id = lax.broadcasted_iota(jnp.int32, (chunk, hd), 0)
    order = list(range(n_chunks))[::-1] if reverse else list(range(n_chunks))
    edge = 0 if reverse else chunk - 1

    def head(h, carry):
        lf = lf_ref[0, h]
        b = jnp.dot(tri, lf, precision=HIGHEST, preferred_element_type=F32)
        b_ref[...] = b
        btot = [b[c * chunk + edge:c * chunk + edge + 1, :] for c in range(n_chunks)]
        worst = functools.reduce(jnp.minimum, btot)
        fast = jnp.min(worst) >= -limit
        kk = 1.0 - jnp.exp(lf)
        qf = q_ref[0, h].astype(F32)
        qt = qf * jnp.exp(b)

        @pl.when(fast)
        def _():
            kh = (kk * jnp.exp(-b)).astype(BF16)
            qb = qt.astype(BF16)
            for c in range(n_chunks):
                sl = slice(c * chunk, (c + 1) * chunk)
                at = lax.dot_general(kh[sl], qb[sl], (((1,), (1,)), ((), ())), preferred_element_type=F32)
                at_ref[c] = jnp.where(at_mask, at, 0.0)

        @pl.when(jnp.logical_not(fast))
        def _():
            q32_ref[...] = qf
            for c in range(n_chunks):
                c0 = c * chunk
                bc = b[c0:c0 + chunk]
                kc = kk[c0:c0 + chunk]

                def tstep(t, at):
                    bt = b_ref[pl.ds(c0 + t, 1), :]
                    qrow = q32_ref[pl.ds(c0 + t, 1), :]
                    valid = (row_id >= t) if reverse else (row_id <= t)
                    e = jnp.exp(jnp.where(valid, bt - bc, -jnp.inf))
                    col = jnp.sum(e * kc * qrow, axis=-1, keepdims=True)
                    return at + col * jnp.where(ac == t, 1.0, 0.0)

                at_ref[c] = lax.fori_loop(0, chunk, tstep, jnp.zeros((chunk, chunk), F32))

        qb = qt.astype(BF16)
        vb = v_ref[0, h]
        st = st_ref[h]
        for c in order:
            sl = slice(c * chunk, (c + 1) * chunk)
            k2 = (kk[sl] * jnp.exp(btot[c] - b[sl])).astype(BF16)
            o = lax.dot_general(at_ref[c].astype(BF16), vb[sl], (((0,), (0,)), ((), ())),
                                preferred_element_type=F32)
            o = o + lax.dot_general(qb[sl], st.astype(BF16), (((1,), (1,)), ((), ())),
                                    preferred_element_type=F32)
            st = st * jnp.exp(btot[c]) + lax.dot_general(vb[sl], k2, (((0,), (0,)), ((), ())),
                                                         preferred_element_type=F32)
            if readout:
                o = o + of_ref[0, h, sl, :]
                o = o * lax.rsqrt(jnp.mean(o * o, axis=-1, keepdims=True) + EPS) * ng_ref[...]
                o_ref[0, h, sl, :] = (o * gs_ref[0, h, sl, :].astype(F32)).astype(o_ref.dtype)
            else:
                o_ref[0, h, sl, :] = o
        st_ref[h] = st
        return carry

    lax.fori_loop(0, hg, head, 0)


def _gla(cfg, qs, lf, v, reverse, n_ctx_blocks, of=None, gs=None, norm_g=None):
    b, nh, s, hd = qs.shape
    tb = cfg.tok_tile
    hg = nh
    nblk = s // tb
    readout = of is not None

    if reverse:
        def blk(j):
            return jnp.where(j < n_ctx_blocks, n_ctx_blocks - 1 - j, nblk - 1 - (j - n_ctx_blocks))
    else:
        def blk(j):
            return j

    spec = pl.BlockSpec((1, hg, tb, hd), lambda i, g, j: (i, g, blk(j), 0))
    in_specs = [spec, spec, spec]
    args = [qs, lf, v]
    if readout:
        in_specs += [spec, spec, pl.BlockSpec((1, hd), lambda i, g, j: (0, 0))]
        args += [of, gs, norm_g.reshape(1, hd)]
    out_dtype = BF16 if readout else F32
    vmem = 2 * hg * tb * hd * (2 + 4 + 2 + 4 + 2 + 4) + hg * hd * hd * 4 + 8 * tb * tb * 4 + 16 * tb * hd * 4
    return pl.pallas_call(
        functools.partial(_gla_kernel, cfg.gla_chunk, reverse, readout, cfg.fast_decay_limit),
        grid=(b, nh // hg, nblk),
        in_specs=in_specs,
        out_specs=spec,
        out_shape=jax.ShapeDtypeStruct((b, nh, s, hd), out_dtype),
        scratch_shapes=[pltpu.VMEM((hg, hd, hd), F32), pltpu.VMEM((tb, hd), F32), pltpu.VMEM((tb, hd), F32),
                        pltpu.VMEM((tb // cfg.gla_chunk, cfg.gla_chunk, cfg.gla_chunk), F32)],
        compiler_params=_cparams(("parallel", "parallel", "arbitrary"), vmem),
        name="hgrn_scan_bwd" if reverse else "hgrn_scan_fwd",
    )(*args)


def _out_proj_kernel(n_heads, y_ref, x_ref, w_ref, gt_ref, o_ref):
    y = jnp.concatenate([y_ref[0, h] for h in range(n_heads)], axis=-1)
    o_ref[0] = x_ref[0] + gt_ref[0] * jnp.dot(y, w_ref[...], preferred_element_type=F32)


def _out_proj(cfg, y_heads, x, w_out_bf16, gate, ctx_tiles):
    b, t, d = x.shape
    nh, hd = y_heads.shape[1], y_heads.shape[3]
    tm = cfg.tok_tile
    return pl.pallas_call(
        functools.partial(_out_proj_kernel, nh),
        grid=(b, t // tm),
        in_specs=[pl.BlockSpec((1, nh, tm, hd), lambda i, j: (i, 0, j + ctx_tiles, 0)),
                  pl.BlockSpec((1, tm, d), lambda i, j: (i, j, 0)),
                  pl.BlockSpec((d, d), lambda i, j: (0, 0)),
                  pl.BlockSpec((1, 1, d), lambda i, j: (i, 0, 0))],
        out_specs=pl.BlockSpec((1, tm, d), lambda i, j: (i, j, 0)),
        out_shape=jax.ShapeDtypeStruct((b, t, d), F32),
        compiler_params=_cparams(("parallel", "parallel"), 2 * d * d * 2 + 8 * tm * d * 4),
        name="hgrn_out_proj",
    )(y_heads, x, w_out_bf16, gate)


def _hgrn_mixer(cfg, x, ctx, g, sh_all, sc_all, gate, w_in, norm_g, w_out, lb):
    b, t, d = x.shape
    tm = cfg.tok_tile
    h_all = _norm_mod_seq(ctx, x, g, sh_all, sc_all, BF16, tm)
    w = w_in.astype(BF16)
    log_lb = jnp.log(lb)
    log1m_lb = jnp.log1p(-lb)
    qs = _proj(cfg, h_all, w[:, 0 * d:1 * d], "silu", BF16)
    gs = _proj(cfg, h_all, w[:, 1 * d:2 * d], "silu", BF16)
    lf = _proj(cfg, h_all, w[:, 2 * d:3 * d], "gate", F32, log_lb[0], log1m_lb[0])
    lr = _proj(cfg, h_all, w[:, 3 * d:4 * d], "gate", F32, log_lb[1], log1m_lb[1])
    v = _proj(cfg, h_all, w[:, 4 * d:5 * d], "none", BF16)
    n_ctx_blocks = ctx.shape[1] // tm
    of = _gla(cfg, qs, lf, v, False, n_ctx_blocks)
    y = _gla(cfg, qs, lr, v, True, n_ctx_blocks, of, gs, norm_g)
    return _out_proj(cfg, y, x, w_out.astype(BF16), gate, n_ctx_blocks)


def _forward(cfg, x, c, ctx, c_ctx, norm_mix, norm_ffn, w_ada, b_ada, pool_w, pool_scale, hgrn_w_in, hgrn_norm,
             hgrn_w_out, hgrn_lb_logits, router_w_group, router_b_group, router_w_expert, router_b_expert,
             moe_w_gate_up, moe_w_down, norm_final):
    b, t, d = x.shape
    depth = w_ada.shape[0]
    n_mixers = 2
    p_lb = jax.nn.softmax(hgrn_lb_logits.astype(F32), axis=0)
    lb_all = jnp.cumsum(p_lb, axis=0) - p_lb[0]

    rows = -(-(b + 1) // SUBLANES_V7X) * SUBLANES_V7X
    cond = jnp.zeros((rows, d), F32).at[:b].set(c).at[b].set(c_ctx)
    mods = _adaln(cond, w_ada, b_ada)

    ng, ne = cfg.n_groups, cfg.n_groups * cfg.experts_per_group
    for i in range(depth):
        ctx_needed = i < depth - 1
        j = i // n_mixers
        m = [mods[i, :, q * d:(q + 1) * d].reshape(rows, 1, d) for q in range(6)]
        sh_m, sc_m, gt_m, sh_f, sc_f, gt_f = m
        ctx_rows = lambda a: jnp.broadcast_to(a[b:b + 1], (b, 1, d))
        if i % n_mixers == 0:
            wp = pool_w[j].astype(BF16)
            x_new = _pool_mixer(cfg, x, norm_mix[i], sh_m, sc_m, gt_m, wp, pool_scale[j], True)
            if ctx_needed:
                ctx = _pool_mixer(cfg, ctx, norm_mix[i], ctx_rows(sh_m), ctx_rows(sc_m), ctx_rows(gt_m), wp,
                                  pool_scale[j], False)
            x = x_new
        else:
            x = _hgrn_mixer(cfg, x, ctx, norm_mix[i], sh_m, sc_m, gt_m, hgrn_w_in[j], hgrn_norm[j],
                            hgrn_w_out[j], lb_all[i])
        lanes = -(-(ng + ne) // LANES_V7X) * LANES_V7X
        w_r = jnp.zeros((d, lanes), F32).at[:, :ng].set(router_w_group[i]).at[:, ng:ng + ne].set(router_w_expert[i])
        b_r = jnp.zeros((1, lanes), F32).at[0, :ng].set(router_b_group[i]).at[0, ng:ng + ne].set(router_b_expert[i])
        slots, n_all = _moe(cfg, x, ctx if ctx_needed else None, norm_ffn[i], sh_f, sc_f, w_r, b_r,
                            moe_w_gate_up[i].astype(BF16), moe_w_down[i].astype(BF16))
        tm = min(cfg.tok_tile, t)
        last = i == depth - 1
        x_next = _combine(x, slots, 0, n_all, gt_f, tm, norm_final if last else None)
        if ctx_needed:
            ctx = _combine(ctx, slots, b * t, n_all, ctx_rows(gt_f), min(cfg.tok_tile, ctx.shape[1]))
        x = x_next
    return x


def kernel(x, c, ctx, c_ctx, norm_mix, norm_ffn, w_ada, b_ada, pool_w, pool_scale, hgrn_w_in, hgrn_norm, hgrn_w_out, hgrn_lb_logits, router_w_group, router_b_group, router_w_expert, router_b_expert, moe_w_gate_up, moe_w_down, norm_final):
    return _forward(Cfg(), x, c, ctx, c_ctx, norm_mix, norm_ffn, w_ada, b_ada, pool_w, pool_scale, hgrn_w_in,
                    hgrn_norm, hgrn_w_out, hgrn_lb_logits, router_w_group, router_b_group, router_w_expert,
                    router_b_expert, moe_w_gate_up, moe_w_down, norm_final)
```

```python
import functools
from typing import NamedTuple

import numpy as np
import jax
import jax.numpy as jnp
from jax import lax
from jax.experimental import pallas as pl
from jax.experimental.pallas import tpu as pltpu

F32 = jnp.float32
BF16 = jnp.bfloat16
HIGHEST = lax.Precision.HIGHEST
EPS = 1e-6

LANES_V7X = 128
SUBLANES_V7X = 8
VMEM_BYTES_V7X = 64 * 1024 * 1024


class Cfg(NamedTuple):
    grid_w: int = 64
    pool_windows: tuple = (2, 4, 8, 16)
    head_dim: int = LANES_V7X
    n_groups: int = 4
    experts_per_group: int = 8
    top_k: int = 2
    moe_rows: int = 256
    gla_chunk: int = 64
    tok_tile: int = 256
    fast_decay_limit: float = 120.0


def _vmem_limit(nbytes):
    return int(min(max(nbytes * 5 // 4, 16 * 1024 * 1024), VMEM_BYTES_V7X - 6 * 1024 * 1024))


def _cparams(sem, vmem_bytes):
    return pltpu.CompilerParams(dimension_semantics=sem, vmem_limit_bytes=_vmem_limit(vmem_bytes))


def _rms_mod(x, g, shift, scale):
    y = x * lax.rsqrt(jnp.mean(x * x, axis=-1, keepdims=True) + EPS) * g
    return y * (1.0 + scale) + shift


def _silu(x):
    return x * jax.nn.sigmoid(x)


def _adaln_kernel(cond_ref, w_ref, b_ref, o_ref):
    s = _silu(cond_ref[...])
    o_ref[0] = jnp.dot(s, w_ref[0], precision=HIGHEST, preferred_element_type=F32) + b_ref[0]


def _adaln(cond, w_ada, b_ada):
    depth, d, n = w_ada.shape
    rows = cond.shape[0]
    tn = 1024 if n % 1024 == 0 else n
    return pl.pallas_call(
        _adaln_kernel,
        grid=(depth, n // tn),
        in_specs=[pl.BlockSpec((rows, d), lambda l, j: (0, 0)),
                  pl.BlockSpec((1, d, tn), lambda l, j: (l, 0, j)),
                  pl.BlockSpec((1, 1, tn), lambda l, j: (l, 0, j))],
        out_specs=pl.BlockSpec((1, rows, tn), lambda l, j: (l, 0, j)),
        out_shape=jax.ShapeDtypeStruct((depth, rows, n), F32),
        compiler_params=_cparams(("arbitrary", "arbitrary"), 2 * d * tn * 4 + 4 * rows * (d + tn) * 4),
        name="adaln",
    )(cond, w_ada, b_ada.reshape(depth, 1, n))


def _norm_mod_kernel(x_ref, g_ref, sh_ref, sc_ref, o_ref):
    o_ref[0] = _rms_mod(x_ref[0], g_ref[...], sh_ref[0], sc_ref[0]).astype(o_ref.dtype)


def _norm_mod(x, g, sh, sc, out_dtype, tm):
    b, t, d = x.shape
    return pl.pallas_call(
        _norm_mod_kernel,
        grid=(b, t // tm),
        in_specs=[pl.BlockSpec((1, tm, d), lambda i, j: (i, j, 0)),
                  pl.BlockSpec((1, d), lambda i, j: (0, 0)),
                  pl.BlockSpec((1, 1, d), lambda i, j: (i, 0, 0)),
                  pl.BlockSpec((1, 1, d), lambda i, j: (i, 0, 0))],
        out_specs=pl.BlockSpec((1, tm, d), lambda i, j: (i, j, 0)),
        out_shape=jax.ShapeDtypeStruct((b, t, d), out_dtype),
        compiler_params=_cparams(("parallel", "parallel"), 6 * tm * d * 4),
        name="norm_mod",
    )(x, g.reshape(1, d), sh, sc)


def _norm_mod_seq_kernel(n_ctx_tiles, c_ref, x_ref, g_ref, sh_ref, sc_ref, o_ref):
    j = pl.program_id(1)
    src = jnp.where(j < n_ctx_tiles, c_ref[0], x_ref[0])
    o_ref[0] = _rms_mod(src, g_ref[...], sh_ref[0], sc_ref[0]).astype(o_ref.dtype)


def _norm_mod_seq(ctx, x, g, sh_all, sc_all, out_dtype, tm):
    b, t, d = x.shape
    l = ctx.shape[1]
    nc, nx = l // tm, t // tm
    return pl.pallas_call(
        functools.partial(_norm_mod_seq_kernel, nc),
        grid=(b, nc + nx),
        in_specs=[pl.BlockSpec((1, tm, d), lambda i, j: (i, jnp.minimum(j, nc - 1), 0)),
                  pl.BlockSpec((1, tm, d), lambda i, j: (i, jnp.maximum(j - nc, 0), 0)),
                  pl.BlockSpec((1, d), lambda i, j: (0, 0)),
                  pl.BlockSpec((1, 1, d), lambda i, j: (jnp.where(j < nc, b, i), 0, 0)),
                  pl.BlockSpec((1, 1, d), lambda i, j: (jnp.where(j < nc, b, i), 0, 0))],
        out_specs=pl.BlockSpec((1, tm, d), lambda i, j: (i, j, 0)),
        out_shape=jax.ShapeDtypeStruct((b, l + t, d), out_dtype),
        compiler_params=_cparams(("parallel", "arbitrary"), 8 * tm * d * 4),
        name="norm_mod_seq",
    )(ctx, x, g.reshape(1, d), sh_all, sc_all)


def _win_matrix(length, k):
    pos = np.arange(length)
    lo = np.clip(pos - k // 2, 0, length - 1)
    hi = np.clip(pos + (k - k // 2 - 1), 0, length - 1)
    m = np.zeros((length, length), np.float32)
    for p in range(length):
        m[p, lo[p]:hi[p] + 1] = 1.0 / float(hi[p] - lo[p] + 1)
    return m, (1.0 / (hi - lo + 1)).astype(np.float32)


def _pool_tokens_kernel(k, w, slab, vertical, inv_ref, ph_ref, h_ref, o_ref, pad_ref):
    t, tc = h_ref.shape[1], h_ref.shape[2]
    if not vertical:
        for s in range(t // slab):
            hs = h_ref[0, s * slab:(s + 1) * slab, :]
            o_ref[0, s * slab:(s + 1) * slab, :] = (
                jnp.dot(ph_ref[...], hs, precision=HIGHEST, preferred_element_type=F32) - hs)
        return
    top = (k // 2) * w
    bot = (k - k // 2 - 1) * w
    pad_ref[0:top, :] = jnp.zeros((top, tc), F32)
    if bot:
        pad_ref[top + t:top + t + bot, :] = jnp.zeros((bot, tc), F32)
    for s in range(t // slab):
        hs = h_ref[0, s * slab:(s + 1) * slab, :]
        pad_ref[top + s * slab:top + (s + 1) * slab, :] = jnp.dot(
            ph_ref[...], hs, precision=HIGHEST, preferred_element_type=F32)

    def row(r, carry):
        base = pl.multiple_of(r * w, w)
        acc = pad_ref[pl.ds(base, w), :]
        for dr in range(1, k):
            acc = acc + pad_ref[pl.ds(pl.multiple_of(base + dr * w, w), w), :]
        o_ref[0, pl.ds(base, w), :] = acc * inv_ref[r] - h_ref[0, pl.ds(base, w), :]
        return carry

    lax.fori_loop(0, t // w, row, 0)


def _pool_tokens(h, group, gdim, k, grid_w, vertical):
    b, t, d = h.shape
    tc = min(gdim, 256)
    per = gdim // tc
    if vertical:
        slab = max(grid_w, min(256, t))
        mh, _ = _win_matrix(grid_w, k)
        ph = np.kron(np.eye(slab // grid_w, dtype=np.float32), mh)
        _, inv_v = _win_matrix(t // grid_w, k)
    else:
        slab = t
        ph, _ = _win_matrix(t, k)
        inv_v = np.ones((1,), np.float32)
    pad_rows = t + (k - 1) * grid_w if vertical else SUBLANES_V7X
    return pl.pallas_call(
        functools.partial(_pool_tokens_kernel, k, grid_w, slab, vertical),
        grid=(b, per),
        in_specs=[pl.BlockSpec(memory_space=pltpu.SMEM),
                  pl.BlockSpec((slab, slab), lambda i, j: (0, 0)),
                  pl.BlockSpec((1, t, tc), lambda i, j: (i, 0, group * per + j))],
        out_specs=pl.BlockSpec((1, t, tc), lambda i, j: (i, 0, j)),
        out_shape=jax.ShapeDtypeStruct((b, t, gdim), F32),
        scratch_shapes=[pltpu.VMEM((pad_rows, tc), F32)],
        compiler_params=_cparams(("parallel", "parallel"), (4 * t + pad_rows) * tc * 4 + 2 * slab * slab * 4),
        name=f"pool_tokens_k{k}",
    )(jnp.asarray(inv_v), jnp.asarray(ph), h)


def _pool_out_kernel(n_groups, *refs):
    d_refs = refs[:n_groups]
    x_ref, w_ref, ps_ref, gt_ref, o_ref = refs[n_groups:]
    ys = [jnp.dot(d_refs[j][0].astype(BF16), w_ref[j], preferred_element_type=F32) for j in range(n_groups)]
    y = jnp.concatenate(ys, axis=-1) * ps_ref[...]
    o_ref[0] = x_ref[0] + gt_ref[0] * y


def _pool_out(ds, x, w_pool_bf16, pool_scale, gate, tm):
    b, t, d = x.shape
    ng, gdim, _ = w_pool_bf16.shape
    return pl.pallas_call(
        functools.partial(_pool_out_kernel, ng),
        grid=(b, t // tm),
        in_specs=[pl.BlockSpec((1, tm, gdim), lambda i, j: (i, j, 0)) for _ in range(ng)] + [
            pl.BlockSpec((1, tm, d), lambda i, j: (i, j, 0)),
            pl.BlockSpec((ng, gdim, gdim), lambda i, j: (0, 0, 0)),
            pl.BlockSpec((1, d), lambda i, j: (0, 0)),
            pl.BlockSpec((1, 1, d), lambda i, j: (i, 0, 0))],
        out_specs=pl.BlockSpec((1, tm, d), lambda i, j: (i, j, 0)),
        out_shape=jax.ShapeDtypeStruct((b, t, d), F32),
        compiler_params=_cparams(("parallel", "parallel"), 8 * tm * d * 4 + 2 * ng * gdim * gdim * 2),
        name="pool_out",
    )(*ds, x, w_pool_bf16, pool_scale.reshape(1, d), gate)


def _pool_mixer(cfg, x, g, sh, sc, gate, w_pool_bf16, pool_scale, on_grid):
    b, t, d = x.shape
    tm = min(cfg.tok_tile, t)
    h = _norm_mod(x, g, sh, sc, F32, tm)
    gdim = d // len(cfg.pool_windows)
    ds = [_pool_tokens(h, j, gdim, k, cfg.grid_w, on_grid) for j, k in enumerate(cfg.pool_windows)]
    return _pool_out(ds, x, w_pool_bf16, pool_scale, gate, tm)


def _router_kernel(nx_tiles, has_ctx, *refs):
    if has_ctx:
        x_ref, c_ref, g_ref, sh_ref, sc_ref, wr_ref, br_ref, fx_ref, lg_ref = refs
        src = jnp.where(pl.program_id(0) < nx_tiles, x_ref[0], c_ref[0])
    else:
        x_ref, g_ref, sh_ref, sc_ref, wr_ref, br_ref, fx_ref, lg_ref = refs
        src = x_ref[0]
    h = _rms_mod(src, g_ref[...], sh_ref[0], sc_ref[0])
    fx_ref[...] = h
    lg_ref[...] = jnp.dot(h, wr_ref[...], precision=HIGHEST, preferred_element_type=F32) + br_ref[...]


def _router(x, ctx, g, sh_all, sc_all, w_r, b_r, tm):
    b, t, d = x.shape
    per_b = t // tm
    nx = b * per_b
    has_ctx = ctx is not None
    nc = b * (ctx.shape[1] // tm) if has_ctx else 0
    per_c = (ctx.shape[1] // tm) if has_ctx else 1
    n_all = (nx + nc) * tm
    lanes = w_r.shape[1]

    def mod_row(i):
        return jnp.where(i < nx, jnp.minimum(i, nx - 1) // per_b, b) if has_ctx else i // per_b

    in_specs = [pl.BlockSpec((1, tm, d), lambda i: (jnp.minimum(i, nx - 1) // per_b, jnp.minimum(i, nx - 1) % per_b, 0))]
    args = [x]
    if has_ctx:
        in_specs.append(pl.BlockSpec(
            (1, tm, d), lambda i: (jnp.maximum(i - nx, 0) // per_c, jnp.maximum(i - nx, 0) % per_c, 0)))
        args.append(ctx)
    in_specs += [pl.BlockSpec((1, d), lambda i: (0, 0)),
                 pl.BlockSpec((1, 1, d), lambda i: (mod_row(i), 0, 0)),
                 pl.BlockSpec((1, 1, d), lambda i: (mod_row(i), 0, 0)),
                 pl.BlockSpec((d, lanes), lambda i: (0, 0)),
                 pl.BlockSpec((1, lanes), lambda i: (0, 0))]
    args += [g.reshape(1, d), sh_all, sc_all, w_r, b_r]
    return pl.pallas_call(
        functools.partial(_router_kernel, nx, has_ctx),
        grid=(nx + nc,),
        in_specs=in_specs,
        out_specs=[pl.BlockSpec((tm, d), lambda i: (i, 0)), pl.BlockSpec((tm, lanes), lambda i: (i, 0))],
        out_shape=[jax.ShapeDtypeStruct((n_all, d), F32), jax.ShapeDtypeStruct((n_all, lanes), F32)],
        compiler_params=_cparams(("arbitrary",), 8 * tm * d * 4 + 2 * d * lanes * 4),
        name="moe_router",
    )(*args)


def _expert_kernel(rows, n_all, col, be_ref, nu_ref, code_ref, fx_hbm, wgu_ref, wdn_ref, out_hbm,
                   xbuf, ybuf, gsem, ssem):
    i = pl.program_id(0)
    n_used = nu_ref[0]
    slot = i % 2
    other = 1 - slot
    n_slots = n_all * 2
    last_blk = code_ref.shape[0] // rows - 1
    d, f = wdn_ref.shape[2], wdn_ref.shape[1]

    def gather_row(blk, sl, r):
        v = code_ref[blk * rows + r]
        src = jnp.where(v >= n_slots, 0, v >> 1)
        return pltpu.make_async_copy(fx_hbm.at[pl.ds(src, 1)], xbuf.at[sl, pl.ds(r, 1)], gsem.at[sl])

    def scatter_row(blk, sl, r, none):
        v = jnp.where(none, n_slots, code_ref[blk * rows + r])
        dst = jnp.where(v >= n_slots, n_slots + sl * rows + r, (v & 1) * n_all + (v >> 1))
        return pltpu.make_async_copy(ybuf.at[sl, pl.ds(r, 1)], out_hbm.at[pl.ds(dst, 1)], ssem.at[sl])

    def wait_gather(sl):
        pltpu.make_async_copy(fx_hbm.at[pl.ds(0, rows)], xbuf.at[sl], gsem.at[sl]).wait()

    def wait_scatter(sl):
        pltpu.make_async_copy(ybuf.at[sl], out_hbm.at[pl.ds(0, rows)], ssem.at[sl]).wait()

    @pl.when(i == 0)
    def _():
        ybuf[...] = jnp.zeros(ybuf.shape, F32)
        for sl in range(2):
            pltpu.make_async_copy(ybuf.at[sl], out_hbm.at[pl.ds(n_slots + sl * rows, rows)], ssem.at[sl]).start()
        for sl in range(2):
            wait_scatter(sl)

        @pl.when(n_used > 0)
        def _():
            def body(r, c):
                gather_row(0, 0, r).start()
                return c
            lax.fori_loop(0, rows, body, 0)

    @pl.when(i < n_used)
    def _():
        wait_gather(slot)

        @pl.when(i >= 1)
        def _():
            wait_scatter(slot)

        nxt = jnp.minimum(i + 1, last_blk)
        prev = jnp.maximum(i - 1, 0)
        no_prev = i == 0
        xb = xbuf[slot].astype(BF16)
        n1 = f // col
        per1 = rows // n1
        acts = []
        for j in range(n1):
            g = jnp.dot(xb, wgu_ref[0, :, j * col:(j + 1) * col], preferred_element_type=F32)
            u = jnp.dot(xb, wgu_ref[0, :, f + j * col:f + (j + 1) * col], preferred_element_type=F32)
            acts.append((_silu(g) * u).astype(BF16))
            for r in range(j * per1, (j + 1) * per1):
                gather_row(nxt, other, r).start()
        act = jnp.concatenate(acts, axis=-1)
        n2 = d // col
        per2 = rows // n2
        for j in range(n2):
            ybuf[slot, :, j * col:(j + 1) * col] = jnp.dot(act, wdn_ref[0, :, j * col:(j + 1) * col],
                                                           preferred_element_type=F32)
            for r in range(j * per2, (j + 1) * per2):
                scatter_row(prev, other, r, no_prev).start()

    @pl.when(jnp.logical_and(i == n_used, i >= 1))
    def _():
        wait_gather(slot)

        def body(r, c):
            scatter_row(i - 1, other, r, False).start()
            return c
        lax.fori_loop(0, rows, body, 0)
        wait_scatter(other)
        wait_scatter(slot)


def _experts(cfg, fx, block_e, n_used, row_code, w_gu_bf16, w_dn_bf16):
    n_all, d = fx.shape
    e, _, f2 = w_gu_bf16.shape
    f = f2 // 2
    rows = cfg.moe_rows
    col = min(256, f)
    n_steps = block_e.shape[0]
    grid_spec = pltpu.PrefetchScalarGridSpec(
        num_scalar_prefetch=3,
        grid=(n_steps,),
        in_specs=[pl.BlockSpec(memory_space=pl.ANY),
                  pl.BlockSpec((1, d, f2), lambda i, be, nu, cd: (be[i], 0, 0)),
                  pl.BlockSpec((1, f, d), lambda i, be, nu, cd: (be[i], 0, 0))],
        out_specs=pl.BlockSpec(memory_space=pl.ANY),
        scratch_shapes=[pltpu.VMEM((2, rows, d), F32), pltpu.VMEM((2, rows, d), F32),
                        pltpu.SemaphoreType.DMA((2,)), pltpu.SemaphoreType.DMA((2,))])
    vmem = 2 * (d * f2 + f * d) * 2 + 4 * rows * d * 4 + 3 * rows * f2 * 4
    return pl.pallas_call(
        functools.partial(_expert_kernel, rows, n_all, col),
        grid_spec=grid_spec,
        out_shape=jax.ShapeDtypeStruct((cfg.top_k * n_all + 2 * rows, d), F32),
        compiler_params=_cparams(("arbitrary",), vmem),
        name="moe_experts",
    )(block_e, n_used, row_code, fx, w_gu_bf16, w_dn_bf16)


def _combine_kernel(final, *refs):
    if final:
        x_ref, a_ref, b_ref, w_ref, gt_ref, gf_ref, o_ref = refs
    else:
        x_ref, a_ref, b_ref, w_ref, gt_ref, o_ref = refs
    w = w_ref[...]
    y = x_ref[0] + gt_ref[0] * (w[:, 0:1] * a_ref[...] + w[:, 1:2] * b_ref[...])
    if final:
        y = y * lax.rsqrt(jnp.mean(y * y, axis=-1, keepdims=True) + EPS) * gf_ref[...]
    o_ref[0] = y


def _combine(x, slots, wts, tok_off, n_all, gate, tm, norm_final=None):
    b, t, d = x.shape
    per_b = t // tm
    off0 = tok_off // tm
    off1 = (n_all + tok_off) // tm
    final = norm_final is not None
    in_specs = [pl.BlockSpec((1, tm, d), lambda i, j: (i, j, 0)),
                pl.BlockSpec((tm, d), lambda i, j: (off0 + i * per_b + j, 0)),
                pl.BlockSpec((tm, d), lambda i, j: (off1 + i * per_b + j, 0)),
                pl.BlockSpec((tm, wts.shape[1]), lambda i, j: (off0 + i * per_b + j, 0)),
                pl.BlockSpec((1, 1, d), lambda i, j: (i, 0, 0))]
    args = [x, slots, slots, wts, gate]
    if final:
        in_specs.append(pl.BlockSpec((1, d), lambda i, j: (0, 0)))
        args.append(norm_final.reshape(1, d))
    return pl.pallas_call(
        functools.partial(_combine_kernel, final),
        grid=(b, per_b),
        in_specs=in_specs,
        out_specs=pl.BlockSpec((1, tm, d), lambda i, j: (i, j, 0)),
        out_shape=jax.ShapeDtypeStruct((b, t, d), F32),
        compiler_params=_cparams(("parallel", "parallel"), 10 * tm * d * 4),
        name="moe_combine",
    )(*args)


def _route(cfg, logits):
    n_all = logits.shape[0]
    ng, eg, k = cfg.n_groups, cfg.experts_per_group, cfg.top_k
    assert k == 2, "row codes pack (token, choice) as 2 * token + choice"
    n_exp = ng * eg
    rows = cfg.moe_rows
    p_grp = jax.nn.softmax(logits[:, :ng], axis=-1)
    g_idx = jnp.argmax(p_grp, axis=-1)
    p_g = jnp.max(p_grp, axis=-1)
    le = logits[:, ng:ng + n_exp].reshape(n_all, ng, eg)
    le = jnp.take_along_axis(le, g_idx[:, None, None], axis=1)[:, 0]
    vals, e_in = lax.top_k(jax.nn.softmax(le, axis=-1), k)
    wts = p_g[:, None] * vals / jnp.sum(vals, axis=-1, keepdims=True)
    eid = (g_idx[:, None] * eg + e_in).astype(jnp.int32)

    s = n_all * k
    e_flat = eid.reshape(-1)
    onehot = (e_flat[:, None] == jnp.arange(n_exp, dtype=jnp.int32)[None, :]).astype(jnp.int32)
    csum = jnp.cumsum(onehot, axis=0)
    rank = jnp.sum((csum - onehot) * onehot, axis=1)
    counts = csum[-1]
    padded = ((counts + rows - 1) // rows) * rows
    pends = jnp.cumsum(padded)
    pstarts = pends - padded
    dest = pstarts[e_flat] + rank
    n_blocks = -(-(s + n_exp * (rows - 1)) // rows)
    p = n_blocks * rows
    row_code = jnp.full((p,), s, jnp.int32).at[dest].set(jnp.arange(s, dtype=jnp.int32))
    blk_start = jnp.arange(n_blocks + 1, dtype=jnp.int32) * rows
    block_e = jnp.minimum(jnp.sum((pends[None, :] <= blk_start[:, None]).astype(jnp.int32), axis=1), n_exp - 1)
    n_used = (pends[-1] // rows).astype(jnp.int32).reshape(1)
    return (block_e, n_used, row_code), wts


def _moe(cfg, x, ctx, g, sh_all, sc_all, w_r, b_r, w_gu_bf16, w_dn_bf16):
    b, t, d = x.shape
    tm = min(cfg.tok_tile, t) if ctx is None else min(cfg.tok_tile, t, ctx.shape[1])
    fx, logits = _router(x, ctx, g, sh_all, sc_all, w_r, b_r, tm)
    n_all = fx.shape[0]
    plan, wts = _route(cfg, logits)
    slots = _experts(cfg, fx, *plan, w_gu_bf16, w_dn_bf16)
    return slots, wts, n_all


def _proj_kernel(kind, n_heads, hd, *refs):
    if kind == "gate":
        h_ref, w_ref, la_ref, lc_ref, o_ref = refs
    else:
        h_ref, w_ref, o_ref = refs
    z = jnp.dot(h_ref[0], w_ref[...], preferred_element_type=F32)
    if kind == "silu":
        z = _silu(z)
    elif kind == "gate":
        ls = jnp.minimum(z, 0.0) - jnp.log1p(jnp.exp(-jnp.abs(z)))
        c = lc_ref[...] + ls
        a = jnp.broadcast_to(la_ref[...], c.shape)
        m = jnp.maximum(a, c)
        z = m + jnp.log1p(jnp.exp(-jnp.abs(a - c)))
    for h in range(n_heads):
        o_ref[0, h] = z[:, h * hd:(h + 1) * hd].astype(o_ref.dtype)


def _proj(cfg, h_all, w_bf16, kind, out_dtype, log_lb=None, log1m_lb=None):
    b, s, d = h_all.shape
    n = w_bf16.shape[1]
    hd = cfg.head_dim
    nh = n // hd
    tm = cfg.tok_tile
    in_specs = [pl.BlockSpec((1, tm, d), lambda i, j: (i, j, 0)),
                pl.BlockSpec((d, n), lambda i, j: (0, 0))]
    args = [h_all, w_bf16]
    if kind == "gate":
        in_specs += [pl.BlockSpec((1, n), lambda i, j: (0, 0)), pl.BlockSpec((1, n), lambda i, j: (0, 0))]
        args += [log_lb.reshape(1, n), log1m_lb.reshape(1, n)]
    return pl.pallas_call(
        functools.partial(_proj_kernel, kind, nh, hd),
        grid=(b, s // tm),
        in_specs=in_specs,
        out_specs=pl.BlockSpec((1, nh, tm, hd), lambda i, j: (i, 0, j, 0)),
        out_shape=jax.ShapeDtypeStruct((b, nh, s, hd), out_dtype),
        compiler_params=_cparams(("parallel", "parallel"), 2 * d * n * 2 + 2 * tm * d * 2 + 6 * tm * n * 4),
        name=f"hgrn_proj_{kind}",
    )(*args)


def _gla_kernel(chunk, reverse, readout, limit, *refs):
    if readout:
        q_ref, lf_ref, v_ref, of_ref, gs_ref, ng_ref, o_ref, st_ref, b_ref, q32_ref, at_ref = refs
    else:
        q_ref, lf_ref, v_ref, o_ref, st_ref, b_ref, q32_ref, at_ref = refs
    hg, tb, hd = q_ref.shape[1], q_ref.shape[2], q_ref.shape[3]
    n_chunks = tb // chunk
    shift = chunk.bit_length() - 1

    @pl.when(pl.program_id(2) == 0)
    def _():
        st_ref[...] = jnp.zeros(st_ref.shape, F32)

    rr = lax.broadcasted_iota(jnp.int32, (tb, tb), 0)
    cc = lax.broadcasted_iota(jnp.int32, (tb, tb), 1)
    same = (rr >> shift) == (cc >> shift)
    tri = jnp.where(jnp.logical_and(same, (cc >= rr) if reverse else (cc <= rr)), 1.0, 0.0).astype(F32)
    ar = lax.broadcasted_iota(jnp.int32, (chunk, chunk), 0)
    ac = lax.broadcasted_iota(jnp.int32, (chunk, chunk), 1)
    at_mask = (ar >= ac) if reverse else (ar <= ac)
    row_id = lax.broadcasted_iota(jnp.int32, (chunk, hd), 0)
    order = list(range(n_chunks))[::-1] if reverse else list(range(n_chunks))
    edge = 0 if reverse else chunk - 1

    def head(h, carry):
        lf = lf_ref[0, h]
        b = jnp.dot(tri, lf, precision=HIGHEST, preferred_element_type=F32)
        b_ref[...] = b
        btot = [b[c * chunk + edge:c * chunk + edge + 1, :] for c in range(n_chunks)]
        worst = functools.reduce(jnp.minimum, btot)
        fast = jnp.min(worst) >= -limit
        kk = 1.0 - jnp.exp(lf)
        qf = q_ref[0, h].astype(F32)
        qt = qf * jnp.exp(b)

        @pl.when(fast)
        def _():
            for c in range(n_chunks):
                sl = slice(c * chunk, (c + 1) * chunk)
                d = b[sl] - 0.5 * btot[c]
                qh = (qf[sl] * jnp.exp(d)).astype(BF16)
                kh = (kk[sl] * jnp.exp(-d)).astype(BF16)
                at = lax.dot_general(kh, qh, (((1,), (1,)), ((), ())), preferred_element_type=F32)
                at_ref[c] = jnp.where(at_mask, at, 0.0)

        @pl.when(jnp.logical_not(fast))
        def _():
            q32_ref[...] = qf
            for c in range(n_chunks):
                c0 = c * chunk
                bc = b[c0:c0 + chunk]
                kc = kk[c0:c0 + chunk]

                def tstep(t, at):
                    bt = b_ref[pl.ds(c0 + t, 1), :]
                    qrow = q32_ref[pl.ds(c0 + t, 1), :]
                    valid = (row_id >= t) if reverse else (row_id <= t)
                    e = jnp.exp(jnp.where(valid, bt - bc, -jnp.inf))
                    col = jnp.sum(e * kc * qrow, axis=-1, keepdims=True)
                    return at + col * jnp.where(ac == t, 1.0, 0.0)

                at_ref[c] = lax.fori_loop(0, chunk, tstep, jnp.zeros((chunk, chunk), F32))

        qb = qt.astype(BF16)
        vb = v_ref[0, h]
        st = st_ref[h]
        for c in order:
            sl = slice(c * chunk, (c + 1) * chunk)
            k2 = (kk[sl] * jnp.exp(btot[c] - b[sl])).astype(BF16)
            o = lax.dot_general(at_ref[c].astype(BF16), vb[sl], (((0,), (0,)), ((), ())),
                                preferred_element_type=F32)
            o = o + lax.dot_general(qb[sl], st.astype(BF16), (((1,), (1,)), ((), ())),
                                    preferred_element_type=F32)
            st = st * jnp.exp(btot[c]) + lax.dot_general(vb[sl], k2, (((0,), (0,)), ((), ())),
                                                         preferred_element_type=F32)
            if readout:
                o = o + of_ref[0, h, sl, :]
                o = o * lax.rsqrt(jnp.mean(o * o, axis=-1, keepdims=True) + EPS) * ng_ref[...]
                o_ref[0, h, sl, :] = (o * gs_ref[0, h, sl, :].astype(F32)).astype(o_ref.dtype)
            else:
                o_ref[0, h, sl, :] = o
        st_ref[h] = st
        return carry

    lax.fori_loop(0, hg, head, 0)


def _gla(cfg, qs, lf, v, reverse, n_ctx_blocks, of=None, gs=None, norm_g=None):
    b, nh, s, hd = qs.shape
    tb = cfg.tok_tile
    hg = nh
    nblk = s // tb
    readout = of is not None

    if reverse:
        def blk(j):
            return jnp.where(j < n_ctx_blocks, n_ctx_blocks - 1 - j, nblk - 1 - (j - n_ctx_blocks))
    else:
        def blk(j):
            return j

    spec = pl.BlockSpec((1, hg, tb, hd), lambda i, g, j: (i, g, blk(j), 0))
    in_specs = [spec, spec, spec]
    args = [qs, lf, v]
    if readout:
        in_specs += [spec, spec, pl.BlockSpec((1, hd), lambda i, g, j: (0, 0))]
        args += [of, gs, norm_g.reshape(1, hd)]
    out_dtype = BF16 if readout else F32
    vmem = 2 * hg * tb * hd * (2 + 4 + 2 + 4 + 2 + 4) + hg * hd * hd * 4 + 8 * tb * tb * 4 + 16 * tb * hd * 4
    return pl.pallas_call(
        functools.partial(_gla_kernel, cfg.gla_chunk, reverse, readout, cfg.fast_decay_limit),
        grid=(b, nh // hg, nblk),
        in_specs=in_specs,
        out_specs=spec,
        out_shape=jax.ShapeDtypeStruct((b, nh, s, hd), out_dtype),
        scratch_shapes=[pltpu.VMEM((hg, hd, hd), F32), pltpu.VMEM((tb, hd), F32), pltpu.VMEM((tb, hd), F32),
                        pltpu.VMEM((tb // cfg.gla_chunk, cfg.gla_chunk, cfg.gla_chunk), F32)],
        compiler_params=_cparams(("parallel", "parallel", "arbitrary"), vmem),
        name="hgrn_scan_bwd" if reverse else "hgrn_scan_fwd",
    )(*args)


def _out_proj_kernel(n_heads, y_ref, x_ref, w_ref, gt_ref, o_ref):
    y = jnp.concatenate([y_ref[0, h] for h in range(n_heads)], axis=-1)
    o_ref[0] = x_ref[0] + gt_ref[0] * jnp.dot(y, w_ref[...], preferred_element_type=F32)


def _out_proj(cfg, y_heads, x, w_out_bf16, gate, ctx_tiles):
    b, t, d = x.shape
    nh, hd = y_heads.shape[1], y_heads.shape[3]
    tm = cfg.tok_tile
    return pl.pallas_call(
        functools.partial(_out_proj_kernel, nh),
        grid=(b, t // tm),
        in_specs=[pl.BlockSpec((1, nh, tm, hd), lambda i, j: (i, 0, j + ctx_tiles, 0)),
                  pl.BlockSpec((1, tm, d), lambda i, j: (i, j, 0)),
                  pl.BlockSpec((d, d), lambda i, j: (0, 0)),
                  pl.BlockSpec((1, 1, d), lambda i, j: (i, 0, 0))],
        out_specs=pl.BlockSpec((1, tm, d), lambda i, j: (i, j, 0)),
        out_shape=jax.ShapeDtypeStruct((b, t, d), F32),
        compiler_params=_cparams(("parallel", "parallel"), 2 * d * d * 2 + 8 * tm * d * 4),
        name="hgrn_out_proj",
    )(y_heads, x, w_out_bf16, gate)


def _hgrn_mixer(cfg, x, ctx, g, sh_all, sc_all, gate, w_in, norm_g, w_out, lb):
    b, t, d = x.shape
    tm = cfg.tok_tile
    h_all = _norm_mod_seq(ctx, x, g, sh_all, sc_all, BF16, tm)
    w = w_in.astype(BF16)
    log_lb = jnp.log(lb)
    log1m_lb = jnp.log1p(-lb)
    qs = _proj(cfg, h_all, w[:, 0 * d:1 * d], "silu", BF16)
    gs = _proj(cfg, h_all, w[:, 1 * d:2 * d], "silu", BF16)
    lf = _proj(cfg, h_all, w[:, 2 * d:3 * d], "gate", F32, log_lb[0], log1m_lb[0])
    lr = _proj(cfg, h_all, w[:, 3 * d:4 * d], "gate", F32, log_lb[1], log1m_lb[1])
    v = _proj(cfg, h_all, w[:, 4 * d:5 * d], "none", BF16)
    n_ctx_blocks = ctx.shape[1] // tm
    of = _gla(cfg, qs, lf, v, False, n_ctx_blocks)
    y = _gla(cfg, qs, lr, v, True, n_ctx_blocks, of, gs, norm_g)
    return _out_proj(cfg, y, x, w_out.astype(BF16), gate, n_ctx_blocks)


def _forward(cfg, x, c, ctx, c_ctx, norm_mix, norm_ffn, w_ada, b_ada, pool_w, pool_scale, hgrn_w_in, hgrn_norm,
             hgrn_w_out, hgrn_lb_logits, router_w_group, router_b_group, router_w_expert, router_b_expert,
             moe_w_gate_up, moe_w_down, norm_final):
    b, t, d = x.shape
    depth = w_ada.shape[0]
    n_mixers = 2
    p_lb = jax.nn.softmax(hgrn_lb_logits.astype(F32), axis=0)
    lb_all = jnp.cumsum(p_lb, axis=0) - p_lb[0]

    rows = -(-(b + 1) // SUBLANES_V7X) * SUBLANES_V7X
    cond = jnp.zeros((rows, d), F32).at[:b].set(c).at[b].set(c_ctx)
    mods = _adaln(cond, w_ada, b_ada)

    ng, ne = cfg.n_groups, cfg.n_groups * cfg.experts_per_group
    for i in range(depth):
        ctx_needed = i < depth - 1
        j = i // n_mixers
        m = [mods[i, :, q * d:(q + 1) * d].reshape(rows, 1, d) for q in range(6)]
        sh_m, sc_m, gt_m, sh_f, sc_f, gt_f = m
        ctx_rows = lambda a: jnp.broadcast_to(a[b:b + 1], (b, 1, d))
        if i % n_mixers == 0:
            wp = pool_w[j].astype(BF16)
            x_new = _pool_mixer(cfg, x, norm_mix[i], sh_m, sc_m, gt_m, wp, pool_scale[j], True)
            if ctx_needed:
                ctx = _pool_mixer(cfg, ctx, norm_mix[i], ctx_rows(sh_m), ctx_rows(sc_m), ctx_rows(gt_m), wp,
                                  pool_scale[j], False)
            x = x_new
        else:
            x = _hgrn_mixer(cfg, x, ctx, norm_mix[i], sh_m, sc_m, gt_m, hgrn_w_in[j], hgrn_norm[j],
                            hgrn_w_out[j], lb_all[i])
        lanes = -(-(ng + ne) // LANES_V7X) * LANES_V7X
        w_r = jnp.zeros((d, lanes), F32).at[:, :ng].set(router_w_group[i]).at[:, ng:ng + ne].set(router_w_expert[i])
        b_r = jnp.zeros((1, lanes), F32).at[0, :ng].set(router_b_group[i]).at[0, ng:ng + ne].set(router_b_expert[i])
        slots, wts, n_all = _moe(cfg, x, ctx if ctx_needed else None, norm_ffn[i], sh_f, sc_f, w_r, b_r,
                                 moe_w_gate_up[i].astype(BF16), moe_w_down[i].astype(BF16))
        tm = min(cfg.tok_tile, t)
        last = i == depth - 1
        x_next = _combine(x, slots, wts, 0, n_all, gt_f, tm, norm_final if last else None)
        if ctx_needed:
            ctx = _combine(ctx, slots, wts, b * t, n_all, ctx_rows(gt_f), min(cfg.tok_tile, ctx.shape[1]))
        x = x_next
    return x


def kernel(x, c, ctx, c_ctx, norm_mix, norm_ffn, w_ada, b_ada, pool_w, pool_scale, hgrn_w_in, hgrn_norm, hgrn_w_out, hgrn_lb_logits, router_w_group, router_b_group, router_w_expert, router_b_expert, moe_w_gate_up, moe_w_down, norm_final):
    return _forward(Cfg(), x, c, ctx, c_ctx, norm_mix, norm_ffn, w_ada, b_ada, pool_w, pool_scale, hgrn_w_in,
                    hgrn_norm, hgrn_w_out, hgrn_lb_logits, router_w_group, router_b_group, router_w_expert,
                    router_b_expert, moe_w_gate_up, moe_w_down, norm_final)
```

```python
import functools
from typing import NamedTuple

import numpy as np
import jax
import jax.numpy as jnp
from jax import lax
from jax.experimental import pallas as pl
from jax.experimental.pallas import tpu as pltpu

F32 = jnp.float32
BF16 = jnp.bfloat16
HIGHEST = lax.Precision.HIGHEST
EPS = 1e-6

LANES_V7X = 128
SUBLANES_V7X = 8
VMEM_BYTES_V7X = 64 * 1024 * 1024


class Cfg(NamedTuple):
    grid_w: int = 64
    pool_windows: tuple = (2, 4, 8, 16)
    head_dim: int = LANES_V7X
    n_groups: int = 4
    experts_per_group: int = 8
    top_k: int = 2
    moe_rows: int = 256
    gla_chunk: int = 64
    gla_group: int = 2
    tok_tile: int = 256
    fast_decay_limit: float = 120.0


def _vmem_limit(nbytes):
    return int(min(max(nbytes * 5 // 4, 16 * 1024 * 1024), VMEM_BYTES_V7X - 6 * 1024 * 1024))


def _cparams(sem, vmem_bytes):
    return pltpu.CompilerParams(dimension_semantics=sem, vmem_limit_bytes=_vmem_limit(vmem_bytes))


def _rms_mod(x, g, shift, scale):
    y = x * lax.rsqrt(jnp.mean(x * x, axis=-1, keepdims=True) + EPS) * g
    return y * (1.0 + scale) + shift


def _silu(x):
    return x * jax.nn.sigmoid(x)


def _store_slabs(ref, val):
    sub = ref.shape[-1]
    for s in range(ref.shape[-2]):
        ref[:, s, :] = val[:, s * sub:(s + 1) * sub]


def _load_slabs(ref):
    return jnp.concatenate([ref[:, s, :] for s in range(ref.shape[-2])], axis=-1)


def _adaln_kernel(cond_ref, w_ref, b_ref, o_ref):
    s = _silu(cond_ref[...])
    o_ref[0] = jnp.dot(s, w_ref[0], precision=HIGHEST, preferred_element_type=F32) + b_ref[0]


def _adaln(cond, w_ada, b_ada):
    depth, d, n = w_ada.shape
    rows = cond.shape[0]
    tn = 1024 if n % 1024 == 0 else n
    return pl.pallas_call(
        _adaln_kernel,
        grid=(depth, n // tn),
        in_specs=[pl.BlockSpec((rows, d), lambda l, j: (0, 0)),
                  pl.BlockSpec((1, d, tn), lambda l, j: (l, 0, j)),
                  pl.BlockSpec((1, 1, tn), lambda l, j: (l, 0, j))],
        out_specs=pl.BlockSpec((1, rows, tn), lambda l, j: (l, 0, j)),
        out_shape=jax.ShapeDtypeStruct((depth, rows, n), F32),
        compiler_params=_cparams(("arbitrary", "arbitrary"), 2 * d * tn * 4 + 4 * rows * (d + tn) * 4),
        name="adaln",
    )(cond, w_ada, b_ada.reshape(depth, 1, n))


def _norm_mod_kernel(x_ref, g_ref, sh_ref, sc_ref, o_ref):
    o_ref[0] = _rms_mod(x_ref[0], g_ref[...], sh_ref[0], sc_ref[0]).astype(o_ref.dtype)


def _norm_mod(x, g, sh, sc, out_dtype, tm):
    b, t, d = x.shape
    return pl.pallas_call(
        _norm_mod_kernel,
        grid=(b, t // tm),
        in_specs=[pl.BlockSpec((1, tm, d), lambda i, j: (i, j, 0)),
                  pl.BlockSpec((1, d), lambda i, j: (0, 0)),
                  pl.BlockSpec((1, 1, d), lambda i, j: (i, 0, 0)),
                  pl.BlockSpec((1, 1, d), lambda i, j: (i, 0, 0))],
        out_specs=pl.BlockSpec((1, tm, d), lambda i, j: (i, j, 0)),
        out_shape=jax.ShapeDtypeStruct((b, t, d), out_dtype),
        compiler_params=_cparams(("parallel", "parallel"), 6 * tm * d * 4),
        name="norm_mod",
    )(x, g.reshape(1, d), sh, sc)


def _norm_mod_seq_kernel(n_ctx_tiles, c_ref, x_ref, g_ref, sh_ref, sc_ref, o_ref):
    j = pl.program_id(1)
    src = jnp.where(j < n_ctx_tiles, c_ref[0], x_ref[0])
    o_ref[0] = _rms_mod(src, g_ref[...], sh_ref[0], sc_ref[0]).astype(o_ref.dtype)


def _norm_mod_seq(ctx, x, g, sh_all, sc_all, out_dtype, tm):
    b, t, d = x.shape
    l = ctx.shape[1]
    nc, nx = l // tm, t // tm
    return pl.pallas_call(
        functools.partial(_norm_mod_seq_kernel, nc),
        grid=(b, nc + nx),
        in_specs=[pl.BlockSpec((1, tm, d), lambda i, j: (i, jnp.minimum(j, nc - 1), 0)),
                  pl.BlockSpec((1, tm, d), lambda i, j: (i, jnp.maximum(j - nc, 0), 0)),
                  pl.BlockSpec((1, d), lambda i, j: (0, 0)),
                  pl.BlockSpec((1, 1, d), lambda i, j: (jnp.where(j < nc, b, i), 0, 0)),
                  pl.BlockSpec((1, 1, d), lambda i, j: (jnp.where(j < nc, b, i), 0, 0))],
        out_specs=pl.BlockSpec((1, tm, d), lambda i, j: (i, j, 0)),
        out_shape=jax.ShapeDtypeStruct((b, l + t, d), out_dtype),
        compiler_params=_cparams(("parallel", "arbitrary"), 8 * tm * d * 4),
        name="norm_mod_seq",
    )(ctx, x, g.reshape(1, d), sh_all, sc_all)


def _win_matrix(length, k):
    pos = np.arange(length)
    lo = np.clip(pos - k // 2, 0, length - 1)
    hi = np.clip(pos + (k - k // 2 - 1), 0, length - 1)
    m = np.zeros((length, length), np.float32)
    for p in range(length):
        m[p, lo[p]:hi[p] + 1] = 1.0 / float(hi[p] - lo[p] + 1)
    return m, (1.0 / (hi - lo + 1)).astype(np.float32)


def _pool_tokens_kernel(k, w, slab, vertical, inv_ref, ph_ref, h_ref, o_ref, pad_ref):
    t, tc = h_ref.shape[1], h_ref.shape[2]
    if not vertical:
        for s in range(t // slab):
            hs = h_ref[0, s * slab:(s + 1) * slab, :]
            o_ref[0, s * slab:(s + 1) * slab, :] = (
                jnp.dot(ph_ref[...], hs, precision=HIGHEST, preferred_element_type=F32) - hs)
        return
    top = (k // 2) * w
    bot = (k - k // 2 - 1) * w
    pad_ref[0:top, :] = jnp.zeros((top, tc), F32)
    if bot:
        pad_ref[top + t:top + t + bot, :] = jnp.zeros((bot, tc), F32)
    for s in range(t // slab):
        hs = h_ref[0, s * slab:(s + 1) * slab, :]
        pad_ref[top + s * slab:top + (s + 1) * slab, :] = jnp.dot(
            ph_ref[...], hs, precision=HIGHEST, preferred_element_type=F32)

    def row(r, carry):
        base = pl.multiple_of(r * w, w)
        acc = pad_ref[pl.ds(base, w), :]
        for dr in range(1, k):
            acc = acc + pad_ref[pl.ds(pl.multiple_of(base + dr * w, w), w), :]
        o_ref[0, pl.ds(base, w), :] = acc * inv_ref[r] - h_ref[0, pl.ds(base, w), :]
        return carry

    lax.fori_loop(0, t // w, row, 0)


def _pool_tokens(h, group, gdim, k, grid_w, vertical):
    b, t, d = h.shape
    tc = min(gdim, 256)
    per = gdim // tc
    if vertical:
        slab = max(grid_w, min(256, t))
        mh, _ = _win_matrix(grid_w, k)
        ph = np.kron(np.eye(slab // grid_w, dtype=np.float32), mh)
        _, inv_v = _win_matrix(t // grid_w, k)
    else:
        slab = t
        ph, _ = _win_matrix(t, k)
        inv_v = np.ones((1,), np.float32)
    pad_rows = t + (k - 1) * grid_w if vertical else SUBLANES_V7X
    return pl.pallas_call(
        functools.partial(_pool_tokens_kernel, k, grid_w, slab, vertical),
        grid=(b, per),
        in_specs=[pl.BlockSpec(memory_space=pltpu.SMEM),
                  pl.BlockSpec((slab, slab), lambda i, j: (0, 0)),
                  pl.BlockSpec((1, t, tc), lambda i, j: (i, 0, group * per + j))],
        out_specs=pl.BlockSpec((1, t, tc), lambda i, j: (i, 0, j)),
        out_shape=jax.ShapeDtypeStruct((b, t, gdim), F32),
        scratch_shapes=[pltpu.VMEM((pad_rows, tc), F32)],
        compiler_params=_cparams(("parallel", "parallel"), (4 * t + pad_rows) * tc * 4 + 2 * slab * slab * 4),
        name=f"pool_tokens_k{k}",
    )(jnp.asarray(inv_v), jnp.asarray(ph), h)


def _pool_out_kernel(n_groups, *refs):
    d_refs = refs[:n_groups]
    x_ref, w_ref, ps_ref, gt_ref, o_ref = refs[n_groups:]
    ys = [jnp.dot(d_refs[j][0].astype(BF16), w_ref[j], preferred_element_type=F32) for j in range(n_groups)]
    y = jnp.concatenate(ys, axis=-1) * ps_ref[...]
    o_ref[0] = x_ref[0] + gt_ref[0] * y


def _pool_out(ds, x, w_pool_bf16, pool_scale, gate, tm):
    b, t, d = x.shape
    ng, gdim, _ = w_pool_bf16.shape
    return pl.pallas_call(
        functools.partial(_pool_out_kernel, ng),
        grid=(b, t // tm),
        in_specs=[pl.BlockSpec((1, tm, gdim), lambda i, j: (i, j, 0)) for _ in range(ng)] + [
            pl.BlockSpec((1, tm, d), lambda i, j: (i, j, 0)),
            pl.BlockSpec((ng, gdim, gdim), lambda i, j: (0, 0, 0)),
            pl.BlockSpec((1, d), lambda i, j: (0, 0)),
            pl.BlockSpec((1, 1, d), lambda i, j: (i, 0, 0))],
        out_specs=pl.BlockSpec((1, tm, d), lambda i, j: (i, j, 0)),
        out_shape=jax.ShapeDtypeStruct((b, t, d), F32),
        compiler_params=_cparams(("parallel", "parallel"), 8 * tm * d * 4 + 2 * ng * gdim * gdim * 2),
        name="pool_out",
    )(*ds, x, w_pool_bf16, pool_scale.reshape(1, d), gate)


def _pool_mixer(cfg, x, g, sh, sc, gate, w_pool_bf16, pool_scale, on_grid):
    b, t, d = x.shape
    tm = min(cfg.tok_tile, t)
    h = _norm_mod(x, g, sh, sc, F32, tm)
    gdim = d // len(cfg.pool_windows)
    ds = [_pool_tokens(h, j, gdim, k, cfg.grid_w, on_grid) for j, k in enumerate(cfg.pool_windows)]
    return _pool_out(ds, x, w_pool_bf16, pool_scale, gate, tm)


def _router_kernel(nx_tiles, has_ctx, *refs):
    if has_ctx:
        x_ref, c_ref, g_ref, sh_ref, sc_ref, wr_ref, br_ref, fx_ref, lg_ref = refs
        src = jnp.where(pl.program_id(0) < nx_tiles, x_ref[0], c_ref[0])
    else:
        x_ref, g_ref, sh_ref, sc_ref, wr_ref, br_ref, fx_ref, lg_ref = refs
        src = x_ref[0]
    h = _rms_mod(src, g_ref[...], sh_ref[0], sc_ref[0])
    _store_slabs(fx_ref, h)
    lg_ref[...] = jnp.dot(h, wr_ref[...], precision=HIGHEST, preferred_element_type=F32) + br_ref[...]


def _router(x, ctx, g, sh_all, sc_all, w_r, b_r, tm):
    b, t, d = x.shape
    per_b = t // tm
    nx = b * per_b
    has_ctx = ctx is not None
    nc = b * (ctx.shape[1] // tm) if has_ctx else 0
    per_c = (ctx.shape[1] // tm) if has_ctx else 1
    n_all = (nx + nc) * tm
    lanes = w_r.shape[1]

    def mod_row(i):
        return jnp.where(i < nx, jnp.minimum(i, nx - 1) // per_b, b) if has_ctx else i // per_b

    in_specs = [pl.BlockSpec((1, tm, d), lambda i: (jnp.minimum(i, nx - 1) // per_b, jnp.minimum(i, nx - 1) % per_b, 0))]
    args = [x]
    if has_ctx:
        in_specs.append(pl.BlockSpec(
            (1, tm, d), lambda i: (jnp.maximum(i - nx, 0) // per_c, jnp.maximum(i - nx, 0) % per_c, 0)))
        args.append(ctx)
    in_specs += [pl.BlockSpec((1, d), lambda i: (0, 0)),
                 pl.BlockSpec((1, 1, d), lambda i: (mod_row(i), 0, 0)),
                 pl.BlockSpec((1, 1, d), lambda i: (mod_row(i), 0, 0)),
                 pl.BlockSpec((d, lanes), lambda i: (0, 0)),
                 pl.BlockSpec((1, lanes), lambda i: (0, 0))]
    args += [g.reshape(1, d), sh_all, sc_all, w_r, b_r]
    return pl.pallas_call(
        functools.partial(_router_kernel, nx, has_ctx),
        grid=(nx + nc,),
        in_specs=in_specs,
        out_specs=[pl.BlockSpec((tm, SUBLANES_V7X, d // SUBLANES_V7X), lambda i: (i, 0, 0)),
                   pl.BlockSpec((tm, lanes), lambda i: (i, 0))],
        out_shape=[jax.ShapeDtypeStruct((n_all, SUBLANES_V7X, d // SUBLANES_V7X), F32),
                   jax.ShapeDtypeStruct((n_all, lanes), F32)],
        compiler_params=_cparams(("arbitrary",), 8 * tm * d * 4 + 2 * d * lanes * 4),
        name="moe_router",
    )(*args)


def _expert_kernel(rows, n_all, col, be_ref, nu_ref, code_ref, fx_hbm, wgu_ref, wdn_ref, out_hbm,
                   xbuf, ybuf, gsem, ssem):
    i = pl.program_id(0)
    n_used = nu_ref[0]
    slot = i % 2
    other = 1 - slot
    n_slots = n_all * 2
    last_blk = code_ref.shape[0] // rows - 1
    d, f = wdn_ref.shape[2], wdn_ref.shape[1]

    def gather_row(blk, sl, r):
        v = code_ref[blk * rows + r]
        src = jnp.where(v >= n_slots, 0, v >> 1)
        return pltpu.make_async_copy(fx_hbm.at[src], xbuf.at[sl, r], gsem.at[sl])

    def scatter_row(blk, sl, r, none):
        v = jnp.where(none, n_slots, code_ref[blk * rows + r])
        dst = jnp.where(v >= n_slots, n_slots + sl * rows + r, (v & 1) * n_all + (v >> 1))
        return pltpu.make_async_copy(ybuf.at[sl, r], out_hbm.at[dst], ssem.at[sl])

    def wait_gather(sl):
        pltpu.make_async_copy(fx_hbm.at[pl.ds(0, rows)], xbuf.at[sl], gsem.at[sl]).wait()

    def wait_scatter(sl):
        pltpu.make_async_copy(ybuf.at[sl], out_hbm.at[pl.ds(0, rows)], ssem.at[sl]).wait()

    @pl.when(i == 0)
    def _():
        ybuf[...] = jnp.zeros(ybuf.shape, F32)
        for sl in range(2):
            pltpu.make_async_copy(ybuf.at[sl], out_hbm.at[pl.ds(n_slots + sl * rows, rows)], ssem.at[sl]).start()
        for sl in range(2):
            wait_scatter(sl)

        @pl.when(n_used > 0)
        def _():
            def body(r, c):
                gather_row(0, 0, r).start()
                return c
            lax.fori_loop(0, rows, body, 0)

    @pl.when(i < n_used)
    def _():
        wait_gather(slot)

        @pl.when(i >= 1)
        def _():
            wait_scatter(slot)

        nxt = jnp.minimum(i + 1, last_blk)
        prev = jnp.maximum(i - 1, 0)
        no_prev = i == 0
        xb = _load_slabs(xbuf.at[slot]).astype(BF16)
        n1 = f // col
        per1 = rows // n1
        acts = []
        for j in range(n1):
            g = jnp.dot(xb, wgu_ref[0, :, j * col:(j + 1) * col], preferred_element_type=F32)
            u = jnp.dot(xb, wgu_ref[0, :, f + j * col:f + (j + 1) * col], preferred_element_type=F32)
            acts.append((_silu(g) * u).astype(BF16))
            for r in range(j * per1, (j + 1) * per1):
                gather_row(nxt, other, r).start()
        act = jnp.concatenate(acts, axis=-1)
        n2, sub = ybuf.shape[2], ybuf.shape[3]
        per2 = rows // n2
        for j in range(n2):
            ybuf[slot, :, j, :] = jnp.dot(act, wdn_ref[0, :, j * sub:(j + 1) * sub], preferred_element_type=F32)
            for r in range(j * per2, (j + 1) * per2):
                scatter_row(prev, other, r, no_prev).start()

    @pl.when(jnp.logical_and(i == n_used, i >= 1))
    def _():
        wait_gather(slot)

        def body(r, c):
            scatter_row(i - 1, other, r, False).start()
            return c
        lax.fori_loop(0, rows, body, 0)
        wait_scatter(other)
        wait_scatter(slot)


def _experts(cfg, fx, block_e, n_used, row_code, w_gu_bf16, w_dn_bf16):
    n_all, n_sub, sub = fx.shape
    d = n_sub * sub
    e, _, f2 = w_gu_bf16.shape
    f = f2 // 2
    rows = cfg.moe_rows
    col = min(256, f)
    n_steps = block_e.shape[0]
    grid_spec = pltpu.PrefetchScalarGridSpec(
        num_scalar_prefetch=3,
        grid=(n_steps,),
        in_specs=[pl.BlockSpec(memory_space=pl.ANY),
                  pl.BlockSpec((1, d, f2), lambda i, be, nu, cd: (be[i], 0, 0)),
                  pl.BlockSpec((1, f, d), lambda i, be, nu, cd: (be[i], 0, 0))],
        out_specs=pl.BlockSpec(memory_space=pl.ANY),
        scratch_shapes=[pltpu.VMEM((2, rows, n_sub, sub), F32), pltpu.VMEM((2, rows, n_sub, sub), F32),
                        pltpu.SemaphoreType.DMA((2,)), pltpu.SemaphoreType.DMA((2,))])
    vmem = 2 * (d * f2 + f * d) * 2 + 4 * rows * d * 4 + 3 * rows * f2 * 4
    return pl.pallas_call(
        functools.partial(_expert_kernel, rows, n_all, col),
        grid_spec=grid_spec,
        out_shape=jax.ShapeDtypeStruct((cfg.top_k * n_all + 2 * rows, n_sub, sub), F32),
        compiler_params=_cparams(("arbitrary",), vmem),
        name="moe_experts",
    )(block_e, n_used, row_code, fx, w_gu_bf16, w_dn_bf16)


def _combine_kernel(final, *refs):
    if final:
        x_ref, a_ref, b_ref, w_ref, gt_ref, gf_ref, o_ref = refs
    else:
        x_ref, a_ref, b_ref, w_ref, gt_ref, o_ref = refs
    w = w_ref[...]
    y = x_ref[0] + gt_ref[0] * (w[:, 0:1] * _load_slabs(a_ref) + w[:, 1:2] * _load_slabs(b_ref))
    if final:
        y = y * lax.rsqrt(jnp.mean(y * y, axis=-1, keepdims=True) + EPS) * gf_ref[...]
    o_ref[0] = y


def _combine(x, slots, wts, tok_off, n_all, gate, tm, norm_final=None):
    b, t, d = x.shape
    per_b = t // tm
    off0 = tok_off // tm
    off1 = (n_all + tok_off) // tm
    final = norm_final is not None
    in_specs = [pl.BlockSpec((1, tm, d), lambda i, j: (i, j, 0)),
                pl.BlockSpec((tm,) + slots.shape[1:], lambda i, j: (off0 + i * per_b + j, 0, 0)),
                pl.BlockSpec((tm,) + slots.shape[1:], lambda i, j: (off1 + i * per_b + j, 0, 0)),
                pl.BlockSpec((tm, wts.shape[1]), lambda i, j: (off0 + i * per_b + j, 0)),
                pl.BlockSpec((1, 1, d), lambda i, j: (i, 0, 0))]
    args = [x, slots, slots, wts, gate]
    if final:
        in_specs.append(pl.BlockSpec((1, d), lambda i, j: (0, 0)))
        args.append(norm_final.reshape(1, d))
    return pl.pallas_call(
        functools.partial(_combine_kernel, final),
        grid=(b, per_b),
        in_specs=in_specs,
        out_specs=pl.BlockSpec((1, tm, d), lambda i, j: (i, j, 0)),
        out_shape=jax.ShapeDtypeStruct((b, t, d), F32),
        compiler_params=_cparams(("parallel", "parallel"), 10 * tm * d * 4),
        name="moe_combine",
    )(*args)


def _route(cfg, logits):
    n_all = logits.shape[0]
    ng, eg, k = cfg.n_groups, cfg.experts_per_group, cfg.top_k
    assert k == 2, "row codes pack (token, choice) as 2 * token + choice"
    n_exp = ng * eg
    rows = cfg.moe_rows
    p_grp = jax.nn.softmax(logits[:, :ng], axis=-1)
    g_idx = jnp.argmax(p_grp, axis=-1)
    p_g = jnp.max(p_grp, axis=-1)
    le = logits[:, ng:ng + n_exp].reshape(n_all, ng, eg)
    le = jnp.take_along_axis(le, g_idx[:, None, None], axis=1)[:, 0]
    vals, e_in = lax.top_k(jax.nn.softmax(le, axis=-1), k)
    wts = p_g[:, None] * vals / jnp.sum(vals, axis=-1, keepdims=True)
    eid = (g_idx[:, None] * eg + e_in).astype(jnp.int32)

    s = n_all * k
    e_flat = eid.reshape(-1)
    onehot = (e_flat[:, None] == jnp.arange(n_exp, dtype=jnp.int32)[None, :]).astype(jnp.int32)
    csum = jnp.cumsum(onehot, axis=0)
    rank = jnp.sum((csum - onehot) * onehot, axis=1)
    counts = csum[-1]
    padded = ((counts + rows - 1) // rows) * rows
    pends = jnp.cumsum(padded)
    pstarts = pends - padded
    dest = pstarts[e_flat] + rank
    n_blocks = -(-(s + n_exp * (rows - 1)) // rows)
    p = n_blocks * rows
    row_code = jnp.full((p,), s, jnp.int32).at[dest].set(jnp.arange(s, dtype=jnp.int32))
    blk_start = jnp.arange(n_blocks + 1, dtype=jnp.int32) * rows
    block_e = jnp.minimum(jnp.sum((pends[None, :] <= blk_start[:, None]).astype(jnp.int32), axis=1), n_exp - 1)
    n_used = (pends[-1] // rows).astype(jnp.int32).reshape(1)
    return (block_e, n_used, row_code), wts


def _moe(cfg, x, ctx, g, sh_all, sc_all, w_r, b_r, w_gu_bf16, w_dn_bf16):
    b, t, d = x.shape
    tm = min(cfg.tok_tile, t) if ctx is None else min(cfg.tok_tile, t, ctx.shape[1])
    fx, logits = _router(x, ctx, g, sh_all, sc_all, w_r, b_r, tm)
    n_all = fx.shape[0]
    plan, wts = _route(cfg, logits)
    slots = _experts(cfg, fx, *plan, w_gu_bf16, w_dn_bf16)
    return slots, wts, n_all


def _proj_kernel(kind, n_heads, hd, *refs):
    if kind == "gate":
        h_ref, w_ref, la_ref, lc_ref, o_ref = refs
    else:
        h_ref, w_ref, o_ref = refs
    z = jnp.dot(h_ref[0], w_ref[...], preferred_element_type=F32)
    if kind == "silu":
        z = _silu(z)
    elif kind == "gate":
        ls = jnp.minimum(z, 0.0) - jnp.log1p(jnp.exp(-jnp.abs(z)))
        c = lc_ref[...] + ls
        a = jnp.broadcast_to(la_ref[...], c.shape)
        m = jnp.maximum(a, c)
        z = m + jnp.log1p(jnp.exp(-jnp.abs(a - c)))
    for h in range(n_heads):
        o_ref[0, h] = z[:, h * hd:(h + 1) * hd].astype(o_ref.dtype)


def _proj(cfg, h_all, w_bf16, kind, out_dtype, log_lb=None, log1m_lb=None):
    b, s, d = h_all.shape
    n = w_bf16.shape[1]
    hd = cfg.head_dim
    nh = n // hd
    tm = cfg.tok_tile
    in_specs = [pl.BlockSpec((1, tm, d), lambda i, j: (i, j, 0)),
                pl.BlockSpec((d, n), lambda i, j: (0, 0))]
    args = [h_all, w_bf16]
    if kind == "gate":
        in_specs += [pl.BlockSpec((1, n), lambda i, j: (0, 0)), pl.BlockSpec((1, n), lambda i, j: (0, 0))]
        args += [log_lb.reshape(1, n), log1m_lb.reshape(1, n)]
    return pl.pallas_call(
        functools.partial(_proj_kernel, kind, nh, hd),
        grid=(b, s // tm),
        in_specs=in_specs,
        out_specs=pl.BlockSpec((1, nh, tm, hd), lambda i, j: (i, 0, j, 0)),
        out_shape=jax.ShapeDtypeStruct((b, nh, s, hd), out_dtype),
        compiler_params=_cparams(("parallel", "parallel"), 2 * d * n * 2 + 2 * tm * d * 2 + 6 * tm * n * 4),
        name=f"hgrn_proj_{kind}",
    )(*args)


def _gla_kernel(chunk, reverse, readout, limit, *refs):
    if readout:
        q_ref, lf_ref, v_ref, of_ref, gs_ref, ng_ref, o_ref, st_ref, b_ref, q32_ref, at_ref = refs
    else:
        q_ref, lf_ref, v_ref, o_ref, st_ref, b_ref, q32_ref, at_ref = refs
    hg, tb, hd = q_ref.shape[1], q_ref.shape[2], q_ref.shape[3]
    n_chunks = tb // chunk
    shift = chunk.bit_length() - 1

    @pl.when(pl.program_id(2) == 0)
    def _():
        st_ref[...] = jnp.zeros(st_ref.shape, F32)

    rr = lax.broadcasted_iota(jnp.int32, (tb, tb), 0)
    cc = lax.broadcasted_iota(jnp.int32, (tb, tb), 1)
    same = (rr >> shift) == (cc >> shift)
    tri = jnp.where(jnp.logical_and(same, (cc >= rr) if reverse else (cc <= rr)), 1.0, 0.0).astype(BF16)
    ar = lax.broadcasted_iota(jnp.int32, (chunk, chunk), 0)
    ac = lax.broadcasted_iota(jnp.int32, (chunk, chunk), 1)
    at_mask = (ar >= ac) if reverse else (ar <= ac)
    row_id = lax.broadcasted_iota(jnp.int32, (chunk, hd), 0)
    order = list(range(n_chunks))[::-1] if reverse else list(range(n_chunks))
    edge = 0 if reverse else chunk - 1
    grp = b_ref.shape[0]

    def group(hp, carry):
        heads = [hp * grp + u for u in range(grp)]
        lf_all = jnp.concatenate([lf_ref[0, h] for h in heads], axis=-1)
        hi = lf_all.astype(BF16)
        r1 = lf_all - hi.astype(F32)
        mid = r1.astype(BF16)
        lo = (r1 - mid.astype(F32)).astype(BF16)
        b_all = (jnp.dot(tri, hi, preferred_element_type=F32) + jnp.dot(tri, mid, preferred_element_type=F32)
                 + jnp.dot(tri, lo, preferred_element_type=F32))
        bs, kks, qfs, btots = [], [], [], []
        worst = None
        for u, h in enumerate(heads):
            b = b_all[:, u * hd:(u + 1) * hd]
            b_ref[u] = b
            btot = [b[c * chunk + edge:c * chunk + edge + 1, :] for c in range(n_chunks)]
            for bt in btot:
                worst = bt if worst is None else jnp.minimum(worst, bt)
            bs.append(b)
            btots.append(btot)
            kks.append(1.0 - jnp.exp(lf_all[:, u * hd:(u + 1) * hd]))
            qfs.append(q_ref[0, h].astype(F32))
        fast = jnp.min(worst) >= -limit

        @pl.when(fast)
        def _():
            for u in range(grp):
                for c in range(n_chunks):
                    sl = slice(c * chunk, (c + 1) * chunk)
                    d = bs[u][sl] - 0.5 * btots[u][c]
                    qh = (qfs[u][sl] * jnp.exp(d)).astype(BF16)
                    kh = (kks[u][sl] * jnp.exp(-d)).astype(BF16)
                    at = lax.dot_general(kh, qh, (((1,), (1,)), ((), ())), preferred_element_type=F32)
                    at_ref[u, c] = jnp.where(at_mask, at, 0.0)

        @pl.when(jnp.logical_not(fast))
        def _():
            for u in range(grp):
                q32_ref[u] = qfs[u]
                for c in range(n_chunks):
                    c0 = c * chunk
                    bc = bs[u][c0:c0 + chunk]
                    kc = kks[u][c0:c0 + chunk]

                    def tstep(t, at):
                        bt = b_ref[u, pl.ds(c0 + t, 1), :]
                        qrow = q32_ref[u, pl.ds(c0 + t, 1), :]
                        valid = (row_id >= t) if reverse else (row_id <= t)
                        e = jnp.exp(jnp.where(valid, bt - bc, -jnp.inf))
                        col = jnp.sum(e * kc * qrow, axis=-1, keepdims=True)
                        return at + col * jnp.where(ac == t, 1.0, 0.0)

                    at_ref[u, c] = lax.fori_loop(0, chunk, tstep, jnp.zeros((chunk, chunk), F32))

        for u, h in enumerate(heads):
            b, kk, btot = bs[u], kks[u], btots[u]
            qb = (qfs[u] * jnp.exp(b)).astype(BF16)
            vb = v_ref[0, h]
            st = st_ref[h]
            for c in order:
                sl = slice(c * chunk, (c + 1) * chunk)
                k2 = (kk[sl] * jnp.exp(btot[c] - b[sl])).astype(BF16)
                o = lax.dot_general(at_ref[u, c].astype(BF16), vb[sl], (((0,), (0,)), ((), ())),
                                    preferred_element_type=F32)
                o = o + lax.dot_general(qb[sl], st.astype(BF16), (((1,), (1,)), ((), ())),
                                        preferred_element_type=F32)
                st = st * jnp.exp(btot[c]) + lax.dot_general(vb[sl], k2, (((0,), (0,)), ((), ())),
                                                             preferred_element_type=F32)
                if readout:
                    o = o + of_ref[0, h, sl, :]
                    o = o * lax.rsqrt(jnp.mean(o * o, axis=-1, keepdims=True) + EPS) * ng_ref[...]
                    o_ref[0, h, sl, :] = (o * gs_ref[0, h, sl, :].astype(F32)).astype(o_ref.dtype)
                else:
                    o_ref[0, h, sl, :] = o
            st_ref[h] = st
        return carry

    lax.fori_loop(0, hg // grp, group, 0)


def _gla(cfg, qs, lf, v, reverse, n_ctx_blocks, of=None, gs=None, norm_g=None):
    b, nh, s, hd = qs.shape
    tb = cfg.tok_tile
    hg = nh
    grp = cfg.gla_group if hg % cfg.gla_group == 0 else 1
    nblk = s // tb
    readout = of is not None

    if reverse:
        def blk(j):
            return jnp.where(j < n_ctx_blocks, n_ctx_blocks - 1 - j, nblk - 1 - (j - n_ctx_blocks))
    else:
        def blk(j):
            return j

    spec = pl.BlockSpec((1, hg, tb, hd), lambda i, g, j: (i, g, blk(j), 0))
    in_specs = [spec, spec, spec]
    args = [qs, lf, v]
    if readout:
        in_specs += [spec, spec, pl.BlockSpec((1, hd), lambda i, g, j: (0, 0))]
        args += [of, gs, norm_g.reshape(1, hd)]
    out_dtype = BF16 if readout else F32
    vmem = 2 * hg * tb * hd * (2 + 4 + 2 + 4 + 2 + 4) + hg * hd * hd * 4 + 8 * tb * tb * 4 + 16 * tb * hd * 4
    return pl.pallas_call(
        functools.partial(_gla_kernel, cfg.gla_chunk, reverse, readout, cfg.fast_decay_limit),
        grid=(b, nh // hg, nblk),
        in_specs=in_specs,
        out_specs=spec,
        out_shape=jax.ShapeDtypeStruct((b, nh, s, hd), out_dtype),
        scratch_shapes=[pltpu.VMEM((hg, hd, hd), F32), pltpu.VMEM((grp, tb, hd), F32),
                        pltpu.VMEM((grp, tb, hd), F32),
                        pltpu.VMEM((grp, tb // cfg.gla_chunk, cfg.gla_chunk, cfg.gla_chunk), F32)],
        compiler_params=_cparams(("parallel", "parallel", "arbitrary"), vmem),
        name="hgrn_scan_bwd" if reverse else "hgrn_scan_fwd",
    )(*args)


def _out_proj_kernel(n_heads, y_ref, x_ref, w_ref, gt_ref, o_ref):
    y = jnp.concatenate([y_ref[0, h] for h in range(n_heads)], axis=-1)
    o_ref[0] = x_ref[0] + gt_ref[0] * jnp.dot(y, w_ref[...], preferred_element_type=F32)


def _out_proj(cfg, y_heads, x, w_out_bf16, gate, ctx_tiles):
    b, t, d = x.shape
    nh, hd = y_heads.shape[1], y_heads.shape[3]
    tm = cfg.tok_tile
    return pl.pallas_call(
        functools.partial(_out_proj_kernel, nh),
        grid=(b, t // tm),
        in_specs=[pl.BlockSpec((1, nh, tm, hd), lambda i, j: (i, 0, j + ctx_tiles, 0)),
                  pl.BlockSpec((1, tm, d), lambda i, j: (i, j, 0)),
                  pl.BlockSpec((d, d), lambda i, j: (0, 0)),
                  pl.BlockSpec((1, 1, d), lambda i, j: (i, 0, 0))],
        out_specs=pl.BlockSpec((1, tm, d), lambda i, j: (i, j, 0)),
        out_shape=jax.ShapeDtypeStruct((b, t, d), F32),
        compiler_params=_cparams(("parallel", "parallel"), 2 * d * d * 2 + 8 * tm * d * 4),
        name="hgrn_out_proj",
    )(y_heads, x, w_out_bf16, gate)


def _hgrn_mixer(cfg, x, ctx, g, sh_all, sc_all, gate, w_in, norm_g, w_out, lb):
    b, t, d = x.shape
    tm = cfg.tok_tile
    h_all = _norm_mod_seq(ctx, x, g, sh_all, sc_all, BF16, tm)
    w = w_in.astype(BF16)
    log_lb = jnp.log(lb)
    log1m_lb = jnp.log1p(-lb)
    qs = _proj(cfg, h_all, w[:, 0 * d:1 * d], "silu", BF16)
    gs = _proj(cfg, h_all, w[:, 1 * d:2 * d], "silu", BF16)
    lf = _proj(cfg, h_all, w[:, 2 * d:3 * d], "gate", F32, log_lb[0], log1m_lb[0])
    lr = _proj(cfg, h_all, w[:, 3 * d:4 * d], "gate", F32, log_lb[1], log1m_lb[1])
    v = _proj(cfg, h_all, w[:, 4 * d:5 * d], "none", BF16)
    n_ctx_blocks = ctx.shape[1] // tm
    of = _gla(cfg, qs, lf, v, False, n_ctx_blocks)
    y = _gla(cfg, qs, lr, v, True, n_ctx_blocks, of, gs, norm_g)
    return _out_proj(cfg, y, x, w_out.astype(BF16), gate, n_ctx_blocks)


def _forward(cfg, x, c, ctx, c_ctx, norm_mix, norm_ffn, w_ada, b_ada, pool_w, pool_scale, hgrn_w_in, hgrn_norm,
             hgrn_w_out, hgrn_lb_logits, router_w_group, router_b_group, router_w_expert, router_b_expert,
             moe_w_gate_up, moe_w_down, norm_final):
    b, t, d = x.shape
    depth = w_ada.shape[0]
    n_mixers = 2
    p_lb = jax.nn.softmax(hgrn_lb_logits.astype(F32), axis=0)
    lb_all = jnp.cumsum(p_lb, axis=0) - p_lb[0]

    rows = -(-(b + 1) // SUBLANES_V7X) * SUBLANES_V7X
    cond = jnp.zeros((rows, d), F32).at[:b].set(c).at[b].set(c_ctx)
    mods = _adaln(cond, w_ada, b_ada)

    ng, ne = cfg.n_groups, cfg.n_groups * cfg.experts_per_group
    for i in range(depth):
        ctx_needed = i < depth - 1
        j = i // n_mixers
        m = [mods[i, :, q * d:(q + 1) * d].reshape(rows, 1, d) for q in range(6)]
        sh_m, sc_m, gt_m, sh_f, sc_f, gt_f = m
        ctx_rows = lambda a: jnp.broadcast_to(a[b:b + 1], (b, 1, d))
        if i % n_mixers == 0:
            wp = pool_w[j].astype(BF16)
            x_new = _pool_mixer(cfg, x, norm_mix[i], sh_m, sc_m, gt_m, wp, pool_scale[j], True)
            if ctx_needed:
                ctx = _pool_mixer(cfg, ctx, norm_mix[i], ctx_rows(sh_m), ctx_rows(sc_m), ctx_rows(gt_m), wp,
                                  pool_scale[j], False)
            x = x_new
        else:
            x = _hgrn_mixer(cfg, x, ctx, norm_mix[i], sh_m, sc_m, gt_m, hgrn_w_in[j], hgrn_norm[j],
                            hgrn_w_out[j], lb_all[i])
        lanes = -(-(ng + ne) // LANES_V7X) * LANES_V7X
        w_r = jnp.zeros((d, lanes), F32).at[:, :ng].set(router_w_group[i]).at[:, ng:ng + ne].set(router_w_expert[i])
        b_r = jnp.zeros((1, lanes), F32).at[0, :ng].set(router_b_group[i]).at[0, ng:ng + ne].set(router_b_expert[i])
        slots, wts, n_all = _moe(cfg, x, ctx if ctx_needed else None, norm_ffn[i], sh_f, sc_f, w_r, b_r,
                                 moe_w_gate_up[i].astype(BF16), moe_w_down[i].astype(BF16))
        tm = min(cfg.tok_tile, t)
        last = i == depth - 1
        x_next = _combine(x, slots, wts, 0, n_all, gt_f, tm, norm_final if last else None)
        if ctx_needed:
            ctx = _combine(ctx, slots, wts, b * t, n_all, ctx_rows(gt_f), min(cfg.tok_tile, ctx.shape[1]))
        x = x_next
    return x


def kernel(x, c, ctx, c_ctx, norm_mix, norm_ffn, w_ada, b_ada, pool_w, pool_scale, hgrn_w_in, hgrn_norm, hgrn_w_out, hgrn_lb_logits, router_w_group, router_b_group, router_w_expert, router_b_expert, moe_w_gate_up, moe_w_down, norm_final):
    return _forward(Cfg(), x, c, ctx, c_ctx, norm_mix, norm_ffn, w_ada, b_ada, pool_w, pool_scale, hgrn_w_in,
                    hgrn_norm, hgrn_w_out, hgrn_lb_logits, router_w_group, router_b_group, router_w_expert,
                    router_b_expert, moe_w_gate_up, moe_w_down, norm_final)
```

```python
import functools
from typing import NamedTuple

import numpy as np
import jax
import jax.numpy as jnp
from jax import lax
from jax.experimental import pallas as pl
from jax.experimental.pallas import tpu as pltpu

F32 = jnp.float32
BF16 = jnp.bfloat16
HIGHEST = lax.Precision.HIGHEST
EPS = 1e-6

LANES_V7X = 128
SUBLANES_V7X = 8
VMEM_BYTES_V7X = 64 * 1024 * 1024


class Cfg(NamedTuple):
    grid_w: int = 64
    pool_windows: tuple = (2, 4, 8, 16)
    head_dim: int = LANES_V7X
    n_groups: int = 4
    experts_per_group: int = 8
    top_k: int = 2
    moe_rows: int = 256
    gla_chunk: int = 64
    gla_group: int = 2
    tok_tile: int = 256
    fast_decay_limit: float = 120.0


def _vmem_limit(nbytes):
    return int(min(max(nbytes * 5 // 4, 16 * 1024 * 1024), VMEM_BYTES_V7X - 6 * 1024 * 1024))


def _cparams(sem, vmem_bytes):
    return pltpu.CompilerParams(dimension_semantics=sem, vmem_limit_bytes=_vmem_limit(vmem_bytes))


def _rms_mod(x, g, shift, scale):
    y = x * lax.rsqrt(jnp.mean(x * x, axis=-1, keepdims=True) + EPS) * g
    return y * (1.0 + scale) + shift


def _silu(x):
    return x * jax.nn.sigmoid(x)


def _store_slabs(ref, val):
    sub = ref.shape[-1]
    for s in range(ref.shape[-2]):
        ref[:, s, :] = val[:, s * sub:(s + 1) * sub]


def _load_slabs(ref):
    return jnp.concatenate([ref[:, s, :] for s in range(ref.shape[-2])], axis=-1)


def _adaln_kernel(cond_ref, w_ref, b_ref, o_ref):
    s = _silu(cond_ref[...])
    o_ref[0] = jnp.dot(s, w_ref[0], precision=HIGHEST, preferred_element_type=F32) + b_ref[0]


def _adaln(cond, w_ada, b_ada):
    depth, d, n = w_ada.shape
    rows = cond.shape[0]
    tn = 1024 if n % 1024 == 0 else n
    return pl.pallas_call(
        _adaln_kernel,
        grid=(depth, n // tn),
        in_specs=[pl.BlockSpec((rows, d), lambda l, j: (0, 0)),
                  pl.BlockSpec((1, d, tn), lambda l, j: (l, 0, j)),
                  pl.BlockSpec((1, 1, tn), lambda l, j: (l, 0, j))],
        out_specs=pl.BlockSpec((1, rows, tn), lambda l, j: (l, 0, j)),
        out_shape=jax.ShapeDtypeStruct((depth, rows, n), F32),
        compiler_params=_cparams(("arbitrary", "arbitrary"), 2 * d * tn * 4 + 4 * rows * (d + tn) * 4),
        name="adaln",
    )(cond, w_ada, b_ada.reshape(depth, 1, n))


def _norm_mod_kernel(x_ref, g_ref, sh_ref, sc_ref, o_ref):
    o_ref[0] = _rms_mod(x_ref[0], g_ref[...], sh_ref[0], sc_ref[0]).astype(o_ref.dtype)


def _norm_mod(x, g, sh, sc, out_dtype, tm):
    b, t, d = x.shape
    return pl.pallas_call(
        _norm_mod_kernel,
        grid=(b, t // tm),
        in_specs=[pl.BlockSpec((1, tm, d), lambda i, j: (i, j, 0)),
                  pl.BlockSpec((1, d), lambda i, j: (0, 0)),
                  pl.BlockSpec((1, 1, d), lambda i, j: (i, 0, 0)),
                  pl.BlockSpec((1, 1, d), lambda i, j: (i, 0, 0))],
        out_specs=pl.BlockSpec((1, tm, d), lambda i, j: (i, j, 0)),
        out_shape=jax.ShapeDtypeStruct((b, t, d), out_dtype),
        compiler_params=_cparams(("parallel", "parallel"), 6 * tm * d * 4),
        name="norm_mod",
    )(x, g.reshape(1, d), sh, sc)


def _norm_mod_seq_kernel(n_ctx_tiles, c_ref, x_ref, g_ref, sh_ref, sc_ref, o_ref):
    j = pl.program_id(1)
    src = jnp.where(j < n_ctx_tiles, c_ref[0], x_ref[0])
    o_ref[0] = _rms_mod(src, g_ref[...], sh_ref[0], sc_ref[0]).astype(o_ref.dtype)


def _norm_mod_seq(ctx, x, g, sh_all, sc_all, out_dtype, tm):
    b, t, d = x.shape
    l = ctx.shape[1]
    nc, nx = l // tm, t // tm
    return pl.pallas_call(
        functools.partial(_norm_mod_seq_kernel, nc),
        grid=(b, nc + nx),
        in_specs=[pl.BlockSpec((1, tm, d), lambda i, j: (i, jnp.minimum(j, nc - 1), 0)),
                  pl.BlockSpec((1, tm, d), lambda i, j: (i, jnp.maximum(j - nc, 0), 0)),
                  pl.BlockSpec((1, d), lambda i, j: (0, 0)),
                  pl.BlockSpec((1, 1, d), lambda i, j: (jnp.where(j < nc, b, i), 0, 0)),
                  pl.BlockSpec((1, 1, d), lambda i, j: (jnp.where(j < nc, b, i), 0, 0))],
        out_specs=pl.BlockSpec((1, tm, d), lambda i, j: (i, j, 0)),
        out_shape=jax.ShapeDtypeStruct((b, l + t, d), out_dtype),
        compiler_params=_cparams(("parallel", "arbitrary"), 8 * tm * d * 4),
        name="norm_mod_seq",
    )(ctx, x, g.reshape(1, d), sh_all, sc_all)


def _win_matrix(length, k):
    pos = np.arange(length)
    lo = np.clip(pos - k // 2, 0, length - 1)
    hi = np.clip(pos + (k - k // 2 - 1), 0, length - 1)
    m = np.zeros((length, length), np.float32)
    for p in range(length):
        m[p, lo[p]:hi[p] + 1] = 1.0 / float(hi[p] - lo[p] + 1)
    return m, (1.0 / (hi - lo + 1)).astype(np.float32)


def _pool_tokens_kernel(k, w, slab, vertical, inv_ref, ph_ref, h_ref, o_ref, pad_ref):
    t, tc = h_ref.shape[1], h_ref.shape[2]
    if not vertical:
        for s in range(t // slab):
            hs = h_ref[0, s * slab:(s + 1) * slab, :]
            o_ref[0, s * slab:(s + 1) * slab, :] = (
                jnp.dot(ph_ref[...], hs, precision=HIGHEST, preferred_element_type=F32) - hs)
        return
    top = (k // 2) * w
    bot = (k - k // 2 - 1) * w
    pad_ref[0:top, :] = jnp.zeros((top, tc), F32)
    if bot:
        pad_ref[top + t:top + t + bot, :] = jnp.zeros((bot, tc), F32)
    for s in range(t // slab):
        hs = h_ref[0, s * slab:(s + 1) * slab, :]
        pad_ref[top + s * slab:top + (s + 1) * slab, :] = jnp.dot(
            ph_ref[...], hs, precision=HIGHEST, preferred_element_type=F32)

    def row(r, carry):
        base = pl.multiple_of(r * w, w)
        acc = pad_ref[pl.ds(base, w), :]
        for dr in range(1, k):
            acc = acc + pad_ref[pl.ds(pl.multiple_of(base + dr * w, w), w), :]
        o_ref[0, pl.ds(base, w), :] = acc * inv_ref[r] - h_ref[0, pl.ds(base, w), :]
        return carry

    lax.fori_loop(0, t // w, row, 0)


def _pool_tokens(h, group, gdim, k, grid_w, vertical):
    b, t, d = h.shape
    tc = min(gdim, 256)
    per = gdim // tc
    if vertical:
        slab = max(grid_w, min(256, t))
        mh, _ = _win_matrix(grid_w, k)
        ph = np.kron(np.eye(slab // grid_w, dtype=np.float32), mh)
        _, inv_v = _win_matrix(t // grid_w, k)
    else:
        slab = t
        ph, _ = _win_matrix(t, k)
        inv_v = np.ones((1,), np.float32)
    pad_rows = t + (k - 1) * grid_w if vertical else SUBLANES_V7X
    return pl.pallas_call(
        functools.partial(_pool_tokens_kernel, k, grid_w, slab, vertical),
        grid=(b, per),
        in_specs=[pl.BlockSpec(memory_space=pltpu.SMEM),
                  pl.BlockSpec((slab, slab), lambda i, j: (0, 0)),
                  pl.BlockSpec((1, t, tc), lambda i, j: (i, 0, group * per + j))],
        out_specs=pl.BlockSpec((1, t, tc), lambda i, j: (i, 0, j)),
        out_shape=jax.ShapeDtypeStruct((b, t, gdim), F32),
        scratch_shapes=[pltpu.VMEM((pad_rows, tc), F32)],
        compiler_params=_cparams(("parallel", "parallel"), (4 * t + pad_rows) * tc * 4 + 2 * slab * slab * 4),
        name=f"pool_tokens_k{k}",
    )(jnp.asarray(inv_v), jnp.asarray(ph), h)


def _pool_out_kernel(n_groups, *refs):
    d_refs = refs[:n_groups]
    x_ref, w_ref, ps_ref, gt_ref, o_ref = refs[n_groups:]
    ys = [jnp.dot(d_refs[j][0].astype(BF16), w_ref[j], preferred_element_type=F32) for j in range(n_groups)]
    y = jnp.concatenate(ys, axis=-1) * ps_ref[...]
    o_ref[0] = x_ref[0] + gt_ref[0] * y


def _pool_out(ds, x, w_pool_bf16, pool_scale, gate, tm):
    b, t, d = x.shape
    ng, gdim, _ = w_pool_bf16.shape
    return pl.pallas_call(
        functools.partial(_pool_out_kernel, ng),
        grid=(b, t // tm),
        in_specs=[pl.BlockSpec((1, tm, gdim), lambda i, j: (i, j, 0)) for _ in range(ng)] + [
            pl.BlockSpec((1, tm, d), lambda i, j: (i, j, 0)),
            pl.BlockSpec((ng, gdim, gdim), lambda i, j: (0, 0, 0)),
            pl.BlockSpec((1, d), lambda i, j: (0, 0)),
            pl.BlockSpec((1, 1, d), lambda i, j: (i, 0, 0))],
        out_specs=pl.BlockSpec((1, tm, d), lambda i, j: (i, j, 0)),
        out_shape=jax.ShapeDtypeStruct((b, t, d), F32),
        compiler_params=_cparams(("parallel", "parallel"), 8 * tm * d * 4 + 2 * ng * gdim * gdim * 2),
        name="pool_out",
    )(*ds, x, w_pool_bf16, pool_scale.reshape(1, d), gate)


def _pool_mixer(cfg, x, g, sh, sc, gate, w_pool_bf16, pool_scale, on_grid):
    b, t, d = x.shape
    tm = min(cfg.tok_tile, t)
    h = _norm_mod(x, g, sh, sc, F32, tm)
    gdim = d // len(cfg.pool_windows)
    ds = [_pool_tokens(h, j, gdim, k, cfg.grid_w, on_grid) for j, k in enumerate(cfg.pool_windows)]
    return _pool_out(ds, x, w_pool_bf16, pool_scale, gate, tm)


def _router_kernel(nx_tiles, has_ctx, *refs):
    if has_ctx:
        x_ref, c_ref, g_ref, sh_ref, sc_ref, wr_ref, br_ref, fx_ref, lg_ref = refs
        src = jnp.where(pl.program_id(0) < nx_tiles, x_ref[0], c_ref[0])
    else:
        x_ref, g_ref, sh_ref, sc_ref, wr_ref, br_ref, fx_ref, lg_ref = refs
        src = x_ref[0]
    h = _rms_mod(src, g_ref[...], sh_ref[0], sc_ref[0])
    _store_slabs(fx_ref, h)
    lg_ref[...] = jnp.dot(h, wr_ref[...], precision=HIGHEST, preferred_element_type=F32) + br_ref[...]


def _router(x, ctx, g, sh_all, sc_all, w_r, b_r, tm):
    b, t, d = x.shape
    per_b = t // tm
    nx = b * per_b
    has_ctx = ctx is not None
    nc = b * (ctx.shape[1] // tm) if has_ctx else 0
    per_c = (ctx.shape[1] // tm) if has_ctx else 1
    n_all = (nx + nc) * tm
    lanes = w_r.shape[1]

    def mod_row(i):
        return jnp.where(i < nx, jnp.minimum(i, nx - 1) // per_b, b) if has_ctx else i // per_b

    in_specs = [pl.BlockSpec((1, tm, d), lambda i: (jnp.minimum(i, nx - 1) // per_b, jnp.minimum(i, nx - 1) % per_b, 0))]
    args = [x]
    if has_ctx:
        in_specs.append(pl.BlockSpec(
            (1, tm, d), lambda i: (jnp.maximum(i - nx, 0) // per_c, jnp.maximum(i - nx, 0) % per_c, 0)))
        args.append(ctx)
    in_specs += [pl.BlockSpec((1, d), lambda i: (0, 0)),
                 pl.BlockSpec((1, 1, d), lambda i: (mod_row(i), 0, 0)),
                 pl.BlockSpec((1, 1, d), lambda i: (mod_row(i), 0, 0)),
                 pl.BlockSpec((d, lanes), lambda i: (0, 0)),
                 pl.BlockSpec((1, lanes), lambda i: (0, 0))]
    args += [g.reshape(1, d), sh_all, sc_all, w_r, b_r]
    return pl.pallas_call(
        functools.partial(_router_kernel, nx, has_ctx),
        grid=(nx + nc,),
        in_specs=in_specs,
        out_specs=[pl.BlockSpec((tm, SUBLANES_V7X, d // SUBLANES_V7X), lambda i: (i, 0, 0)),
                   pl.BlockSpec((tm, lanes), lambda i: (i, 0))],
        out_shape=[jax.ShapeDtypeStruct((n_all, SUBLANES_V7X, d // SUBLANES_V7X), F32),
                   jax.ShapeDtypeStruct((n_all, lanes), F32)],
        compiler_params=_cparams(("arbitrary",), 8 * tm * d * 4 + 2 * d * lanes * 4),
        name="moe_router",
    )(*args)


def _expert_kernel(rows, n_all, col, be_ref, nu_ref, code_ref, fx_hbm, wgu_ref, wdn_ref, out_hbm,
                   xbuf, ybuf, gsem, ssem):
    i = pl.program_id(0)
    n_used = nu_ref[0]
    slot = i % 2
    other = 1 - slot
    n_slots = n_all * 2
    last_blk = code_ref.shape[0] // rows - 1
    f = wdn_ref.shape[2]

    def gather_row(blk, sl, r):
        v = code_ref[blk * rows + r]
        src = jnp.where(v >= n_slots, 0, v >> 1)
        return pltpu.make_async_copy(fx_hbm.at[src], xbuf.at[sl, r], gsem.at[sl])

    def scatter_row(blk, sl, r, none):
        v = jnp.where(none, n_slots, code_ref[blk * rows + r])
        dst = jnp.where(v >= n_slots, n_slots + sl * rows + r, (v & 1) * n_all + (v >> 1))
        return pltpu.make_async_copy(ybuf.at[sl, r], out_hbm.at[dst], ssem.at[sl])

    def wait_gather(sl):
        pltpu.make_async_copy(fx_hbm.at[pl.ds(0, rows)], xbuf.at[sl], gsem.at[sl]).wait()

    def wait_scatter(sl):
        pltpu.make_async_copy(ybuf.at[sl], out_hbm.at[pl.ds(0, rows)], ssem.at[sl]).wait()

    @pl.when(i == 0)
    def _():
        ybuf[...] = jnp.zeros(ybuf.shape, F32)
        for sl in range(2):
            pltpu.make_async_copy(ybuf.at[sl], out_hbm.at[pl.ds(n_slots + sl * rows, rows)], ssem.at[sl]).start()
        for sl in range(2):
            wait_scatter(sl)

        @pl.when(n_used > 0)
        def _():
            def body(r, c):
                gather_row(0, 0, r).start()
                return c
            lax.fori_loop(0, rows, body, 0)

    @pl.when(i < n_used)
    def _():
        wait_gather(slot)

        @pl.when(i >= 1)
        def _():
            wait_scatter(slot)

        nxt = jnp.minimum(i + 1, last_blk)
        prev = jnp.maximum(i - 1, 0)
        no_prev = i == 0
        xb = _load_slabs(xbuf.at[slot]).astype(BF16)
        n1 = f // col
        per1 = rows // n1
        acts = []
        for j in range(n1):
            g = jnp.dot(xb, wgu_ref[0, 0, :, j * col:(j + 1) * col].astype(BF16), preferred_element_type=F32)
            u = jnp.dot(xb, wgu_ref[0, 0, :, f + j * col:f + (j + 1) * col].astype(BF16),
                        preferred_element_type=F32)
            acts.append((_silu(g) * u).astype(BF16))
            for r in range(j * per1, (j + 1) * per1):
                gather_row(nxt, other, r).start(priority=r % 2)
        act = jnp.concatenate(acts, axis=-1)
        n2, sub = ybuf.shape[2], ybuf.shape[3]
        per2 = rows // n2
        for j in range(n2):
            ybuf[slot, :, j, :] = jnp.dot(act, wdn_ref[0, 0, :, j * sub:(j + 1) * sub].astype(BF16),
                                          preferred_element_type=F32)
            for r in range(j * per2, (j + 1) * per2):
                scatter_row(prev, other, r, no_prev).start(priority=(r + 1) % 2)

    @pl.when(jnp.logical_and(i == n_used, i >= 1))
    def _():
        wait_gather(slot)

        def body(r, c):
            scatter_row(i - 1, other, r, False).start()
            return c
        lax.fori_loop(0, rows, body, 0)
        wait_scatter(other)
        wait_scatter(slot)


def _experts(cfg, fx, block_e, n_used, row_code, w_gu, w_dn, layer):
    n_all, n_sub, sub = fx.shape
    d = n_sub * sub
    f2 = w_gu.shape[3]
    f = f2 // 2
    rows = cfg.moe_rows
    col = min(256, f)
    n_steps = block_e.shape[0]
    grid_spec = pltpu.PrefetchScalarGridSpec(
        num_scalar_prefetch=3,
        grid=(n_steps,),
        in_specs=[pl.BlockSpec(memory_space=pl.ANY),
                  pl.BlockSpec((1, 1, d, f2), lambda i, be, nu, cd: (layer, be[i], 0, 0),
                               pipeline_mode=pl.Buffered(1)),
                  pl.BlockSpec((1, 1, f, d), lambda i, be, nu, cd: (layer, be[i], 0, 0))],
        out_specs=pl.BlockSpec(memory_space=pl.ANY),
        scratch_shapes=[pltpu.VMEM((2, rows, n_sub, sub), F32), pltpu.VMEM((2, rows, n_sub, sub), F32),
                        pltpu.SemaphoreType.DMA((2,)), pltpu.SemaphoreType.DMA((2,))])
    vmem = (d * f2 + 2 * f * d) * 4 + 4 * rows * d * 4 + 3 * rows * f2 * 4 + 4 * d * col * 4
    return pl.pallas_call(
        functools.partial(_expert_kernel, rows, n_all, col),
        grid_spec=grid_spec,
        out_shape=jax.ShapeDtypeStruct((cfg.top_k * n_all + 2 * rows, n_sub, sub), F32),
        compiler_params=_cparams(("arbitrary",), vmem),
        name="moe_experts",
    )(block_e, n_used, row_code, fx, w_gu, w_dn)


def _combine_kernel(final, *refs):
    if final:
        x_ref, a_ref, b_ref, w_ref, gt_ref, gf_ref, o_ref = refs
    else:
        x_ref, a_ref, b_ref, w_ref, gt_ref, o_ref = refs
    w = w_ref[...]
    y = x_ref[0] + gt_ref[0] * (w[:, 0:1] * _load_slabs(a_ref) + w[:, 1:2] * _load_slabs(b_ref))
    if final:
        y = y * lax.rsqrt(jnp.mean(y * y, axis=-1, keepdims=True) + EPS) * gf_ref[...]
    o_ref[0] = y


def _combine(x, slots, wts, tok_off, n_all, gate, tm, norm_final=None):
    b, t, d = x.shape
    per_b = t // tm
    off0 = tok_off // tm
    off1 = (n_all + tok_off) // tm
    final = norm_final is not None
    in_specs = [pl.BlockSpec((1, tm, d), lambda i, j: (i, j, 0)),
                pl.BlockSpec((tm,) + slots.shape[1:], lambda i, j: (off0 + i * per_b + j, 0, 0)),
                pl.BlockSpec((tm,) + slots.shape[1:], lambda i, j: (off1 + i * per_b + j, 0, 0)),
                pl.BlockSpec((tm, wts.shape[1]), lambda i, j: (off0 + i * per_b + j, 0)),
                pl.BlockSpec((1, 1, d), lambda i, j: (i, 0, 0))]
    args = [x, slots, slots, wts, gate]
    if final:
        in_specs.append(pl.BlockSpec((1, d), lambda i, j: (0, 0)))
        args.append(norm_final.reshape(1, d))
    return pl.pallas_call(
        functools.partial(_combine_kernel, final),
        grid=(b, per_b),
        in_specs=in_specs,
        out_specs=pl.BlockSpec((1, tm, d), lambda i, j: (i, j, 0)),
        out_shape=jax.ShapeDtypeStruct((b, t, d), F32),
        compiler_params=_cparams(("parallel", "parallel"), 10 * tm * d * 4),
        name="moe_combine",
    )(*args)


def _route(cfg, logits):
    n_all = logits.shape[0]
    ng, eg, k = cfg.n_groups, cfg.experts_per_group, cfg.top_k
    assert k == 2, "row codes pack (token, choice) as 2 * token + choice"
    n_exp = ng * eg
    rows = cfg.moe_rows
    p_grp = jax.nn.softmax(logits[:, :ng], axis=-1)
    g_idx = jnp.argmax(p_grp, axis=-1)
    p_g = jnp.max(p_grp, axis=-1)
    le = logits[:, ng:ng + n_exp].reshape(n_all, ng, eg)
    le = jnp.take_along_axis(le, g_idx[:, None, None], axis=1)[:, 0]
    vals, e_in = lax.top_k(jax.nn.softmax(le, axis=-1), k)
    wts = p_g[:, None] * vals / jnp.sum(vals, axis=-1, keepdims=True)
    eid = (g_idx[:, None] * eg + e_in).astype(jnp.int32)

    s = n_all * k
    e_flat = eid.reshape(-1)
    onehot = (e_flat[:, None] == jnp.arange(n_exp, dtype=jnp.int32)[None, :]).astype(jnp.int32)
    csum = jnp.cumsum(onehot, axis=0)
    rank = jnp.sum((csum - onehot) * onehot, axis=1)
    counts = csum[-1]
    padded = ((counts + rows - 1) // rows) * rows
    pends = jnp.cumsum(padded)
    pstarts = pends - padded
    dest = pstarts[e_flat] + rank
    n_blocks = -(-(s + n_exp * (rows - 1)) // rows)
    p = n_blocks * rows
    row_code = jnp.full((p,), s, jnp.int32).at[dest].set(jnp.arange(s, dtype=jnp.int32))
    blk_start = jnp.arange(n_blocks + 1, dtype=jnp.int32) * rows
    block_e = jnp.minimum(jnp.sum((pends[None, :] <= blk_start[:, None]).astype(jnp.int32), axis=1), n_exp - 1)
    n_used = (pends[-1] // rows).astype(jnp.int32).reshape(1)
    return (block_e, n_used, row_code), wts


def _moe(cfg, x, ctx, g, sh_all, sc_all, w_r, b_r, w_gu, w_dn, layer):
    b, t, d = x.shape
    tm = min(cfg.tok_tile, t) if ctx is None else min(cfg.tok_tile, t, ctx.shape[1])
    fx, logits = _router(x, ctx, g, sh_all, sc_all, w_r, b_r, tm)
    n_all = fx.shape[0]
    plan, wts = _route(cfg, logits)
    slots = _experts(cfg, fx, *plan, w_gu, w_dn, layer)
    return slots, wts, n_all


def _cast_weight_once(w_ref, wbf_ref):
    @pl.when(jnp.logical_and(pl.program_id(0) == 0, pl.program_id(1) == 0))
    def _():
        k = wbf_ref.shape[0]
        step = min(k, 256)

        def body(c, carry):
            r0 = pl.multiple_of(c * step, step)
            wbf_ref[pl.ds(r0, step), :] = w_ref[0, pl.ds(r0, step), :].astype(BF16)
            return carry
        lax.fori_loop(0, k // step, body, 0)


def _proj_kernel(kind, n_heads, hd, *refs):
    if kind == "gate":
        h_ref, w_ref, la_ref, lc_ref, o_ref, wbf_ref = refs
    else:
        h_ref, w_ref, o_ref, wbf_ref = refs
    _cast_weight_once(w_ref, wbf_ref)
    z = jnp.dot(h_ref[0], wbf_ref[...], preferred_element_type=F32)
    if kind == "silu":
        z = _silu(z)
    elif kind == "gate":
        ls = jnp.minimum(z, 0.0) - jnp.log1p(jnp.exp(-jnp.abs(z)))
        c = lc_ref[...] + ls
        a = jnp.broadcast_to(la_ref[...], c.shape)
        m = jnp.maximum(a, c)
        z = m + jnp.log1p(jnp.exp(-jnp.abs(a - c)))
    for h in range(n_heads):
        o_ref[0, h] = z[:, h * hd:(h + 1) * hd].astype(o_ref.dtype)


def _proj(cfg, h_all, w_all, layer, section, kind, out_dtype, log_lb=None, log1m_lb=None):
    b, s, d = h_all.shape
    n = d
    hd = cfg.head_dim
    nh = n // hd
    tm = cfg.tok_tile
    in_specs = [pl.BlockSpec((1, tm, d), lambda i, j: (i, j, 0)),
                pl.BlockSpec((1, d, n), lambda i, j: (layer, 0, section), pipeline_mode=pl.Buffered(1))]
    args = [h_all, w_all]
    if kind == "gate":
        in_specs += [pl.BlockSpec((1, n), lambda i, j: (0, 0)), pl.BlockSpec((1, n), lambda i, j: (0, 0))]
        args += [log_lb.reshape(1, n), log1m_lb.reshape(1, n)]
    return pl.pallas_call(
        functools.partial(_proj_kernel, kind, nh, hd),
        grid=(b, s // tm),
        in_specs=in_specs,
        out_specs=pl.BlockSpec((1, nh, tm, hd), lambda i, j: (i, 0, j, 0)),
        out_shape=jax.ShapeDtypeStruct((b, nh, s, hd), out_dtype),
        scratch_shapes=[pltpu.VMEM((d, n), BF16)],
        compiler_params=_cparams(("arbitrary", "arbitrary"), d * n * 6 + 2 * tm * d * 2 + 6 * tm * n * 4),
        name=f"hgrn_proj_{kind}",
    )(*args)


def _gla_kernel(chunk, reverse, readout, limit, *refs):
    if readout:
        q_ref, lf_ref, v_ref, of_ref, gs_ref, ng_ref, o_ref, st_ref, b_ref, q32_ref, at_ref = refs
    else:
        q_ref, lf_ref, v_ref, o_ref, st_ref, b_ref, q32_ref, at_ref = refs
    hg, tb, hd = q_ref.shape[1], q_ref.shape[2], q_ref.shape[3]
    n_chunks = tb // chunk
    shift = chunk.bit_length() - 1

    @pl.when(pl.program_id(2) == 0)
    def _():
        st_ref[...] = jnp.zeros(st_ref.shape, F32)

    rr = lax.broadcasted_iota(jnp.int32, (tb, tb), 0)
    cc = lax.broadcasted_iota(jnp.int32, (tb, tb), 1)
    same = (rr >> shift) == (cc >> shift)
    tri = jnp.where(jnp.logical_and(same, (cc >= rr) if reverse else (cc <= rr)), 1.0, 0.0).astype(BF16)
    ar = lax.broadcasted_iota(jnp.int32, (chunk, chunk), 0)
    ac = lax.broadcasted_iota(jnp.int32, (chunk, chunk), 1)
    at_mask = (ar >= ac) if reverse else (ar <= ac)
    row_id = lax.broadcasted_iota(jnp.int32, (chunk, hd), 0)
    order = list(range(n_chunks))[::-1] if reverse else list(range(n_chunks))
    edge = 0 if reverse else chunk - 1
    grp = b_ref.shape[0]

    def group(hp, carry):
        heads = [hp * grp + u for u in range(grp)]
        lf_all = jnp.concatenate([lf_ref[0, h] for h in heads], axis=-1)
        hi = lf_all.astype(BF16)
        r1 = lf_all - hi.astype(F32)
        mid = r1.astype(BF16)
        lo = (r1 - mid.astype(F32)).astype(BF16)
        b_all = (jnp.dot(tri, hi, preferred_element_type=F32) + jnp.dot(tri, mid, preferred_element_type=F32)
                 + jnp.dot(tri, lo, preferred_element_type=F32))
        bs, kks, qfs, btots = [], [], [], []
        worst = None
        for u, h in enumerate(heads):
            b = b_all[:, u * hd:(u + 1) * hd]
            b_ref[u] = b
            btot = [b[c * chunk + edge:c * chunk + edge + 1, :] for c in range(n_chunks)]
            for bt in btot:
                worst = bt if worst is None else jnp.minimum(worst, bt)
            bs.append(b)
            btots.append(btot)
            kks.append(1.0 - jnp.exp(lf_all[:, u * hd:(u + 1) * hd]))
            qfs.append(q_ref[0, h].astype(F32))
        fast = jnp.min(worst) >= -limit

        @pl.when(fast)
        def _():
            for u in range(grp):
                for c in range(n_chunks):
                    sl = slice(c * chunk, (c + 1) * chunk)
                    d = bs[u][sl] - 0.5 * btots[u][c]
                    qh = (qfs[u][sl] * jnp.exp(d)).astype(BF16)
                    kh = (kks[u][sl] * jnp.exp(-d)).astype(BF16)
                    at = lax.dot_general(kh, qh, (((1,), (1,)), ((), ())), preferred_element_type=F32)
                    at_ref[u, c] = jnp.where(at_mask, at, 0.0)

        @pl.when(jnp.logical_not(fast))
        def _():
            for u in range(grp):
                q32_ref[u] = qfs[u]
                for c in range(n_chunks):
                    c0 = c * chunk
                    bc = bs[u][c0:c0 + chunk]
                    kc = kks[u][c0:c0 + chunk]

                    def tstep(t, at):
                        bt = b_ref[u, pl.ds(c0 + t, 1), :]
                        qrow = q32_ref[u, pl.ds(c0 + t, 1), :]
                        valid = (row_id >= t) if reverse else (row_id <= t)
                        e = jnp.exp(jnp.where(valid, bt - bc, -jnp.inf))
                        col = jnp.sum(e * kc * qrow, axis=-1, keepdims=True)
                        return at + col * jnp.where(ac == t, 1.0, 0.0)

                    at_ref[u, c] = lax.fori_loop(0, chunk, tstep, jnp.zeros((chunk, chunk), F32))

        for u, h in enumerate(heads):
            b, kk, btot = bs[u], kks[u], btots[u]
            qb = (qfs[u] * jnp.exp(b)).astype(BF16)
            vb = v_ref[0, h]
            st = st_ref[h]
            for c in order:
                sl = slice(c * chunk, (c + 1) * chunk)
                k2 = (kk[sl] * jnp.exp(btot[c] - b[sl])).astype(BF16)
                o = lax.dot_general(at_ref[u, c].astype(BF16), vb[sl], (((0,), (0,)), ((), ())),
                                    preferred_element_type=F32)
                o = o + lax.dot_general(qb[sl], st.astype(BF16), (((1,), (1,)), ((), ())),
                                        preferred_element_type=F32)
                st = st * jnp.exp(btot[c]) + lax.dot_general(vb[sl], k2, (((0,), (0,)), ((), ())),
                                                             preferred_element_type=F32)
                if readout:
                    o = o + of_ref[0, h, sl, :]
                    o = o * lax.rsqrt(jnp.mean(o * o, axis=-1, keepdims=True) + EPS) * ng_ref[...]
                    o_ref[0, h, sl, :] = (o * gs_ref[0, h, sl, :].astype(F32)).astype(o_ref.dtype)
                else:
                    o_ref[0, h, sl, :] = o
            st_ref[h] = st
        return carry

    lax.fori_loop(0, hg // grp, group, 0)


def _gla(cfg, qs, lf, v, reverse, n_ctx_blocks, of=None, gs=None, norm_g=None):
    b, nh, s, hd = qs.shape
    tb = cfg.tok_tile
    hg = nh
    grp = cfg.gla_group if hg % cfg.gla_group == 0 else 1
    nblk = s // tb
    readout = of is not None

    if reverse:
        def blk(j):
            return jnp.where(j < n_ctx_blocks, n_ctx_blocks - 1 - j, nblk - 1 - (j - n_ctx_blocks))
    else:
        def blk(j):
            return j

    spec = pl.BlockSpec((1, hg, tb, hd), lambda i, g, j: (i, g, blk(j), 0))
    in_specs = [spec, spec, spec]
    args = [qs, lf, v]
    if readout:
        in_specs += [spec, spec, pl.BlockSpec((1, hd), lambda i, g, j: (0, 0))]
        args += [of, gs, norm_g.reshape(1, hd)]
    out_dtype = BF16 if readout else F32
    vmem = 2 * hg * tb * hd * (2 + 4 + 2 + 4 + 2 + 4) + hg * hd * hd * 4 + 8 * tb * tb * 4 + 16 * tb * hd * 4
    return pl.pallas_call(
        functools.partial(_gla_kernel, cfg.gla_chunk, reverse, readout, cfg.fast_decay_limit),
        grid=(b, nh // hg, nblk),
        in_specs=in_specs,
        out_specs=spec,
        out_shape=jax.ShapeDtypeStruct((b, nh, s, hd), out_dtype),
        scratch_shapes=[pltpu.VMEM((hg, hd, hd), F32), pltpu.VMEM((grp, tb, hd), F32),
                        pltpu.VMEM((grp, tb, hd), F32),
                        pltpu.VMEM((grp, tb // cfg.gla_chunk, cfg.gla_chunk, cfg.gla_chunk), F32)],
        compiler_params=_cparams(("parallel", "parallel", "arbitrary"), vmem),
        name="hgrn_scan_bwd" if reverse else "hgrn_scan_fwd",
    )(*args)


def _out_proj_kernel(n_heads, y_ref, x_ref, w_ref, gt_ref, o_ref, wbf_ref):
    _cast_weight_once(w_ref, wbf_ref)
    y = jnp.concatenate([y_ref[0, h] for h in range(n_heads)], axis=-1)
    o_ref[0] = x_ref[0] + gt_ref[0] * jnp.dot(y, wbf_ref[...], preferred_element_type=F32)


def _out_proj(cfg, y_heads, x, w_out_all, layer, gate, ctx_tiles):
    b, t, d = x.shape
    nh, hd = y_heads.shape[1], y_heads.shape[3]
    tm = cfg.tok_tile
    return pl.pallas_call(
        functools.partial(_out_proj_kernel, nh),
        grid=(b, t // tm),
        in_specs=[pl.BlockSpec((1, nh, tm, hd), lambda i, j: (i, 0, j + ctx_tiles, 0)),
                  pl.BlockSpec((1, tm, d), lambda i, j: (i, j, 0)),
                  pl.BlockSpec((1, d, d), lambda i, j: (layer, 0, 0), pipeline_mode=pl.Buffered(1)),
                  pl.BlockSpec((1, 1, d), lambda i, j: (i, 0, 0))],
        out_specs=pl.BlockSpec((1, tm, d), lambda i, j: (i, j, 0)),
        out_shape=jax.ShapeDtypeStruct((b, t, d), F32),
        scratch_shapes=[pltpu.VMEM((d, d), BF16)],
        compiler_params=_cparams(("arbitrary", "arbitrary"), d * d * 6 + 8 * tm * d * 4),
        name="hgrn_out_proj",
    )(y_heads, x, w_out_all, gate)


def _hgrn_mixer(cfg, x, ctx, g, sh_all, sc_all, gate, w_in_all, norm_g, w_out_all, layer, lb):
    b, t, d = x.shape
    tm = cfg.tok_tile
    h_all = _norm_mod_seq(ctx, x, g, sh_all, sc_all, BF16, tm)
    log_lb = jnp.log(lb)
    log1m_lb = jnp.log1p(-lb)
    qs = _proj(cfg, h_all, w_in_all, layer, 0, "silu", BF16)
    gs = _proj(cfg, h_all, w_in_all, layer, 1, "silu", BF16)
    lf = _proj(cfg, h_all, w_in_all, layer, 2, "gate", F32, log_lb[0], log1m_lb[0])
    lr = _proj(cfg, h_all, w_in_all, layer, 3, "gate", F32, log_lb[1], log1m_lb[1])
    v = _proj(cfg, h_all, w_in_all, layer, 4, "none", BF16)
    n_ctx_blocks = ctx.shape[1] // tm
    of = _gla(cfg, qs, lf, v, False, n_ctx_blocks)
    y = _gla(cfg, qs, lr, v, True, n_ctx_blocks, of, gs, norm_g)
    return _out_proj(cfg, y, x, w_out_all, layer, gate, n_ctx_blocks)


def _forward(cfg, x, c, ctx, c_ctx, norm_mix, norm_ffn, w_ada, b_ada, pool_w, pool_scale, hgrn_w_in, hgrn_norm,
             hgrn_w_out, hgrn_lb_logits, router_w_group, router_b_group, router_w_expert, router_b_expert,
             moe_w_gate_up, moe_w_down, norm_final):
    b, t, d = x.shape
    depth = w_ada.shape[0]
    n_mixers = 2
    p_lb = jax.nn.softmax(hgrn_lb_logits.astype(F32), axis=0)
    lb_all = jnp.cumsum(p_lb, axis=0) - p_lb[0]

    rows = -(-(b + 1) // SUBLANES_V7X) * SUBLANES_V7X
    cond = jnp.zeros((rows, d), F32).at[:b].set(c).at[b].set(c_ctx)
    mods = _adaln(cond, w_ada, b_ada)

    ng, ne = cfg.n_groups, cfg.n_groups * cfg.experts_per_group
    for i in range(depth):
        ctx_needed = i < depth - 1
        j = i // n_mixers
        m = [mods[i, :, q * d:(q + 1) * d].reshape(rows, 1, d) for q in range(6)]
        sh_m, sc_m, gt_m, sh_f, sc_f, gt_f = m
        ctx_rows = lambda a: jnp.broadcast_to(a[b:b + 1], (b, 1, d))
        if i % n_mixers == 0:
            wp = pool_w[j].astype(BF16)
            x_new = _pool_mixer(cfg, x, norm_mix[i], sh_m, sc_m, gt_m, wp, pool_scale[j], True)
            if ctx_needed:
                ctx = _pool_mixer(cfg, ctx, norm_mix[i], ctx_rows(sh_m), ctx_rows(sc_m), ctx_rows(gt_m), wp,
                                  pool_scale[j], False)
            x = x_new
        else:
            x = _hgrn_mixer(cfg, x, ctx, norm_mix[i], sh_m, sc_m, gt_m, hgrn_w_in, hgrn_norm[j], hgrn_w_out, j,
                            lb_all[i])
        lanes = -(-(ng + ne) // LANES_V7X) * LANES_V7X
        w_r = jnp.zeros((d, lanes), F32).at[:, :ng].set(router_w_group[i]).at[:, ng:ng + ne].set(router_w_expert[i])
        b_r = jnp.zeros((1, lanes), F32).at[0, :ng].set(router_b_group[i]).at[0, ng:ng + ne].set(router_b_expert[i])
        slots, wts, n_all = _moe(cfg, x, ctx if ctx_needed else None, norm_ffn[i], sh_f, sc_f, w_r, b_r,
                                 moe_w_gate_up, moe_w_down, i)
        tm = min(cfg.tok_tile, t)
        last = i == depth - 1
        x_next = _combine(x, slots, wts, 0, n_all, gt_f, tm, norm_final if last else None)
        if ctx_needed:
            ctx = _combine(ctx, slots, wts, b * t, n_all, ctx_rows(gt_f), min(cfg.tok_tile, ctx.shape[1]))
        x = x_next
    return x


def kernel(x, c, ctx, c_ctx, norm_mix, norm_ffn, w_ada, b_ada, pool_w, pool_scale, hgrn_w_in, hgrn_norm, hgrn_w_out, hgrn_lb_logits, router_w_group, router_b_group, router_w_expert, router_b_expert, moe_w_gate_up, moe_w_down, norm_final):
    return _forward(Cfg(), x, c, ctx, c_ctx, norm_mix, norm_ffn, w_ada, b_ada, pool_w, pool_scale, hgrn_w_in,
                    hgrn_norm, hgrn_w_out, hgrn_lb_logits, router_w_group, router_b_group, router_w_expert,
                    router_b_expert, moe_w_gate_up, moe_w_down, norm_final)
```

```python
import functools
from typing import NamedTuple

import numpy as np
import jax
import jax.numpy as jnp
from jax import lax
from jax.experimental import pallas as pl
from jax.experimental.pallas import tpu as pltpu

F32 = jnp.float32
BF16 = jnp.bfloat16
HIGHEST = lax.Precision.HIGHEST
EPS = 1e-6

LANES_V7X = 128
SUBLANES_V7X = 8
VMEM_BYTES_V7X = 64 * 1024 * 1024


class Cfg(NamedTuple):
    grid_w: int = 64
    pool_windows: tuple = (2, 4, 8, 16)
    head_dim: int = LANES_V7X
    n_groups: int = 4
    experts_per_group: int = 8
    top_k: int = 2
    moe_rows: int = 256
    gla_chunk: int = 64
    gla_group: int = 4
    tok_tile: int = 256
    fast_decay_limit: float = 120.0


def _vmem_limit(nbytes):
    return int(min(max(nbytes * 5 // 4, 16 * 1024 * 1024), VMEM_BYTES_V7X - 6 * 1024 * 1024))


def _cparams(sem, vmem_bytes):
    return pltpu.CompilerParams(dimension_semantics=sem, vmem_limit_bytes=_vmem_limit(vmem_bytes))


def _rms_mod(x, g, shift, scale):
    y = x * lax.rsqrt(jnp.mean(x * x, axis=-1, keepdims=True) + EPS) * g
    return y * (1.0 + scale) + shift


def _silu(x):
    return x * jax.nn.sigmoid(x)


def _store_slabs(ref, val):
    sub = ref.shape[-1]
    for s in range(ref.shape[-2]):
        ref[:, s, :] = val[:, s * sub:(s + 1) * sub]


def _load_slabs(ref):
    return jnp.concatenate([ref[:, s, :] for s in range(ref.shape[-2])], axis=-1)


def _adaln_kernel(cond_ref, w_ref, b_ref, o_ref):
    s = _silu(cond_ref[...])
    o_ref[0] = jnp.dot(s, w_ref[0], precision=HIGHEST, preferred_element_type=F32) + b_ref[0]


def _adaln(cond, w_ada, b_ada):
    depth, d, n = w_ada.shape
    rows = cond.shape[0]
    tn = 1024 if n % 1024 == 0 else n
    return pl.pallas_call(
        _adaln_kernel,
        grid=(depth, n // tn),
        in_specs=[pl.BlockSpec((rows, d), lambda l, j: (0, 0)),
                  pl.BlockSpec((1, d, tn), lambda l, j: (l, 0, j)),
                  pl.BlockSpec((1, 1, tn), lambda l, j: (l, 0, j))],
        out_specs=pl.BlockSpec((1, rows, tn), lambda l, j: (l, 0, j)),
        out_shape=jax.ShapeDtypeStruct((depth, rows, n), F32),
        compiler_params=_cparams(("arbitrary", "arbitrary"), 2 * d * tn * 4 + 4 * rows * (d + tn) * 4),
        name="adaln",
    )(cond, w_ada, b_ada.reshape(depth, 1, n))


def _norm_mod_kernel(x_ref, g_ref, sh_ref, sc_ref, o_ref):
    o_ref[0] = _rms_mod(x_ref[0], g_ref[...], sh_ref[0], sc_ref[0]).astype(o_ref.dtype)


def _norm_mod(x, g, sh, sc, out_dtype, tm):
    b, t, d = x.shape
    return pl.pallas_call(
        _norm_mod_kernel,
        grid=(b, t // tm),
        in_specs=[pl.BlockSpec((1, tm, d), lambda i, j: (i, j, 0)),
                  pl.BlockSpec((1, d), lambda i, j: (0, 0)),
                  pl.BlockSpec((1, 1, d), lambda i, j: (i, 0, 0)),
                  pl.BlockSpec((1, 1, d), lambda i, j: (i, 0, 0))],
        out_specs=pl.BlockSpec((1, tm, d), lambda i, j: (i, j, 0)),
        out_shape=jax.ShapeDtypeStruct((b, t, d), out_dtype),
        compiler_params=_cparams(("parallel", "parallel"), 6 * tm * d * 4),
        name="norm_mod",
    )(x, g.reshape(1, d), sh, sc)


def _norm_mod_seq_kernel(n_ctx_tiles, c_ref, x_ref, g_ref, sh_ref, sc_ref, o_ref):
    j = pl.program_id(1)
    src = jnp.where(j < n_ctx_tiles, c_ref[0], x_ref[0])
    o_ref[0] = _rms_mod(src, g_ref[...], sh_ref[0], sc_ref[0]).astype(o_ref.dtype)


def _norm_mod_seq(ctx, x, g, sh_all, sc_all, out_dtype, tm):
    b, t, d = x.shape
    l = ctx.shape[1]
    nc, nx = l // tm, t // tm
    return pl.pallas_call(
        functools.partial(_norm_mod_seq_kernel, nc),
        grid=(b, nc + nx),
        in_specs=[pl.BlockSpec((1, tm, d), lambda i, j: (i, jnp.minimum(j, nc - 1), 0)),
                  pl.BlockSpec((1, tm, d), lambda i, j: (i, jnp.maximum(j - nc, 0), 0)),
                  pl.BlockSpec((1, d), lambda i, j: (0, 0)),
                  pl.BlockSpec((1, 1, d), lambda i, j: (jnp.where(j < nc, b, i), 0, 0)),
                  pl.BlockSpec((1, 1, d), lambda i, j: (jnp.where(j < nc, b, i), 0, 0))],
        out_specs=pl.BlockSpec((1, tm, d), lambda i, j: (i, j, 0)),
        out_shape=jax.ShapeDtypeStruct((b, l + t, d), out_dtype),
        compiler_params=_cparams(("parallel", "arbitrary"), 8 * tm * d * 4),
        name="norm_mod_seq",
    )(ctx, x, g.reshape(1, d), sh_all, sc_all)


def _win_matrix(length, k):
    pos = np.arange(length)
    lo = np.clip(pos - k // 2, 0, length - 1)
    hi = np.clip(pos + (k - k // 2 - 1), 0, length - 1)
    m = np.zeros((length, length), np.float32)
    for p in range(length):
        m[p, lo[p]:hi[p] + 1] = 1.0 / float(hi[p] - lo[p] + 1)
    return m, (1.0 / (hi - lo + 1)).astype(np.float32)


def _pool_tokens_kernel(k, w, slab, vertical, inv_ref, ph_ref, h_ref, o_ref, pad_ref):
    t, tc = h_ref.shape[1], h_ref.shape[2]
    if not vertical:
        for s in range(t // slab):
            hs = h_ref[0, s * slab:(s + 1) * slab, :]
            o_ref[0, s * slab:(s + 1) * slab, :] = (
                jnp.dot(ph_ref[...], hs, precision=HIGHEST, preferred_element_type=F32) - hs)
        return
    top = (k // 2) * w
    bot = (k - k // 2 - 1) * w
    pad_ref[0:top, :] = jnp.zeros((top, tc), F32)
    if bot:
        pad_ref[top + t:top + t + bot, :] = jnp.zeros((bot, tc), F32)
    for s in range(t // slab):
        hs = h_ref[0, s * slab:(s + 1) * slab, :]
        pad_ref[top + s * slab:top + (s + 1) * slab, :] = jnp.dot(
            ph_ref[...], hs, precision=HIGHEST, preferred_element_type=F32)

    def row(r, carry):
        base = pl.multiple_of(r * w, w)
        acc = pad_ref[pl.ds(base, w), :]
        for dr in range(1, k):
            acc = acc + pad_ref[pl.ds(pl.multiple_of(base + dr * w, w), w), :]
        o_ref[0, pl.ds(base, w), :] = acc * inv_ref[r] - h_ref[0, pl.ds(base, w), :]
        return carry

    lax.fori_loop(0, t // w, row, 0)


def _pool_tokens(h, group, gdim, k, grid_w, vertical):
    b, t, d = h.shape
    tc = min(gdim, 256)
    per = gdim // tc
    if vertical:
        slab = max(grid_w, min(256, t))
        mh, _ = _win_matrix(grid_w, k)
        ph = np.kron(np.eye(slab // grid_w, dtype=np.float32), mh)
        _, inv_v = _win_matrix(t // grid_w, k)
    else:
        slab = t
        ph, _ = _win_matrix(t, k)
        inv_v = np.ones((1,), np.float32)
    pad_rows = t + (k - 1) * grid_w if vertical else SUBLANES_V7X
    return pl.pallas_call(
        functools.partial(_pool_tokens_kernel, k, grid_w, slab, vertical),
        grid=(b, per),
        in_specs=[pl.BlockSpec(memory_space=pltpu.SMEM),
                  pl.BlockSpec((slab, slab), lambda i, j: (0, 0)),
                  pl.BlockSpec((1, t, tc), lambda i, j: (i, 0, group * per + j))],
        out_specs=pl.BlockSpec((1, t, tc), lambda i, j: (i, 0, j)),
        out_shape=jax.ShapeDtypeStruct((b, t, gdim), F32),
        scratch_shapes=[pltpu.VMEM((pad_rows, tc), F32)],
        compiler_params=_cparams(("parallel", "parallel"), (4 * t + pad_rows) * tc * 4 + 2 * slab * slab * 4),
        name=f"pool_tokens_k{k}",
    )(jnp.asarray(inv_v), jnp.asarray(ph), h)


def _pool_out_kernel(n_groups, *refs):
    d_refs = refs[:n_groups]
    x_ref, w_ref, ps_ref, gt_ref, o_ref = refs[n_groups:]
    ys = [jnp.dot(d_refs[j][0].astype(BF16), w_ref[j], preferred_element_type=F32) for j in range(n_groups)]
    y = jnp.concatenate(ys, axis=-1) * ps_ref[...]
    o_ref[0] = x_ref[0] + gt_ref[0] * y


def _pool_out(ds, x, w_pool_bf16, pool_scale, gate, tm):
    b, t, d = x.shape
    ng, gdim, _ = w_pool_bf16.shape
    return pl.pallas_call(
        functools.partial(_pool_out_kernel, ng),
        grid=(b, t // tm),
        in_specs=[pl.BlockSpec((1, tm, gdim), lambda i, j: (i, j, 0)) for _ in range(ng)] + [
            pl.BlockSpec((1, tm, d), lambda i, j: (i, j, 0)),
            pl.BlockSpec((ng, gdim, gdim), lambda i, j: (0, 0, 0)),
            pl.BlockSpec((1, d), lambda i, j: (0, 0)),
            pl.BlockSpec((1, 1, d), lambda i, j: (i, 0, 0))],
        out_specs=pl.BlockSpec((1, tm, d), lambda i, j: (i, j, 0)),
        out_shape=jax.ShapeDtypeStruct((b, t, d), F32),
        compiler_params=_cparams(("parallel", "parallel"), 8 * tm * d * 4 + 2 * ng * gdim * gdim * 2),
        name="pool_out",
    )(*ds, x, w_pool_bf16, pool_scale.reshape(1, d), gate)


def _pool_mixer(cfg, x, g, sh, sc, gate, w_pool_bf16, pool_scale, on_grid):
    b, t, d = x.shape
    tm = min(cfg.tok_tile, t)
    h = _norm_mod(x, g, sh, sc, F32, tm)
    gdim = d // len(cfg.pool_windows)
    ds = [_pool_tokens(h, j, gdim, k, cfg.grid_w, on_grid) for j, k in enumerate(cfg.pool_windows)]
    return _pool_out(ds, x, w_pool_bf16, pool_scale, gate, tm)


def _router_kernel(nx_tiles, has_ctx, *refs):
    if has_ctx:
        x_ref, c_ref, g_ref, sh_ref, sc_ref, wr_ref, br_ref, fx_ref, lg_ref = refs
        src = jnp.where(pl.program_id(0) < nx_tiles, x_ref[0], c_ref[0])
    else:
        x_ref, g_ref, sh_ref, sc_ref, wr_ref, br_ref, fx_ref, lg_ref = refs
        src = x_ref[0]
    h = _rms_mod(src, g_ref[...], sh_ref[0], sc_ref[0])
    _store_slabs(fx_ref, h)
    lg_ref[...] = jnp.dot(h, wr_ref[...], precision=HIGHEST, preferred_element_type=F32) + br_ref[...]


def _router(x, ctx, g, sh_all, sc_all, w_r, b_r, tm):
    b, t, d = x.shape
    per_b = t // tm
    nx = b * per_b
    has_ctx = ctx is not None
    nc = b * (ctx.shape[1] // tm) if has_ctx else 0
    per_c = (ctx.shape[1] // tm) if has_ctx else 1
    n_all = (nx + nc) * tm
    lanes = w_r.shape[1]

    def mod_row(i):
        return jnp.where(i < nx, jnp.minimum(i, nx - 1) // per_b, b) if has_ctx else i // per_b

    in_specs = [pl.BlockSpec((1, tm, d), lambda i: (jnp.minimum(i, nx - 1) // per_b, jnp.minimum(i, nx - 1) % per_b, 0))]
    args = [x]
    if has_ctx:
        in_specs.append(pl.BlockSpec(
            (1, tm, d), lambda i: (jnp.maximum(i - nx, 0) // per_c, jnp.maximum(i - nx, 0) % per_c, 0)))
        args.append(ctx)
    in_specs += [pl.BlockSpec((1, d), lambda i: (0, 0)),
                 pl.BlockSpec((1, 1, d), lambda i: (mod_row(i), 0, 0)),
                 pl.BlockSpec((1, 1, d), lambda i: (mod_row(i), 0, 0)),
                 pl.BlockSpec((d, lanes), lambda i: (0, 0)),
                 pl.BlockSpec((1, lanes), lambda i: (0, 0))]
    args += [g.reshape(1, d), sh_all, sc_all, w_r, b_r]
    return pl.pallas_call(
        functools.partial(_router_kernel, nx, has_ctx),
        grid=(nx + nc,),
        in_specs=in_specs,
        out_specs=[pl.BlockSpec((tm, SUBLANES_V7X, d // SUBLANES_V7X), lambda i: (i, 0, 0)),
                   pl.BlockSpec((tm, lanes), lambda i: (i, 0))],
        out_shape=[jax.ShapeDtypeStruct((n_all, SUBLANES_V7X, d // SUBLANES_V7X), F32),
                   jax.ShapeDtypeStruct((n_all, lanes), F32)],
        compiler_params=_cparams(("arbitrary",), 8 * tm * d * 4 + 2 * d * lanes * 4),
        name="moe_router",
    )(*args)


def _expert_kernel(rows, n_all, col, be_ref, nu_ref, code_ref, fx_hbm, wgu_ref, wdn_ref, out_hbm,
                   xbuf, ybuf, gsem, ssem):
    i = pl.program_id(0)
    n_used = nu_ref[0]
    slot = i % 2
    other = 1 - slot
    n_slots = n_all * 2
    last_blk = code_ref.shape[0] // rows - 1
    f = wdn_ref.shape[2]

    def gather_row(blk, sl, r):
        v = code_ref[blk * rows + r]
        src = jnp.where(v >= n_slots, 0, v >> 1)
        return pltpu.make_async_copy(fx_hbm.at[src], xbuf.at[sl, r], gsem.at[sl])

    def scatter_row(blk, sl, r, none):
        v = jnp.where(none, n_slots, code_ref[blk * rows + r])
        dst = jnp.where(v >= n_slots, n_slots + sl * rows + r, (v & 1) * n_all + (v >> 1))
        return pltpu.make_async_copy(ybuf.at[sl, r], out_hbm.at[dst], ssem.at[sl])

    def wait_gather(sl):
        pltpu.make_async_copy(fx_hbm.at[pl.ds(0, rows)], xbuf.at[sl], gsem.at[sl]).wait()

    def wait_scatter(sl):
        pltpu.make_async_copy(ybuf.at[sl], out_hbm.at[pl.ds(0, rows)], ssem.at[sl]).wait()

    @pl.when(i == 0)
    def _():
        ybuf[...] = jnp.zeros(ybuf.shape, F32)
        for sl in range(2):
            pltpu.make_async_copy(ybuf.at[sl], out_hbm.at[pl.ds(n_slots + sl * rows, rows)], ssem.at[sl]).start()
        wait_scatter(1)

        @pl.when(n_used > 0)
        def _():
            def body(r, c):
                gather_row(0, 0, r).start()
                return c
            lax.fori_loop(0, rows, body, 0)

        @pl.when(n_used == 0)
        def _():
            wait_scatter(0)

    @pl.when(i < n_used)
    def _():
        wait_gather(slot)
        nxt = jnp.minimum(i + 1, last_blk)
        prev = jnp.maximum(i - 1, 0)
        no_prev = i == 0
        xb = _load_slabs(xbuf.at[slot]).astype(BF16)
        n1 = f // col
        per1 = rows // n1
        acts = []
        for j in range(n1):
            g = jnp.dot(xb, wgu_ref[0, 0, :, j * col:(j + 1) * col].astype(BF16), preferred_element_type=F32)
            u = jnp.dot(xb, wgu_ref[0, 0, :, f + j * col:f + (j + 1) * col].astype(BF16),
                        preferred_element_type=F32)
            acts.append((_silu(g) * u).astype(BF16))
            for r in range(j * per1, (j + 1) * per1):
                scatter_row(prev, other, r, no_prev).start()
                gather_row(nxt, other, r).start()
        act = jnp.concatenate(acts, axis=-1)
        wait_scatter(slot)
        n2, sub = ybuf.shape[2], ybuf.shape[3]
        for j in range(n2):
            ybuf[slot, :, j, :] = jnp.dot(act, wdn_ref[0, 0, :, j * sub:(j + 1) * sub].astype(BF16),
                                          preferred_element_type=F32)

    @pl.when(jnp.logical_and(i == n_used, i >= 1))
    def _():
        wait_gather(slot)

        def body(r, c):
            scatter_row(i - 1, other, r, False).start()
            return c
        lax.fori_loop(0, rows, body, 0)
        wait_scatter(other)
        wait_scatter(slot)


def _experts(cfg, fx, block_e, n_used, row_code, w_gu, w_dn, layer):
    n_all, n_sub, sub = fx.shape
    d = n_sub * sub
    f2 = w_gu.shape[3]
    f = f2 // 2
    rows = cfg.moe_rows
    col = min(256, f)
    n_steps = block_e.shape[0]
    grid_spec = pltpu.PrefetchScalarGridSpec(
        num_scalar_prefetch=3,
        grid=(n_steps,),
        in_specs=[pl.BlockSpec(memory_space=pl.ANY),
                  pl.BlockSpec((1, 1, d, f2), lambda i, be, nu, cd: (layer, be[i], 0, 0),
                               pipeline_mode=pl.Buffered(1)),
                  pl.BlockSpec((1, 1, f, d), lambda i, be, nu, cd: (layer, be[i], 0, 0))],
        out_specs=pl.BlockSpec(memory_space=pl.ANY),
        scratch_shapes=[pltpu.VMEM((2, rows, n_sub, sub), F32), pltpu.VMEM((2, rows, n_sub, sub), F32),
                        pltpu.SemaphoreType.DMA((2,)), pltpu.SemaphoreType.DMA((2,))])
    vmem = (d * f2 + 2 * f * d) * 4 + 4 * rows * d * 4 + 3 * rows * f2 * 4 + 4 * d * col * 4
    return pl.pallas_call(
        functools.partial(_expert_kernel, rows, n_all, col),
        grid_spec=grid_spec,
        out_shape=jax.ShapeDtypeStruct((cfg.top_k * n_all + 2 * rows, n_sub, sub), F32),
        compiler_params=_cparams(("arbitrary",), vmem),
        name="moe_experts",
    )(block_e, n_used, row_code, fx, w_gu, w_dn)


def _combine_kernel(final, *refs):
    if final:
        x_ref, a_ref, b_ref, w_ref, gt_ref, gf_ref, o_ref = refs
    else:
        x_ref, a_ref, b_ref, w_ref, gt_ref, o_ref = refs
    w = w_ref[...]
    y = x_ref[0] + gt_ref[0] * (w[:, 0:1] * _load_slabs(a_ref) + w[:, 1:2] * _load_slabs(b_ref))
    if final:
        y = y * lax.rsqrt(jnp.mean(y * y, axis=-1, keepdims=True) + EPS) * gf_ref[...]
    o_ref[0] = y


def _combine(x, slots, wts, tok_off, n_all, gate, tm, norm_final=None):
    b, t, d = x.shape
    per_b = t // tm
    off0 = tok_off // tm
    off1 = (n_all + tok_off) // tm
    final = norm_final is not None
    in_specs = [pl.BlockSpec((1, tm, d), lambda i, j: (i, j, 0)),
                pl.BlockSpec((tm,) + slots.shape[1:], lambda i, j: (off0 + i * per_b + j, 0, 0)),
                pl.BlockSpec((tm,) + slots.shape[1:], lambda i, j: (off1 + i * per_b + j, 0, 0)),
                pl.BlockSpec((tm, wts.shape[1]), lambda i, j: (off0 + i * per_b + j, 0)),
                pl.BlockSpec((1, 1, d), lambda i, j: (i, 0, 0))]
    args = [x, slots, slots, wts, gate]
    if final:
        in_specs.append(pl.BlockSpec((1, d), lambda i, j: (0, 0)))
        args.append(norm_final.reshape(1, d))
    return pl.pallas_call(
        functools.partial(_combine_kernel, final),
        grid=(b, per_b),
        in_specs=in_specs,
        out_specs=pl.BlockSpec((1, tm, d), lambda i, j: (i, j, 0)),
        out_shape=jax.ShapeDtypeStruct((b, t, d), F32),
        compiler_params=_cparams(("parallel", "parallel"), 10 * tm * d * 4),
        name="moe_combine",
    )(*args)


def _route(cfg, logits):
    n_all = logits.shape[0]
    ng, eg, k = cfg.n_groups, cfg.experts_per_group, cfg.top_k
    assert k == 2, "row codes pack (token, choice) as 2 * token + choice"
    n_exp = ng * eg
    rows = cfg.moe_rows
    p_grp = jax.nn.softmax(logits[:, :ng], axis=-1)
    g_idx = jnp.argmax(p_grp, axis=-1)
    p_g = jnp.max(p_grp, axis=-1)
    le = logits[:, ng:ng + n_exp].reshape(n_all, ng, eg)
    le = jnp.take_along_axis(le, g_idx[:, None, None], axis=1)[:, 0]
    vals, e_in = lax.top_k(jax.nn.softmax(le, axis=-1), k)
    wts = p_g[:, None] * vals / jnp.sum(vals, axis=-1, keepdims=True)
    eid = (g_idx[:, None] * eg + e_in).astype(jnp.int32)

    s = n_all * k
    e_flat = eid.reshape(-1)
    onehot = (e_flat[:, None] == jnp.arange(n_exp, dtype=jnp.int32)[None, :]).astype(jnp.int32)
    csum = jnp.cumsum(onehot, axis=0)
    rank = jnp.sum((csum - onehot) * onehot, axis=1)
    counts = csum[-1]
    padded = ((counts + rows - 1) // rows) * rows
    pends = jnp.cumsum(padded)
    pstarts = pends - padded
    dest = pstarts[e_flat] + rank
    n_blocks = -(-(s + n_exp * (rows - 1)) // rows)
    p = n_blocks * rows
    row_code = jnp.full((p,), s, jnp.int32).at[dest].set(jnp.arange(s, dtype=jnp.int32))
    blk_start = jnp.arange(n_blocks + 1, dtype=jnp.int32) * rows
    block_e = jnp.minimum(jnp.sum((pends[None, :] <= blk_start[:, None]).astype(jnp.int32), axis=1), n_exp - 1)
    n_used = (pends[-1] // rows).astype(jnp.int32).reshape(1)
    return (block_e, n_used, row_code), wts


def _moe(cfg, x, ctx, g, sh_all, sc_all, w_r, b_r, w_gu, w_dn, layer):
    b, t, d = x.shape
    tm = min(cfg.tok_tile, t) if ctx is None else min(cfg.tok_tile, t, ctx.shape[1])
    fx, logits = _router(x, ctx, g, sh_all, sc_all, w_r, b_r, tm)
    n_all = fx.shape[0]
    plan, wts = _route(cfg, logits)
    slots = _experts(cfg, fx, *plan, w_gu, w_dn, layer)
    return slots, wts, n_all


def _cast_weight_once(w_ref, wbf_ref):
    @pl.when(jnp.logical_and(pl.program_id(0) == 0, pl.program_id(1) == 0))
    def _():
        k = wbf_ref.shape[0]
        step = min(k, 256)

        def body(c, carry):
            r0 = pl.multiple_of(c * step, step)
            wbf_ref[pl.ds(r0, step), :] = w_ref[0, pl.ds(r0, step), :].astype(BF16)
            return carry
        lax.fori_loop(0, k // step, body, 0)


def _proj_kernel(kind, n_heads, hd, *refs):
    if kind == "gate":
        h_ref, w_ref, la_ref, lc_ref, o_ref, wbf_ref = refs
    else:
        h_ref, w_ref, o_ref, wbf_ref = refs
    _cast_weight_once(w_ref, wbf_ref)
    z = jnp.dot(h_ref[0], wbf_ref[...], preferred_element_type=F32)
    if kind == "silu":
        z = _silu(z)
    elif kind == "gate":
        ls = jnp.minimum(z, 0.0) - jnp.log1p(jnp.exp(-jnp.abs(z)))
        c = lc_ref[...] + ls
        a = jnp.broadcast_to(la_ref[...], c.shape)
        m = jnp.maximum(a, c)
        z = m + jnp.log1p(jnp.exp(-jnp.abs(a - c)))
    for h in range(n_heads):
        o_ref[0, h] = z[:, h * hd:(h + 1) * hd].astype(o_ref.dtype)


def _proj(cfg, h_all, w_all, layer, section, kind, out_dtype, log_lb=None, log1m_lb=None):
    b, s, d = h_all.shape
    n = d
    hd = cfg.head_dim
    nh = n // hd
    tm = cfg.tok_tile
    in_specs = [pl.BlockSpec((1, tm, d), lambda i, j: (i, j, 0)),
                pl.BlockSpec((1, d, n), lambda i, j: (layer, 0, section), pipeline_mode=pl.Buffered(1))]
    args = [h_all, w_all]
    if kind == "gate":
        in_specs += [pl.BlockSpec((1, n), lambda i, j: (0, 0)), pl.BlockSpec((1, n), lambda i, j: (0, 0))]
        args += [log_lb.reshape(1, n), log1m_lb.reshape(1, n)]
    return pl.pallas_call(
        functools.partial(_proj_kernel, kind, nh, hd),
        grid=(b, s // tm),
        in_specs=in_specs,
        out_specs=pl.BlockSpec((1, nh, tm, hd), lambda i, j: (i, 0, j, 0)),
        out_shape=jax.ShapeDtypeStruct((b, nh, s, hd), out_dtype),
        scratch_shapes=[pltpu.VMEM((d, n), BF16)],
        compiler_params=_cparams(("arbitrary", "arbitrary"), d * n * 6 + 2 * tm * d * 2 + 6 * tm * n * 4),
        name=f"hgrn_proj_{kind}",
    )(*args)


def _gla_kernel(chunk, reverse, readout, limit, *refs):
    if readout:
        q_ref, lf_ref, v_ref, of_ref, gs_ref, ng_ref, o_ref, st_ref, b_ref, q32_ref, at_ref = refs
    else:
        q_ref, lf_ref, v_ref, o_ref, st_ref, b_ref, q32_ref, at_ref = refs
    hg, tb, hd = q_ref.shape[1], q_ref.shape[2], q_ref.shape[3]
    n_chunks = tb // chunk
    shift = chunk.bit_length() - 1

    @pl.when(pl.program_id(2) == 0)
    def _():
        st_ref[...] = jnp.zeros(st_ref.shape, F32)

    rr = lax.broadcasted_iota(jnp.int32, (tb, tb), 0)
    cc = lax.broadcasted_iota(jnp.int32, (tb, tb), 1)
    same = (rr >> shift) == (cc >> shift)
    tri = jnp.where(jnp.logical_and(same, (cc >= rr) if reverse else (cc <= rr)), 1.0, 0.0).astype(BF16)
    ar = lax.broadcasted_iota(jnp.int32, (chunk, chunk), 0)
    ac = lax.broadcasted_iota(jnp.int32, (chunk, chunk), 1)
    at_mask = (ar >= ac) if reverse else (ar <= ac)
    row_id = lax.broadcasted_iota(jnp.int32, (chunk, hd), 0)
    order = list(range(n_chunks))[::-1] if reverse else list(range(n_chunks))
    edge = 0 if reverse else chunk - 1
    grp = b_ref.shape[0]

    def group(hp, carry):
        heads = [hp * grp + u for u in range(grp)]
        lf_all = jnp.concatenate([lf_ref[0, h] for h in heads], axis=-1)
        hi = lf_all.astype(BF16)
        r1 = lf_all - hi.astype(F32)
        mid = r1.astype(BF16)
        lo = (r1 - mid.astype(F32)).astype(BF16)
        b_all = (jnp.dot(tri, hi, preferred_element_type=F32) + jnp.dot(tri, mid, preferred_element_type=F32)
                 + jnp.dot(tri, lo, preferred_element_type=F32))
        bs, kks, qfs, btots = [], [], [], []
        worst = None
        for u, h in enumerate(heads):
            b = b_all[:, u * hd:(u + 1) * hd]
            b_ref[u] = b
            btot = [b[c * chunk + edge:c * chunk + edge + 1, :] for c in range(n_chunks)]
            for bt in btot:
                worst = bt if worst is None else jnp.minimum(worst, bt)
            bs.append(b)
            btots.append(btot)
            kks.append(1.0 - jnp.exp(lf_all[:, u * hd:(u + 1) * hd]))
            qfs.append(q_ref[0, h].astype(F32))
        fast = jnp.min(worst) >= -limit

        @pl.when(fast)
        def _():
            for u in range(grp):
                for c in range(n_chunks):
                    sl = slice(c * chunk, (c + 1) * chunk)
                    d = bs[u][sl] - 0.5 * btots[u][c]
                    qh = (qfs[u][sl] * jnp.exp(d)).astype(BF16)
                    kh = (kks[u][sl] * jnp.exp(-d)).astype(BF16)
                    at = lax.dot_general(kh, qh, (((1,), (1,)), ((), ())), preferred_element_type=F32)
                    at_ref[u, c] = jnp.where(at_mask, at, 0.0)

        @pl.when(jnp.logical_not(fast))
        def _():
            for u in range(grp):
                q32_ref[u] = qfs[u]
                for c in range(n_chunks):
                    c0 = c * chunk
                    bc = bs[u][c0:c0 + chunk]
                    kc = kks[u][c0:c0 + chunk]

                    def tstep(t, at):
                        bt = b_ref[u, pl.ds(c0 + t, 1), :]
                        qrow = q32_ref[u, pl.ds(c0 + t, 1), :]
                        valid = (row_id >= t) if reverse else (row_id <= t)
                        e = jnp.exp(jnp.where(valid, bt - bc, -jnp.inf))
                        col = jnp.sum(e * kc * qrow, axis=-1, keepdims=True)
                        return at + col * jnp.where(ac == t, 1.0, 0.0)

                    at_ref[u, c] = lax.fori_loop(0, chunk, tstep, jnp.zeros((chunk, chunk), F32))

        for u, h in enumerate(heads):
            b, kk, btot = bs[u], kks[u], btots[u]
            qb = (qfs[u] * jnp.exp(b)).astype(BF16)
            vb = v_ref[0, h]
            st = st_ref[h]
            for c in order:
                sl = slice(c * chunk, (c + 1) * chunk)
                k2 = (kk[sl] * jnp.exp(btot[c] - b[sl])).astype(BF16)
                o = lax.dot_general(at_ref[u, c].astype(BF16), vb[sl], (((0,), (0,)), ((), ())),
                                    preferred_element_type=F32)
                o = o + lax.dot_general(qb[sl], st.astype(BF16), (((1,), (1,)), ((), ())),
                                        preferred_element_type=F32)
                st = st * jnp.exp(btot[c]) + lax.dot_general(vb[sl], k2, (((0,), (0,)), ((), ())),
                                                             preferred_element_type=F32)
                if readout:
                    o = o + of_ref[0, h, sl, :]
                    o = o * lax.rsqrt(jnp.mean(o * o, axis=-1, keepdims=True) + EPS) * ng_ref[...]
                    o_ref[0, h, sl, :] = (o * gs_ref[0, h, sl, :].astype(F32)).astype(o_ref.dtype)
                else:
                    o_ref[0, h, sl, :] = o
            st_ref[h] = st
        return carry

    lax.fori_loop(0, hg // grp, group, 0)


def _gla(cfg, qs, lf, v, reverse, n_ctx_blocks, of=None, gs=None, norm_g=None):
    b, nh, s, hd = qs.shape
    tb = cfg.tok_tile
    hg = nh
    grp = cfg.gla_group if hg % cfg.gla_group == 0 else 1
    nblk = s // tb
    readout = of is not None

    if reverse:
        def blk(j):
            return jnp.where(j < n_ctx_blocks, n_ctx_blocks - 1 - j, nblk - 1 - (j - n_ctx_blocks))
    else:
        def blk(j):
            return j

    spec = pl.BlockSpec((1, hg, tb, hd), lambda i, g, j: (i, g, blk(j), 0))
    in_specs = [spec, spec, spec]
    args = [qs, lf, v]
    if readout:
        in_specs += [spec, spec, pl.BlockSpec((1, hd), lambda i, g, j: (0, 0))]
        args += [of, gs, norm_g.reshape(1, hd)]
    out_dtype = BF16 if readout else F32
    vmem = 2 * hg * tb * hd * (2 + 4 + 2 + 4 + 2 + 4) + hg * hd * hd * 4 + 8 * tb * tb * 4 + 16 * tb * hd * 4
    return pl.pallas_call(
        functools.partial(_gla_kernel, cfg.gla_chunk, reverse, readout, cfg.fast_decay_limit),
        grid=(b, nh // hg, nblk),
        in_specs=in_specs,
        out_specs=spec,
        out_shape=jax.ShapeDtypeStruct((b, nh, s, hd), out_dtype),
        scratch_shapes=[pltpu.VMEM((hg, hd, hd), F32), pltpu.VMEM((grp, tb, hd), F32),
                        pltpu.VMEM((grp, tb, hd), F32),
                        pltpu.VMEM((grp, tb // cfg.gla_chunk, cfg.gla_chunk, cfg.gla_chunk), F32)],
        compiler_params=_cparams(("parallel", "parallel", "arbitrary"), vmem),
        name="hgrn_scan_bwd" if reverse else "hgrn_scan_fwd",
    )(*args)


def _out_proj_kernel(n_heads, y_ref, x_ref, w_ref, gt_ref, o_ref, wbf_ref):
    _cast_weight_once(w_ref, wbf_ref)
    y = jnp.concatenate([y_ref[0, h] for h in range(n_heads)], axis=-1)
    o_ref[0] = x_ref[0] + gt_ref[0] * jnp.dot(y, wbf_ref[...], preferred_element_type=F32)


def _out_proj(cfg, y_heads, x, w_out_all, layer, gate, ctx_tiles):
    b, t, d = x.shape
    nh, hd = y_heads.shape[1], y_heads.shape[3]
    tm = cfg.tok_tile
    return pl.pallas_call(
        functools.partial(_out_proj_kernel, nh),
        grid=(b, t // tm),
        in_specs=[pl.BlockSpec((1, nh, tm, hd), lambda i, j: (i, 0, j + ctx_tiles, 0)),
                  pl.BlockSpec((1, tm, d), lambda i, j: (i, j, 0)),
                  pl.BlockSpec((1, d, d), lambda i, j: (layer, 0, 0), pipeline_mode=pl.Buffered(1)),
                  pl.BlockSpec((1, 1, d), lambda i, j: (i, 0, 0))],
        out_specs=pl.BlockSpec((1, tm, d), lambda i, j: (i, j, 0)),
        out_shape=jax.ShapeDtypeStruct((b, t, d), F32),
        scratch_shapes=[pltpu.VMEM((d, d), BF16)],
        compiler_params=_cparams(("arbitrary", "arbitrary"), d * d * 6 + 8 * tm * d * 4),
        name="hgrn_out_proj",
    )(y_heads, x, w_out_all, gate)


def _hgrn_mixer(cfg, x, ctx, g, sh_all, sc_all, gate, w_in_all, norm_g, w_out_all, layer, lb):
    b, t, d = x.shape
    tm = cfg.tok_tile
    h_all = _norm_mod_seq(ctx, x, g, sh_all, sc_all, BF16, tm)
    log_lb = jnp.log(lb)
    log1m_lb = jnp.log1p(-lb)
    qs = _proj(cfg, h_all, w_in_all, layer, 0, "silu", BF16)
    gs = _proj(cfg, h_all, w_in_all, layer, 1, "silu", BF16)
    lf = _proj(cfg, h_all, w_in_all, layer, 2, "gate", F32, log_lb[0], log1m_lb[0])
    lr = _proj(cfg, h_all, w_in_all, layer, 3, "gate", F32, log_lb[1], log1m_lb[1])
    v = _proj(cfg, h_all, w_in_all, layer, 4, "none", BF16)
    n_ctx_blocks = ctx.shape[1] // tm
    of = _gla(cfg, qs, lf, v, False, n_ctx_blocks)
    y = _gla(cfg, qs, lr, v, True, n_ctx_blocks, of, gs, norm_g)
    return _out_proj(cfg, y, x, w_out_all, layer, gate, n_ctx_blocks)


def _forward(cfg, x, c, ctx, c_ctx, norm_mix, norm_ffn, w_ada, b_ada, pool_w, pool_scale, hgrn_w_in, hgrn_norm,
             hgrn_w_out, hgrn_lb_logits, router_w_group, router_b_group, router_w_expert, router_b_expert,
             moe_w_gate_up, moe_w_down, norm_final):
    b, t, d = x.shape
    depth = w_ada.shape[0]
    n_mixers = 2
    p_lb = jax.nn.softmax(hgrn_lb_logits.astype(F32), axis=0)
    lb_all = jnp.cumsum(p_lb, axis=0) - p_lb[0]

    rows = -(-(b + 1) // SUBLANES_V7X) * SUBLANES_V7X
    cond = jnp.zeros((rows, d), F32).at[:b].set(c).at[b].set(c_ctx)
    mods = _adaln(cond, w_ada, b_ada)

    ng, ne = cfg.n_groups, cfg.n_groups * cfg.experts_per_group
    for i in range(depth):
        ctx_needed = i < depth - 1
        j = i // n_mixers
        m = [mods[i, :, q * d:(q + 1) * d].reshape(rows, 1, d) for q in range(6)]
        sh_m, sc_m, gt_m, sh_f, sc_f, gt_f = m
        ctx_rows = lambda a: jnp.broadcast_to(a[b:b + 1], (b, 1, d))
        if i % n_mixers == 0:
            wp = pool_w[j].astype(BF16)
            x_new = _pool_mixer(cfg, x, norm_mix[i], sh_m, sc_m, gt_m, wp, pool_scale[j], True)
            if ctx_needed:
                ctx = _pool_mixer(cfg, ctx, norm_mix[i], ctx_rows(sh_m), ctx_rows(sc_m), ctx_rows(gt_m), wp,
                                  pool_scale[j], False)
            x = x_new
        else:
            x = _hgrn_mixer(cfg, x, ctx, norm_mix[i], sh_m, sc_m, gt_m, hgrn_w_in, hgrn_norm[j], hgrn_w_out, j,
                            lb_all[i])
        lanes = -(-(ng + ne) // LANES_V7X) * LANES_V7X
        w_r = jnp.zeros((d, lanes), F32).at[:, :ng].set(router_w_group[i]).at[:, ng:ng + ne].set(router_w_expert[i])
        b_r = jnp.zeros((1, lanes), F32).at[0, :ng].set(router_b_group[i]).at[0, ng:ng + ne].set(router_b_expert[i])
        slots, wts, n_all = _moe(cfg, x, ctx if ctx_needed else None, norm_ffn[i], sh_f, sc_f, w_r, b_r,
                                 moe_w_gate_up, moe_w_down, i)
        tm = min(cfg.tok_tile, t)
        last = i == depth - 1
        x_next = _combine(x, slots, wts, 0, n_all, gt_f, tm, norm_final if last else None)
        if ctx_needed:
            ctx = _combine(ctx, slots, wts, b * t, n_all, ctx_rows(gt_f), min(cfg.tok_tile, ctx.shape[1]))
        x = x_next
    return x


def kernel(x, c, ctx, c_ctx, norm_mix, norm_ffn, w_ada, b_ada, pool_w, pool_scale, hgrn_w_in, hgrn_norm, hgrn_w_out, hgrn_lb_logits, router_w_group, router_b_group, router_w_expert, router_b_expert, moe_w_gate_up, moe_w_down, norm_final):
    return _forward(Cfg(), x, c, ctx, c_ctx, norm_mix, norm_ffn, w_ada, b_ada, pool_w, pool_scale, hgrn_w_in,
                    hgrn_norm, hgrn_w_out, hgrn_lb_logits, router_w_group, router_b_group, router_w_expert,
                    router_b_expert, moe_w_gate_up, moe_w_down, norm_final)
```

```python
import functools
from typing import NamedTuple

import numpy as np
import jax
import jax.numpy as jnp
from jax import lax
from jax.experimental import pallas as pl
from jax.experimental.pallas import tpu as pltpu

F32 = jnp.float32
BF16 = jnp.bfloat16
HIGHEST = lax.Precision.HIGHEST
EPS = 1e-6

LANES_V7X = 128
SUBLANES_V7X = 8
VMEM_BYTES_V7X = 64 * 1024 * 1024


class Cfg(NamedTuple):
    grid_w: int = 64
    pool_windows: tuple = (2, 4, 8, 16)
    head_dim: int = LANES_V7X
    n_groups: int = 4
    experts_per_group: int = 8
    top_k: int = 2
    moe_rows: int = 256
    gla_chunk: int = 64
    gla_group: int = 8
    tok_tile: int = 256
    row_tile: int = 512
    fast_decay_limit: float = 120.0


def _vmem_limit(nbytes):
    return int(min(max(nbytes * 5 // 4, 16 * 1024 * 1024), VMEM_BYTES_V7X - 6 * 1024 * 1024))


def _cparams(sem, vmem_bytes):
    return pltpu.CompilerParams(dimension_semantics=sem, vmem_limit_bytes=_vmem_limit(vmem_bytes))


def _rms_mod(x, g, shift, scale):
    y = x * lax.rsqrt(jnp.mean(x * x, axis=-1, keepdims=True) + EPS) * g
    return y * (1.0 + scale) + shift


def _silu(x):
    return x * jax.nn.sigmoid(x)


def _store_slabs(ref, val):
    sub = ref.shape[-1]
    for s in range(ref.shape[-2]):
        ref[:, s, :] = val[:, s * sub:(s + 1) * sub]


def _load_slabs(ref):
    return jnp.concatenate([ref[:, s, :] for s in range(ref.shape[-2])], axis=-1)


def _adaln_kernel(cond_ref, w_ref, b_ref, o_ref):
    s = _silu(cond_ref[...])
    o_ref[0] = jnp.dot(s, w_ref[0], precision=HIGHEST, preferred_element_type=F32) + b_ref[0]


def _adaln(cond, w_ada, b_ada):
    depth, d, n = w_ada.shape
    rows = cond.shape[0]
    tn = 1024 if n % 1024 == 0 else n
    return pl.pallas_call(
        _adaln_kernel,
        grid=(depth, n // tn),
        in_specs=[pl.BlockSpec((rows, d), lambda l, j: (0, 0)),
                  pl.BlockSpec((1, d, tn), lambda l, j: (l, 0, j)),
                  pl.BlockSpec((1, 1, tn), lambda l, j: (l, 0, j))],
        out_specs=pl.BlockSpec((1, rows, tn), lambda l, j: (l, 0, j)),
        out_shape=jax.ShapeDtypeStruct((depth, rows, n), F32),
        compiler_params=_cparams(("arbitrary", "arbitrary"), 2 * d * tn * 4 + 4 * rows * (d + tn) * 4),
        name="adaln",
    )(cond, w_ada, b_ada.reshape(depth, 1, n))


def _norm_mod_kernel(x_ref, g_ref, sh_ref, sc_ref, o_ref):
    o_ref[0] = _rms_mod(x_ref[0], g_ref[...], sh_ref[0], sc_ref[0]).astype(o_ref.dtype)


def _norm_mod(x, g, sh, sc, out_dtype, tm):
    b, t, d = x.shape
    return pl.pallas_call(
        _norm_mod_kernel,
        grid=(b, t // tm),
        in_specs=[pl.BlockSpec((1, tm, d), lambda i, j: (i, j, 0)),
                  pl.BlockSpec((1, d), lambda i, j: (0, 0)),
                  pl.BlockSpec((1, 1, d), lambda i, j: (i, 0, 0)),
                  pl.BlockSpec((1, 1, d), lambda i, j: (i, 0, 0))],
        out_specs=pl.BlockSpec((1, tm, d), lambda i, j: (i, j, 0)),
        out_shape=jax.ShapeDtypeStruct((b, t, d), out_dtype),
        compiler_params=_cparams(("parallel", "parallel"), 6 * tm * d * 4),
        name="norm_mod",
    )(x, g.reshape(1, d), sh, sc)


def _norm_mod_seq_kernel(n_ctx_tiles, c_ref, x_ref, g_ref, sh_ref, sc_ref, o_ref):
    j = pl.program_id(1)
    src = jnp.where(j < n_ctx_tiles, c_ref[0], x_ref[0])
    o_ref[0] = _rms_mod(src, g_ref[...], sh_ref[0], sc_ref[0]).astype(o_ref.dtype)


def _norm_mod_seq(ctx, x, g, sh_all, sc_all, out_dtype, tm):
    b, t, d = x.shape
    l = ctx.shape[1]
    nc, nx = l // tm, t // tm
    return pl.pallas_call(
        functools.partial(_norm_mod_seq_kernel, nc),
        grid=(b, nc + nx),
        in_specs=[pl.BlockSpec((1, tm, d), lambda i, j: (i, jnp.minimum(j, nc - 1), 0)),
                  pl.BlockSpec((1, tm, d), lambda i, j: (i, jnp.maximum(j - nc, 0), 0)),
                  pl.BlockSpec((1, d), lambda i, j: (0, 0)),
                  pl.BlockSpec((1, 1, d), lambda i, j: (jnp.where(j < nc, b, i), 0, 0)),
                  pl.BlockSpec((1, 1, d), lambda i, j: (jnp.where(j < nc, b, i), 0, 0))],
        out_specs=pl.BlockSpec((1, tm, d), lambda i, j: (i, j, 0)),
        out_shape=jax.ShapeDtypeStruct((b, l + t, d), out_dtype),
        compiler_params=_cparams(("parallel", "arbitrary"), 8 * tm * d * 4),
        name="norm_mod_seq",
    )(ctx, x, g.reshape(1, d), sh_all, sc_all)


def _win_matrix(length, k):
    pos = np.arange(length)
    lo = np.clip(pos - k // 2, 0, length - 1)
    hi = np.clip(pos + (k - k // 2 - 1), 0, length - 1)
    m = np.zeros((length, length), np.float32)
    for p in range(length):
        m[p, lo[p]:hi[p] + 1] = 1.0 / float(hi[p] - lo[p] + 1)
    return m, (1.0 / (hi - lo + 1)).astype(np.float32)


def _pool_tokens_kernel(k, w, slab, vertical, inv_ref, ph_ref, h_ref, o_ref, pad_ref):
    t, tc = h_ref.shape[1], h_ref.shape[2]
    if not vertical:
        for s in range(t // slab):
            hs = h_ref[0, s * slab:(s + 1) * slab, :]
            o_ref[0, s * slab:(s + 1) * slab, :] = (
                jnp.dot(ph_ref[...], hs, precision=HIGHEST, preferred_element_type=F32) - hs)
        return
    top = (k // 2) * w
    bot = (k - k // 2 - 1) * w
    pad_ref[0:top, :] = jnp.zeros((top, tc), F32)
    if bot:
        pad_ref[top + t:top + t + bot, :] = jnp.zeros((bot, tc), F32)
    for s in range(t // slab):
        hs = h_ref[0, s * slab:(s + 1) * slab, :]
        pad_ref[top + s * slab:top + (s + 1) * slab, :] = jnp.dot(
            ph_ref[...], hs, precision=HIGHEST, preferred_element_type=F32)

    def row(r, carry):
        base = pl.multiple_of(r * w, w)
        acc = pad_ref[pl.ds(base, w), :]
        for dr in range(1, k):
            acc = acc + pad_ref[pl.ds(pl.multiple_of(base + dr * w, w), w), :]
        o_ref[0, pl.ds(base, w), :] = acc * inv_ref[r] - h_ref[0, pl.ds(base, w), :]
        return carry

    lax.fori_loop(0, t // w, row, 0)


def _pool_tokens(h, group, gdim, k, grid_w, vertical):
    b, t, d = h.shape
    tc = min(gdim, 256)
    per = gdim // tc
    if vertical:
        slab = max(grid_w, min(256, t))
        mh, _ = _win_matrix(grid_w, k)
        ph = np.kron(np.eye(slab // grid_w, dtype=np.float32), mh)
        _, inv_v = _win_matrix(t // grid_w, k)
    else:
        slab = t
        ph, _ = _win_matrix(t, k)
        inv_v = np.ones((1,), np.float32)
    pad_rows = t + (k - 1) * grid_w if vertical else SUBLANES_V7X
    return pl.pallas_call(
        functools.partial(_pool_tokens_kernel, k, grid_w, slab, vertical),
        grid=(b, per),
        in_specs=[pl.BlockSpec(memory_space=pltpu.SMEM),
                  pl.BlockSpec((slab, slab), lambda i, j: (0, 0)),
                  pl.BlockSpec((1, t, tc), lambda i, j: (i, 0, group * per + j))],
        out_specs=pl.BlockSpec((1, t, tc), lambda i, j: (i, 0, j)),
        out_shape=jax.ShapeDtypeStruct((b, t, gdim), F32),
        scratch_shapes=[pltpu.VMEM((pad_rows, tc), F32)],
        compiler_params=_cparams(("parallel", "parallel"), (4 * t + pad_rows) * tc * 4 + 2 * slab * slab * 4),
        name=f"pool_tokens_k{k}",
    )(jnp.asarray(inv_v), jnp.asarray(ph), h)


def _pool_out_kernel(n_groups, *refs):
    d_refs = refs[:n_groups]
    x_ref, w_ref, ps_ref, gt_ref, o_ref = refs[n_groups:]
    ys = [jnp.dot(d_refs[j][0].astype(BF16), w_ref[j], preferred_element_type=F32) for j in range(n_groups)]
    y = jnp.concatenate(ys, axis=-1) * ps_ref[...]
    o_ref[0] = x_ref[0] + gt_ref[0] * y


def _pool_out(ds, x, w_pool_bf16, pool_scale, gate, tm):
    b, t, d = x.shape
    ng, gdim, _ = w_pool_bf16.shape
    return pl.pallas_call(
        functools.partial(_pool_out_kernel, ng),
        grid=(b, t // tm),
        in_specs=[pl.BlockSpec((1, tm, gdim), lambda i, j: (i, j, 0)) for _ in range(ng)] + [
            pl.BlockSpec((1, tm, d), lambda i, j: (i, j, 0)),
            pl.BlockSpec((ng, gdim, gdim), lambda i, j: (0, 0, 0)),
            pl.BlockSpec((1, d), lambda i, j: (0, 0)),
            pl.BlockSpec((1, 1, d), lambda i, j: (i, 0, 0))],
        out_specs=pl.BlockSpec((1, tm, d), lambda i, j: (i, j, 0)),
        out_shape=jax.ShapeDtypeStruct((b, t, d), F32),
        compiler_params=_cparams(("parallel", "parallel"), 8 * tm * d * 4 + 2 * ng * gdim * gdim * 2),
        name="pool_out",
    )(*ds, x, w_pool_bf16, pool_scale.reshape(1, d), gate)


def _row_tile(cfg, t):
    return cfg.row_tile if t % cfg.row_tile == 0 else min(cfg.tok_tile, t)


def _pool_mixer(cfg, x, g, sh, sc, gate, w_pool_bf16, pool_scale, on_grid):
    b, t, d = x.shape
    tm = _row_tile(cfg, t)
    h = _norm_mod(x, g, sh, sc, F32, tm)
    gdim = d // len(cfg.pool_windows)
    ds = [_pool_tokens(h, j, gdim, k, cfg.grid_w, on_grid) for j, k in enumerate(cfg.pool_windows)]
    return _pool_out(ds, x, w_pool_bf16, pool_scale, gate, tm)


def _router_kernel(nx_tiles, has_ctx, *refs):
    if has_ctx:
        x_ref, c_ref, g_ref, sh_ref, sc_ref, wr_ref, br_ref, fx_ref, lg_ref = refs
        src = jnp.where(pl.program_id(0) < nx_tiles, x_ref[0], c_ref[0])
    else:
        x_ref, g_ref, sh_ref, sc_ref, wr_ref, br_ref, fx_ref, lg_ref = refs
        src = x_ref[0]
    h = _rms_mod(src, g_ref[...], sh_ref[0], sc_ref[0])
    _store_slabs(fx_ref, h)
    lg_ref[...] = jnp.dot(h, wr_ref[...], precision=HIGHEST, preferred_element_type=F32) + br_ref[...]


def _router(x, ctx, g, sh_all, sc_all, w_r, b_r, tm):
    b, t, d = x.shape
    per_b = t // tm
    nx = b * per_b
    has_ctx = ctx is not None
    nc = b * (ctx.shape[1] // tm) if has_ctx else 0
    per_c = (ctx.shape[1] // tm) if has_ctx else 1
    n_all = (nx + nc) * tm
    lanes = w_r.shape[1]

    def mod_row(i):
        return jnp.where(i < nx, jnp.minimum(i, nx - 1) // per_b, b) if has_ctx else i // per_b

    in_specs = [pl.BlockSpec((1, tm, d), lambda i: (jnp.minimum(i, nx - 1) // per_b, jnp.minimum(i, nx - 1) % per_b, 0))]
    args = [x]
    if has_ctx:
        in_specs.append(pl.BlockSpec(
            (1, tm, d), lambda i: (jnp.maximum(i - nx, 0) // per_c, jnp.maximum(i - nx, 0) % per_c, 0)))
        args.append(ctx)
    in_specs += [pl.BlockSpec((1, d), lambda i: (0, 0)),
                 pl.BlockSpec((1, 1, d), lambda i: (mod_row(i), 0, 0)),
                 pl.BlockSpec((1, 1, d), lambda i: (mod_row(i), 0, 0)),
                 pl.BlockSpec((d, lanes), lambda i: (0, 0)),
                 pl.BlockSpec((1, lanes), lambda i: (0, 0))]
    args += [g.reshape(1, d), sh_all, sc_all, w_r, b_r]
    return pl.pallas_call(
        functools.partial(_router_kernel, nx, has_ctx),
        grid=(nx + nc,),
        in_specs=in_specs,
        out_specs=[pl.BlockSpec((tm, SUBLANES_V7X, d // SUBLANES_V7X), lambda i: (i, 0, 0)),
                   pl.BlockSpec((tm, lanes), lambda i: (i, 0))],
        out_shape=[jax.ShapeDtypeStruct((n_all, SUBLANES_V7X, d // SUBLANES_V7X), F32),
                   jax.ShapeDtypeStruct((n_all, lanes), F32)],
        compiler_params=_cparams(("arbitrary",), 8 * tm * d * 4 + 2 * d * lanes * 4),
        name="moe_router",
    )(*args)


def _expert_kernel(rows, n_all, col, be_ref, nu_ref, code_ref, fx_hbm, wgu_ref, wdn_ref, out_hbm,
                   xbuf, ybuf, gsem, ssem):
    i = pl.program_id(0)
    n_used = nu_ref[0]
    slot = i % 2
    other = 1 - slot
    n_slots = n_all * 2
    last_blk = code_ref.shape[0] // rows - 1
    f = wdn_ref.shape[2]

    def gather_row(blk, sl, r):
        v = code_ref[blk * rows + r]
        src = jnp.where(v >= n_slots, 0, v >> 1)
        return pltpu.make_async_copy(fx_hbm.at[src], xbuf.at[sl, r], gsem.at[sl])

    def scatter_row(blk, sl, r, none):
        v = jnp.where(none, n_slots, code_ref[blk * rows + r])
        dst = jnp.where(v >= n_slots, n_slots + sl * rows + r, (v & 1) * n_all + (v >> 1))
        return pltpu.make_async_copy(ybuf.at[sl, r], out_hbm.at[dst], ssem.at[sl])

    def wait_gather(sl):
        pltpu.make_async_copy(fx_hbm.at[pl.ds(0, rows)], xbuf.at[sl], gsem.at[sl]).wait()

    def wait_scatter(sl):
        pltpu.make_async_copy(ybuf.at[sl], out_hbm.at[pl.ds(0, rows)], ssem.at[sl]).wait()

    @pl.when(i == 0)
    def _():
        ybuf[...] = jnp.zeros(ybuf.shape, F32)
        for sl in range(2):
            pltpu.make_async_copy(ybuf.at[sl], out_hbm.at[pl.ds(n_slots + sl * rows, rows)], ssem.at[sl]).start()
        wait_scatter(1)

        @pl.when(n_used > 0)
        def _():
            def body(r, c):
                gather_row(0, 0, r).start()
                return c
            lax.fori_loop(0, rows, body, 0)

        @pl.when(n_used == 0)
        def _():
            wait_scatter(0)

    @pl.when(i < n_used)
    def _():
        wait_gather(slot)
        nxt = jnp.minimum(i + 1, last_blk)
        prev = jnp.maximum(i - 1, 0)
        no_prev = i == 0
        xb = _load_slabs(xbuf.at[slot]).astype(BF16)
        n1 = f // col
        per1 = rows // n1
        acts = []
        for j in range(n1):
            g = jnp.dot(xb, wgu_ref[0, 0, :, j * col:(j + 1) * col].astype(BF16), preferred_element_type=F32)
            u = jnp.dot(xb, wgu_ref[0, 0, :, f + j * col:f + (j + 1) * col].astype(BF16),
                        preferred_element_type=F32)
            acts.append((_silu(g) * u).astype(BF16))
            for r in range(j * per1, (j + 1) * per1):
                scatter_row(prev, other, r, no_prev).start()
                gather_row(nxt, other, r).start()
        act = jnp.concatenate(acts, axis=-1)
        wait_scatter(slot)
        n2, sub = ybuf.shape[2], ybuf.shape[3]
        for j in range(n2):
            ybuf[slot, :, j, :] = jnp.dot(act, wdn_ref[0, 0, :, j * sub:(j + 1) * sub].astype(BF16),
                                          preferred_element_type=F32)

    @pl.when(jnp.logical_and(i == n_used, i >= 1))
    def _():
        wait_gather(slot)

        def body(r, c):
            scatter_row(i - 1, other, r, False).start()
            return c
        lax.fori_loop(0, rows, body, 0)
        wait_scatter(other)
        wait_scatter(slot)


def _experts(cfg, fx, block_e, n_used, row_code, w_gu, w_dn, layer):
    n_all, n_sub, sub = fx.shape
    d = n_sub * sub
    f2 = w_gu.shape[3]
    f = f2 // 2
    rows = cfg.moe_rows
    col = min(256, f)
    n_steps = block_e.shape[0]
    grid_spec = pltpu.PrefetchScalarGridSpec(
        num_scalar_prefetch=3,
        grid=(n_steps,),
        in_specs=[pl.BlockSpec(memory_space=pl.ANY),
                  pl.BlockSpec((1, 1, d, f2), lambda i, be, nu, cd: (layer, be[i], 0, 0),
                               pipeline_mode=pl.Buffered(1)),
                  pl.BlockSpec((1, 1, f, d), lambda i, be, nu, cd: (layer, be[i], 0, 0))],
        out_specs=pl.BlockSpec(memory_space=pl.ANY),
        scratch_shapes=[pltpu.VMEM((2, rows, n_sub, sub), F32), pltpu.VMEM((2, rows, n_sub, sub), F32),
                        pltpu.SemaphoreType.DMA((2,)), pltpu.SemaphoreType.DMA((2,))])
    vmem = (d * f2 + 2 * f * d) * 4 + 4 * rows * d * 4 + 3 * rows * f2 * 4 + 4 * d * col * 4
    return pl.pallas_call(
        functools.partial(_expert_kernel, rows, n_all, col),
        grid_spec=grid_spec,
        out_shape=jax.ShapeDtypeStruct((cfg.top_k * n_all + 2 * rows, n_sub, sub), F32),
        compiler_params=_cparams(("arbitrary",), vmem),
        name="moe_experts",
    )(block_e, n_used, row_code, fx, w_gu, w_dn)


def _combine_kernel(final, *refs):
    if final:
        x_ref, a_ref, b_ref, w_ref, gt_ref, gf_ref, o_ref = refs
    else:
        x_ref, a_ref, b_ref, w_ref, gt_ref, o_ref = refs
    w = w_ref[...]
    y = x_ref[0] + gt_ref[0] * (w[:, 0:1] * _load_slabs(a_ref) + w[:, 1:2] * _load_slabs(b_ref))
    if final:
        y = y * lax.rsqrt(jnp.mean(y * y, axis=-1, keepdims=True) + EPS) * gf_ref[...]
    o_ref[0] = y


def _combine(x, slots, wts, tok_off, n_all, gate, tm, norm_final=None):
    b, t, d = x.shape
    per_b = t // tm
    off0 = tok_off // tm
    off1 = (n_all + tok_off) // tm
    final = norm_final is not None
    in_specs = [pl.BlockSpec((1, tm, d), lambda i, j: (i, j, 0)),
                pl.BlockSpec((tm,) + slots.shape[1:], lambda i, j: (off0 + i * per_b + j, 0, 0)),
                pl.BlockSpec((tm,) + slots.shape[1:], lambda i, j: (off1 + i * per_b + j, 0, 0)),
                pl.BlockSpec((tm, wts.shape[1]), lambda i, j: (off0 + i * per_b + j, 0)),
                pl.BlockSpec((1, 1, d), lambda i, j: (i, 0, 0))]
    args = [x, slots, slots, wts, gate]
    if final:
        in_specs.append(pl.BlockSpec((1, d), lambda i, j: (0, 0)))
        args.append(norm_final.reshape(1, d))
    return pl.pallas_call(
        functools.partial(_combine_kernel, final),
        grid=(b, per_b),
        in_specs=in_specs,
        out_specs=pl.BlockSpec((1, tm, d), lambda i, j: (i, j, 0)),
        out_shape=jax.ShapeDtypeStruct((b, t, d), F32),
        compiler_params=_cparams(("parallel", "parallel"), 10 * tm * d * 4),
        name="moe_combine",
    )(*args)


def _route(cfg, logits):
    n_all = logits.shape[0]
    ng, eg, k = cfg.n_groups, cfg.experts_per_group, cfg.top_k
    assert k == 2, "row codes pack (token, choice) as 2 * token + choice"
    n_exp = ng * eg
    rows = cfg.moe_rows
    p_grp = jax.nn.softmax(logits[:, :ng], axis=-1)
    g_idx = jnp.argmax(p_grp, axis=-1)
    p_g = jnp.max(p_grp, axis=-1)
    le = logits[:, ng:ng + n_exp].reshape(n_all, ng, eg)
    le = jnp.take_along_axis(le, g_idx[:, None, None], axis=1)[:, 0]
    vals, e_in = lax.top_k(jax.nn.softmax(le, axis=-1), k)
    wts = p_g[:, None] * vals / jnp.sum(vals, axis=-1, keepdims=True)
    eid = (g_idx[:, None] * eg + e_in).astype(jnp.int32)

    s = n_all * k
    e_flat = eid.reshape(-1)
    counts = jnp.sum((e_flat[:, None] == jnp.arange(n_exp, dtype=jnp.int32)[None, :]).astype(jnp.int32), axis=0)
    order = jnp.argsort(e_flat, stable=True).astype(jnp.int32)
    starts = jnp.cumsum(counts) - counts
    padded = ((counts + rows - 1) // rows) * rows
    pends = jnp.cumsum(padded)
    pstarts = pends - padded
    n_blocks = -(-(s + n_exp * (rows - 1)) // rows)
    blk_start = jnp.arange(n_blocks + 1, dtype=jnp.int32) * rows
    block_e = jnp.minimum(jnp.sum((pends[None, :] <= blk_start[:, None]).astype(jnp.int32), axis=1), n_exp - 1)
    pos = jnp.arange(n_blocks * rows, dtype=jnp.int32)
    e_row = jnp.repeat(block_e[:n_blocks], rows)
    rank = pos - pstarts[e_row]
    routed = jnp.logical_and(rank < counts[e_row], pos < pends[-1])
    row_code = jnp.where(routed, order[jnp.clip(starts[e_row] + rank, 0, s - 1)], s).astype(jnp.int32)
    n_used = (pends[-1] // rows).astype(jnp.int32).reshape(1)
    return (block_e, n_used, row_code), wts


def _moe(cfg, x, ctx, g, sh_all, sc_all, w_r, b_r, w_gu, w_dn, layer):
    b, t, d = x.shape
    tm = _row_tile(cfg, t) if ctx is None else min(cfg.tok_tile, t, ctx.shape[1])
    fx, logits = _router(x, ctx, g, sh_all, sc_all, w_r, b_r, tm)
    n_all = fx.shape[0]
    plan, wts = _route(cfg, logits)
    slots = _experts(cfg, fx, *plan, w_gu, w_dn, layer)
    return slots, wts, n_all


def _cast_weight_once(w_ref, wbf_ref):
    @pl.when(jnp.logical_and(pl.program_id(0) == 0, pl.program_id(1) == 0))
    def _():
        k = wbf_ref.shape[0]
        step = min(k, 256)

        def body(c, carry):
            r0 = pl.multiple_of(c * step, step)
            wbf_ref[pl.ds(r0, step), :] = w_ref[0, pl.ds(r0, step), :].astype(BF16)
            return carry
        lax.fori_loop(0, k // step, body, 0)


def _proj_kernel(kind, n_heads, hd, *refs):
    if kind == "gate":
        h_ref, w_ref, la_ref, lc_ref, o_ref, wbf_ref = refs
    else:
        h_ref, w_ref, o_ref, wbf_ref = refs
    _cast_weight_once(w_ref, wbf_ref)
    z = jnp.dot(h_ref[0], wbf_ref[...], preferred_element_type=F32)
    if kind == "silu":
        z = _silu(z)
    elif kind == "gate":
        ls = jnp.minimum(z, 0.0) - jnp.log(1.0 + jnp.exp(-jnp.abs(z)))
        c = lc_ref[...] + ls
        a = jnp.broadcast_to(la_ref[...], c.shape)
        m = jnp.maximum(a, c)
        z = m + jnp.log(1.0 + jnp.exp(-jnp.abs(a - c)))
    for h in range(n_heads):
        o_ref[0, h] = z[:, h * hd:(h + 1) * hd].astype(o_ref.dtype)


def _proj(cfg, h_all, w_all, layer, section, kind, out_dtype, log_lb=None, log1m_lb=None):
    b, s, d = h_all.shape
    n = d
    hd = cfg.head_dim
    nh = n // hd
    tm = cfg.tok_tile
    in_specs = [pl.BlockSpec((1, tm, d), lambda i, j: (i, j, 0)),
                pl.BlockSpec((1, d, n), lambda i, j: (layer, 0, section), pipeline_mode=pl.Buffered(1))]
    args = [h_all, w_all]
    if kind == "gate":
        in_specs += [pl.BlockSpec((1, n), lambda i, j: (0, 0)), pl.BlockSpec((1, n), lambda i, j: (0, 0))]
        args += [log_lb.reshape(1, n), log1m_lb.reshape(1, n)]
    return pl.pallas_call(
        functools.partial(_proj_kernel, kind, nh, hd),
        grid=(b, s // tm),
        in_specs=in_specs,
        out_specs=pl.BlockSpec((1, nh, tm, hd), lambda i, j: (i, 0, j, 0)),
        out_shape=jax.ShapeDtypeStruct((b, nh, s, hd), out_dtype),
        scratch_shapes=[pltpu.VMEM((d, n), BF16)],
        compiler_params=_cparams(("arbitrary", "arbitrary"), d * n * 6 + 2 * tm * d * 2 + 6 * tm * n * 4),
        name=f"hgrn_proj_{kind}",
    )(*args)


def _gla_kernel(chunk, reverse, readout, limit, *refs):
    if readout:
        q_ref, lf_ref, v_ref, of_ref, gs_ref, ng_ref, o_ref, st_ref, b_ref, q32_ref, at_ref = refs
    else:
        q_ref, lf_ref, v_ref, o_ref, st_ref, b_ref, q32_ref, at_ref = refs
    hg, tb, hd = q_ref.shape[1], q_ref.shape[2], q_ref.shape[3]
    n_chunks = tb // chunk
    shift = chunk.bit_length() - 1

    @pl.when(pl.program_id(2) == 0)
    def _():
        st_ref[...] = jnp.zeros(st_ref.shape, F32)

    rr = lax.broadcasted_iota(jnp.int32, (tb, tb), 0)
    cc = lax.broadcasted_iota(jnp.int32, (tb, tb), 1)
    same = (rr >> shift) == (cc >> shift)
    tri = jnp.where(jnp.logical_and(same, (cc >= rr) if reverse else (cc <= rr)), 1.0, 0.0).astype(BF16)
    ar = lax.broadcasted_iota(jnp.int32, (chunk, chunk), 0)
    ac = lax.broadcasted_iota(jnp.int32, (chunk, chunk), 1)
    at_mask = (ar >= ac) if reverse else (ar <= ac)
    row_id = lax.broadcasted_iota(jnp.int32, (chunk, hd), 0)
    order = list(range(n_chunks))[::-1] if reverse else list(range(n_chunks))
    edge = 0 if reverse else chunk - 1
    grp = b_ref.shape[0]

    def group(hp, carry):
        heads = [hp * grp + u for u in range(grp)]
        lf_all = jnp.concatenate([lf_ref[0, h] for h in heads], axis=-1)
        hi = lf_all.astype(BF16)
        r1 = lf_all - hi.astype(F32)
        mid = r1.astype(BF16)
        lo = (r1 - mid.astype(F32)).astype(BF16)
        b_all = (jnp.dot(tri, hi, preferred_element_type=F32) + jnp.dot(tri, mid, preferred_element_type=F32)
                 + jnp.dot(tri, lo, preferred_element_type=F32))
        bs, kks, qfs, btots = [], [], [], []
        worst = None
        for u, h in enumerate(heads):
            b = b_all[:, u * hd:(u + 1) * hd]
            b_ref[u] = b
            btot = [b[c * chunk + edge:c * chunk + edge + 1, :] for c in range(n_chunks)]
            for bt in btot:
                worst = bt if worst is None else jnp.minimum(worst, bt)
            bs.append(b)
            btots.append(btot)
            kks.append(1.0 - jnp.exp(lf_all[:, u * hd:(u + 1) * hd]))
            qfs.append(q_ref[0, h].astype(F32))
        fast = jnp.min(worst) >= -limit

        @pl.when(fast)
        def _():
            for u in range(grp):
                for c in range(n_chunks):
                    sl = slice(c * chunk, (c + 1) * chunk)
                    d = bs[u][sl] - 0.5 * btots[u][c]
                    qh = (qfs[u][sl] * jnp.exp(d)).astype(BF16)
                    kh = (kks[u][sl] * jnp.exp(-d)).astype(BF16)
                    at = lax.dot_general(kh, qh, (((1,), (1,)), ((), ())), preferred_element_type=F32)
                    at_ref[u, c] = jnp.where(at_mask, at, 0.0)

        @pl.when(jnp.logical_not(fast))
        def _():
            for u in range(grp):
                q32_ref[u] = qfs[u]
                for c in range(n_chunks):
                    c0 = c * chunk
                    bc = bs[u][c0:c0 + chunk]
                    kc = kks[u][c0:c0 + chunk]

                    def tstep(t, at):
                        bt = b_ref[u, pl.ds(c0 + t, 1), :]
                        qrow = q32_ref[u, pl.ds(c0 + t, 1), :]
                        valid = (row_id >= t) if reverse else (row_id <= t)
                        e = jnp.exp(jnp.where(valid, bt - bc, -jnp.inf))
                        col = jnp.sum(e * kc * qrow, axis=-1, keepdims=True)
                        return at + col * jnp.where(ac == t, 1.0, 0.0)

                    at_ref[u, c] = lax.fori_loop(0, chunk, tstep, jnp.zeros((chunk, chunk), F32))

        for u, h in enumerate(heads):
            b, kk, btot = bs[u], kks[u], btots[u]
            qb = (qfs[u] * jnp.exp(b)).astype(BF16)
            vb = v_ref[0, h]
            st = st_ref[h]
            for c in order:
                sl = slice(c * chunk, (c + 1) * chunk)
                k2 = (kk[sl] * jnp.exp(btot[c] - b[sl])).astype(BF16)
                o = lax.dot_general(at_ref[u, c].astype(BF16), vb[sl], (((0,), (0,)), ((), ())),
                                    preferred_element_type=F32)
                o = o + lax.dot_general(qb[sl], st.astype(BF16), (((1,), (1,)), ((), ())),
                                        preferred_element_type=F32)
                st = st * jnp.exp(btot[c]) + lax.dot_general(vb[sl], k2, (((0,), (0,)), ((), ())),
                                                             preferred_element_type=F32)
                if readout:
                    o = o + of_ref[0, h, sl, :]
                    o = o * lax.rsqrt(jnp.mean(o * o, axis=-1, keepdims=True) + EPS) * ng_ref[...]
                    o_ref[0, h, sl, :] = (o * gs_ref[0, h, sl, :].astype(F32)).astype(o_ref.dtype)
                else:
                    o_ref[0, h, sl, :] = o
            st_ref[h] = st
        return carry

    lax.fori_loop(0, hg // grp, group, 0)


def _gla(cfg, qs, lf, v, reverse, n_ctx_blocks, of=None, gs=None, norm_g=None):
    b, nh, s, hd = qs.shape
    tb = cfg.tok_tile
    hg = nh
    grp = cfg.gla_group if hg % cfg.gla_group == 0 else 1
    nblk = s // tb
    readout = of is not None

    if reverse:
        def blk(j):
            return jnp.where(j < n_ctx_blocks, n_ctx_blocks - 1 - j, nblk - 1 - (j - n_ctx_blocks))
    else:
        def blk(j):
            return j

    spec = pl.BlockSpec((1, hg, tb, hd), lambda i, g, j: (i, g, blk(j), 0))
    in_specs = [spec, spec, spec]
    args = [qs, lf, v]
    if readout:
        in_specs += [spec, spec, pl.BlockSpec((1, hd), lambda i, g, j: (0, 0))]
        args += [of, gs, norm_g.reshape(1, hd)]
    out_dtype = BF16 if readout else F32
    vmem = 2 * hg * tb * hd * (2 + 4 + 2 + 4 + 2 + 4) + hg * hd * hd * 4 + 8 * tb * tb * 4 + 16 * tb * hd * 4
    return pl.pallas_call(
        functools.partial(_gla_kernel, cfg.gla_chunk, reverse, readout, cfg.fast_decay_limit),
        grid=(b, nh // hg, nblk),
        in_specs=in_specs,
        out_specs=spec,
        out_shape=jax.ShapeDtypeStruct((b, nh, s, hd), out_dtype),
        scratch_shapes=[pltpu.VMEM((hg, hd, hd), F32), pltpu.VMEM((grp, tb, hd), F32),
                        pltpu.VMEM((grp, tb, hd), F32),
                        pltpu.VMEM((grp, tb // cfg.gla_chunk, cfg.gla_chunk, cfg.gla_chunk), F32)],
        compiler_params=_cparams(("parallel", "parallel", "arbitrary"), vmem),
        name="hgrn_scan_bwd" if reverse else "hgrn_scan_fwd",
    )(*args)


def _out_proj_kernel(n_heads, y_ref, x_ref, w_ref, gt_ref, o_ref, wbf_ref):
    _cast_weight_once(w_ref, wbf_ref)
    y = jnp.concatenate([y_ref[0, h] for h in range(n_heads)], axis=-1)
    o_ref[0] = x_ref[0] + gt_ref[0] * jnp.dot(y, wbf_ref[...], preferred_element_type=F32)


def _out_proj(cfg, y_heads, x, w_out_all, layer, gate, ctx_tiles):
    b, t, d = x.shape
    nh, hd = y_heads.shape[1], y_heads.shape[3]
    tm = cfg.tok_tile
    return pl.pallas_call(
        functools.partial(_out_proj_kernel, nh),
        grid=(b, t // tm),
        in_specs=[pl.BlockSpec((1, nh, tm, hd), lambda i, j: (i, 0, j + ctx_tiles, 0)),
                  pl.BlockSpec((1, tm, d), lambda i, j: (i, j, 0)),
                  pl.BlockSpec((1, d, d), lambda i, j: (layer, 0, 0), pipeline_mode=pl.Buffered(1)),
                  pl.BlockSpec((1, 1, d), lambda i, j: (i, 0, 0))],
        out_specs=pl.BlockSpec((1, tm, d), lambda i, j: (i, j, 0)),
        out_shape=jax.ShapeDtypeStruct((b, t, d), F32),
        scratch_shapes=[pltpu.VMEM((d, d), BF16)],
        compiler_params=_cparams(("arbitrary", "arbitrary"), d * d * 6 + 8 * tm * d * 4),
        name="hgrn_out_proj",
    )(y_heads, x, w_out_all, gate)


def _hgrn_mixer(cfg, x, ctx, g, sh_all, sc_all, gate, w_in_all, norm_g, w_out_all, layer, lb):
    b, t, d = x.shape
    tm = cfg.tok_tile
    h_all = _norm_mod_seq(ctx, x, g, sh_all, sc_all, BF16, tm)
    log_lb = jnp.log(lb)
    log1m_lb = jnp.log1p(-lb)
    qs = _proj(cfg, h_all, w_in_all, layer, 0, "silu", BF16)
    gs = _proj(cfg, h_all, w_in_all, layer, 1, "silu", BF16)
    lf = _proj(cfg, h_all, w_in_all, layer, 2, "gate", F32, log_lb[0], log1m_lb[0])
    lr = _proj(cfg, h_all, w_in_all, layer, 3, "gate", F32, log_lb[1], log1m_lb[1])
    v = _proj(cfg, h_all, w_in_all, layer, 4, "none", BF16)
    n_ctx_blocks = ctx.shape[1] // tm
    of = _gla(cfg, qs, lf, v, False, n_ctx_blocks)
    y = _gla(cfg, qs, lr, v, True, n_ctx_blocks, of, gs, norm_g)
    return _out_proj(cfg, y, x, w_out_all, layer, gate, n_ctx_blocks)


def _forward(cfg, x, c, ctx, c_ctx, norm_mix, norm_ffn, w_ada, b_ada, pool_w, pool_scale, hgrn_w_in, hgrn_norm,
             hgrn_w_out, hgrn_lb_logits, router_w_group, router_b_group, router_w_expert, router_b_expert,
             moe_w_gate_up, moe_w_down, norm_final):
    b, t, d = x.shape
    depth = w_ada.shape[0]
    n_mixers = 2
    p_lb = jax.nn.softmax(hgrn_lb_logits.astype(F32), axis=0)
    lb_all = jnp.cumsum(p_lb, axis=0) - p_lb[0]

    rows = -(-(b + 1) // SUBLANES_V7X) * SUBLANES_V7X
    cond = jnp.zeros((rows, d), F32).at[:b].set(c).at[b].set(c_ctx)
    mods = _adaln(cond, w_ada, b_ada)

    ng, ne = cfg.n_groups, cfg.n_groups * cfg.experts_per_group
    for i in range(depth):
        ctx_needed = i < depth - 1
        j = i // n_mixers
        m = [mods[i, :, q * d:(q + 1) * d].reshape(rows, 1, d) for q in range(6)]
        sh_m, sc_m, gt_m, sh_f, sc_f, gt_f = m
        ctx_rows = lambda a: jnp.broadcast_to(a[b:b + 1], (b, 1, d))
        if i % n_mixers == 0:
            wp = pool_w[j].astype(BF16)
            x_new = _pool_mixer(cfg, x, norm_mix[i], sh_m, sc_m, gt_m, wp, pool_scale[j], True)
            if ctx_needed:
                ctx = _pool_mixer(cfg, ctx, norm_mix[i], ctx_rows(sh_m), ctx_rows(sc_m), ctx_rows(gt_m), wp,
                                  pool_scale[j], False)
            x = x_new
        else:
            x = _hgrn_mixer(cfg, x, ctx, norm_mix[i], sh_m, sc_m, gt_m, hgrn_w_in, hgrn_norm[j], hgrn_w_out, j,
                            lb_all[i])
        lanes = -(-(ng + ne) // LANES_V7X) * LANES_V7X
        w_r = jnp.zeros((d, lanes), F32).at[:, :ng].set(router_w_group[i]).at[:, ng:ng + ne].set(router_w_expert[i])
        b_r = jnp.zeros((1, lanes), F32).at[0, :ng].set(router_b_group[i]).at[0, ng:ng + ne].set(router_b_expert[i])
        slots, wts, n_all = _moe(cfg, x, ctx if ctx_needed else None, norm_ffn[i], sh_f, sc_f, w_r, b_r,
                                 moe_w_gate_up, moe_w_down, i)
        tm = _row_tile(cfg, t)
        last = i == depth - 1
        x_next = _combine(x, slots, wts, 0, n_all, gt_f, tm, norm_final if last else None)
        if ctx_needed:
            ctx = _combine(ctx, slots, wts, b * t, n_all, ctx_rows(gt_f), min(cfg.tok_tile, ctx.shape[1]))
        x = x_next
    return x


def kernel(x, c, ctx, c_ctx, norm_mix, norm_ffn, w_ada, b_ada, pool_w, pool_scale, hgrn_w_in, hgrn_norm, hgrn_w_out, hgrn_lb_logits, router_w_group, router_b_group, router_w_expert, router_b_expert, moe_w_gate_up, moe_w_down, norm_final):
    return _forward(Cfg(), x, c, ctx, c_ctx, norm_mix, norm_ffn, w_ada, b_ada, pool_w, pool_scale, hgrn_w_in,
                    hgrn_norm, hgrn_w_out, hgrn_lb_logits, router_w_group, router_b_group, router_w_expert,
                    router_b_expert, moe_w_gate_up, moe_w_down, norm_final)
```

```python
import functools
from typing import NamedTuple

import numpy as np
import jax
import jax.numpy as jnp
from jax import lax
from jax.experimental import pallas as pl
from jax.experimental.pallas import tpu as pltpu

F32 = jnp.float32
BF16 = jnp.bfloat16
HIGHEST = lax.Precision.HIGHEST
EPS = 1e-6

LANES_V7X = 128
SUBLANES_V7X = 8
VMEM_BYTES_V7X = 64 * 1024 * 1024


class Cfg(NamedTuple):
    grid_w: int = 64
    pool_windows: tuple = (2, 4, 8, 16)
    head_dim: int = LANES_V7X
    n_groups: int = 4
    experts_per_group: int = 8
    top_k: int = 2
    moe_rows: int = 512
    gla_chunk: int = 64
    gla_group: int = 8
    tok_tile: int = 256
    row_tile: int = 512
    fast_decay_limit: float = 120.0


def _vmem_limit(nbytes):
    return int(min(max(nbytes * 5 // 4, 16 * 1024 * 1024), VMEM_BYTES_V7X - 6 * 1024 * 1024))


def _cparams(sem, vmem_bytes):
    return pltpu.CompilerParams(dimension_semantics=sem, vmem_limit_bytes=_vmem_limit(vmem_bytes))


def _rms_mod(x, g, shift, scale):
    y = x * lax.rsqrt(jnp.mean(x * x, axis=-1, keepdims=True) + EPS) * g
    return y * (1.0 + scale) + shift


def _silu(x):
    return x * jax.nn.sigmoid(x)


def _store_slabs(ref, val):
    rows, d = val.shape
    n_sub = d // LANES_V7X
    for s in range(n_sub):
        ref[pl.ds(s, rows, stride=n_sub), :] = val[:, s * LANES_V7X:(s + 1) * LANES_V7X]


def _load_slabs(ref, rows):
    n_sub = ref.shape[0] // rows
    return jnp.concatenate([ref[pl.ds(s, rows, stride=n_sub), :] for s in range(n_sub)], axis=-1)


def _adaln_kernel(cond_ref, w_ref, b_ref, o_ref):
    s = _silu(cond_ref[...])
    o_ref[0] = jnp.dot(s, w_ref[0], precision=HIGHEST, preferred_element_type=F32) + b_ref[0]


def _adaln(cond, w_ada, b_ada):
    depth, d, n = w_ada.shape
    rows = cond.shape[0]
    tn = 1024 if n % 1024 == 0 else n
    return pl.pallas_call(
        _adaln_kernel,
        grid=(depth, n // tn),
        in_specs=[pl.BlockSpec((rows, d), lambda l, j: (0, 0)),
                  pl.BlockSpec((1, d, tn), lambda l, j: (l, 0, j)),
                  pl.BlockSpec((1, 1, tn), lambda l, j: (l, 0, j))],
        out_specs=pl.BlockSpec((1, rows, tn), lambda l, j: (l, 0, j)),
        out_shape=jax.ShapeDtypeStruct((depth, rows, n), F32),
        compiler_params=_cparams(("arbitrary", "arbitrary"), 2 * d * tn * 4 + 4 * rows * (d + tn) * 4),
        name="adaln",
    )(cond, w_ada, b_ada.reshape(depth, 1, n))


def _norm_mod_kernel(x_ref, g_ref, sh_ref, sc_ref, o_ref):
    o_ref[0] = _rms_mod(x_ref[0], g_ref[...], sh_ref[0], sc_ref[0]).astype(o_ref.dtype)


def _norm_mod(x, g, sh, sc, out_dtype, tm):
    b, t, d = x.shape
    return pl.pallas_call(
        _norm_mod_kernel,
        grid=(b, t // tm),
        in_specs=[pl.BlockSpec((1, tm, d), lambda i, j: (i, j, 0)),
                  pl.BlockSpec((1, d), lambda i, j: (0, 0)),
                  pl.BlockSpec((1, 1, d), lambda i, j: (i, 0, 0)),
                  pl.BlockSpec((1, 1, d), lambda i, j: (i, 0, 0))],
        out_specs=pl.BlockSpec((1, tm, d), lambda i, j: (i, j, 0)),
        out_shape=jax.ShapeDtypeStruct((b, t, d), out_dtype),
        compiler_params=_cparams(("parallel", "parallel"), 6 * tm * d * 4),
        name="norm_mod",
    )(x, g.reshape(1, d), sh, sc)


def _norm_mod_seq_kernel(n_ctx_tiles, c_ref, x_ref, g_ref, sh_ref, sc_ref, o_ref):
    j = pl.program_id(1)
    src = jnp.where(j < n_ctx_tiles, c_ref[0], x_ref[0])
    o_ref[0] = _rms_mod(src, g_ref[...], sh_ref[0], sc_ref[0]).astype(o_ref.dtype)


def _norm_mod_seq(ctx, x, g, sh_all, sc_all, out_dtype, tm):
    b, t, d = x.shape
    l = ctx.shape[1]
    nc, nx = l // tm, t // tm
    return pl.pallas_call(
        functools.partial(_norm_mod_seq_kernel, nc),
        grid=(b, nc + nx),
        in_specs=[pl.BlockSpec((1, tm, d), lambda i, j: (i, jnp.minimum(j, nc - 1), 0)),
                  pl.BlockSpec((1, tm, d), lambda i, j: (i, jnp.maximum(j - nc, 0), 0)),
                  pl.BlockSpec((1, d), lambda i, j: (0, 0)),
                  pl.BlockSpec((1, 1, d), lambda i, j: (jnp.where(j < nc, b, i), 0, 0)),
                  pl.BlockSpec((1, 1, d), lambda i, j: (jnp.where(j < nc, b, i), 0, 0))],
        out_specs=pl.BlockSpec((1, tm, d), lambda i, j: (i, j, 0)),
        out_shape=jax.ShapeDtypeStruct((b, l + t, d), out_dtype),
        compiler_params=_cparams(("parallel", "arbitrary"), 8 * tm * d * 4),
        name="norm_mod_seq",
    )(ctx, x, g.reshape(1, d), sh_all, sc_all)


def _win_matrix(length, k):
    pos = np.arange(length)
    lo = np.clip(pos - k // 2, 0, length - 1)
    hi = np.clip(pos + (k - k // 2 - 1), 0, length - 1)
    m = np.zeros((length, length), np.float32)
    for p in range(length):
        m[p, lo[p]:hi[p] + 1] = 1.0 / float(hi[p] - lo[p] + 1)
    return m, (1.0 / (hi - lo + 1)).astype(np.float32)


def _pool_tokens_kernel(k, w, slab, vertical, inv_ref, ph_ref, h_ref, o_ref, pad_ref):
    t, tc = h_ref.shape[1], h_ref.shape[2]
    if not vertical:
        for s in range(t // slab):
            hs = h_ref[0, s * slab:(s + 1) * slab, :]
            o_ref[0, s * slab:(s + 1) * slab, :] = (
                jnp.dot(ph_ref[...], hs, precision=HIGHEST, preferred_element_type=F32) - hs)
        return
    top = (k // 2) * w
    bot = (k - k // 2 - 1) * w
    pad_ref[0:top, :] = jnp.zeros((top, tc), F32)
    if bot:
        pad_ref[top + t:top + t + bot, :] = jnp.zeros((bot, tc), F32)
    for s in range(t // slab):
        hs = h_ref[0, s * slab:(s + 1) * slab, :]
        pad_ref[top + s * slab:top + (s + 1) * slab, :] = jnp.dot(
            ph_ref[...], hs, precision=HIGHEST, preferred_element_type=F32)

    def row(r, carry):
        base = pl.multiple_of(r * w, w)
        acc = pad_ref[pl.ds(base, w), :]
        for dr in range(1, k):
            acc = acc + pad_ref[pl.ds(pl.multiple_of(base + dr * w, w), w), :]
        o_ref[0, pl.ds(base, w), :] = acc * inv_ref[r] - h_ref[0, pl.ds(base, w), :]
        return carry

    lax.fori_loop(0, t // w, row, 0)


def _pool_tokens(h, group, gdim, k, grid_w, vertical):
    b, t, d = h.shape
    tc = min(gdim, 256)
    per = gdim // tc
    if vertical:
        slab = max(grid_w, min(256, t))
        mh, _ = _win_matrix(grid_w, k)
        ph = np.kron(np.eye(slab // grid_w, dtype=np.float32), mh)
        _, inv_v = _win_matrix(t // grid_w, k)
    else:
        slab = t
        ph, _ = _win_matrix(t, k)
        inv_v = np.ones((1,), np.float32)
    pad_rows = t + (k - 1) * grid_w if vertical else SUBLANES_V7X
    return pl.pallas_call(
        functools.partial(_pool_tokens_kernel, k, grid_w, slab, vertical),
        grid=(b, per),
        in_specs=[pl.BlockSpec(memory_space=pltpu.SMEM),
                  pl.BlockSpec((slab, slab), lambda i, j: (0, 0)),
                  pl.BlockSpec((1, t, tc), lambda i, j: (i, 0, group * per + j))],
        out_specs=pl.BlockSpec((1, t, tc), lambda i, j: (i, 0, j)),
        out_shape=jax.ShapeDtypeStruct((b, t, gdim), F32),
        scratch_shapes=[pltpu.VMEM((pad_rows, tc), F32)],
        compiler_params=_cparams(("parallel", "parallel"), (4 * t + pad_rows) * tc * 4 + 2 * slab * slab * 4),
        name=f"pool_tokens_k{k}",
    )(jnp.asarray(inv_v), jnp.asarray(ph), h)


def _pool_out_kernel(n_groups, *refs):
    d_refs = refs[:n_groups]
    x_ref, w_ref, ps_ref, gt_ref, o_ref = refs[n_groups:]
    ys = [jnp.dot(d_refs[j][0].astype(BF16), w_ref[j], preferred_element_type=F32) for j in range(n_groups)]
    y = jnp.concatenate(ys, axis=-1) * ps_ref[...]
    o_ref[0] = x_ref[0] + gt_ref[0] * y


def _pool_out(ds, x, w_pool_bf16, pool_scale, gate, tm):
    b, t, d = x.shape
    ng, gdim, _ = w_pool_bf16.shape
    return pl.pallas_call(
        functools.partial(_pool_out_kernel, ng),
        grid=(b, t // tm),
        in_specs=[pl.BlockSpec((1, tm, gdim), lambda i, j: (i, j, 0)) for _ in range(ng)] + [
            pl.BlockSpec((1, tm, d), lambda i, j: (i, j, 0)),
            pl.BlockSpec((ng, gdim, gdim), lambda i, j: (0, 0, 0)),
            pl.BlockSpec((1, d), lambda i, j: (0, 0)),
            pl.BlockSpec((1, 1, d), lambda i, j: (i, 0, 0))],
        out_specs=pl.BlockSpec((1, tm, d), lambda i, j: (i, j, 0)),
        out_shape=jax.ShapeDtypeStruct((b, t, d), F32),
        compiler_params=_cparams(("parallel", "parallel"), 8 * tm * d * 4 + 2 * ng * gdim * gdim * 2),
        name="pool_out",
    )(*ds, x, w_pool_bf16, pool_scale.reshape(1, d), gate)


def _row_tile(cfg, t):
    return cfg.row_tile if t % cfg.row_tile == 0 else min(cfg.tok_tile, t)


def _pool_mixer(cfg, x, g, sh, sc, gate, w_pool_bf16, pool_scale, on_grid):
    b, t, d = x.shape
    tm = _row_tile(cfg, t)
    h = _norm_mod(x, g, sh, sc, F32, tm)
    gdim = d // len(cfg.pool_windows)
    ds = [_pool_tokens(h, j, gdim, k, cfg.grid_w, on_grid) for j, k in enumerate(cfg.pool_windows)]
    return _pool_out(ds, x, w_pool_bf16, pool_scale, gate, tm)


def _router_kernel(nx_tiles, has_ctx, *refs):
    if has_ctx:
        x_ref, c_ref, g_ref, sh_ref, sc_ref, wr_ref, br_ref, fx_ref, lg_ref = refs
        src = jnp.where(pl.program_id(0) < nx_tiles, x_ref[0], c_ref[0])
    else:
        x_ref, g_ref, sh_ref, sc_ref, wr_ref, br_ref, fx_ref, lg_ref = refs
        src = x_ref[0]
    h = _rms_mod(src, g_ref[...], sh_ref[0], sc_ref[0])
    _store_slabs(fx_ref, h)
    lg_ref[...] = jnp.dot(h, wr_ref[...], precision=HIGHEST, preferred_element_type=F32) + br_ref[...]


def _router(x, ctx, g, sh_all, sc_all, w_r, b_r, tm):
    b, t, d = x.shape
    per_b = t // tm
    nx = b * per_b
    has_ctx = ctx is not None
    nc = b * (ctx.shape[1] // tm) if has_ctx else 0
    per_c = (ctx.shape[1] // tm) if has_ctx else 1
    n_all = (nx + nc) * tm
    lanes = w_r.shape[1]

    def mod_row(i):
        return jnp.where(i < nx, jnp.minimum(i, nx - 1) // per_b, b) if has_ctx else i // per_b

    in_specs = [pl.BlockSpec((1, tm, d), lambda i: (jnp.minimum(i, nx - 1) // per_b, jnp.minimum(i, nx - 1) % per_b, 0))]
    args = [x]
    if has_ctx:
        in_specs.append(pl.BlockSpec(
            (1, tm, d), lambda i: (jnp.maximum(i - nx, 0) // per_c, jnp.maximum(i - nx, 0) % per_c, 0)))
        args.append(ctx)
    in_specs += [pl.BlockSpec((1, d), lambda i: (0, 0)),
                 pl.BlockSpec((1, 1, d), lambda i: (mod_row(i), 0, 0)),
                 pl.BlockSpec((1, 1, d), lambda i: (mod_row(i), 0, 0)),
                 pl.BlockSpec((d, lanes), lambda i: (0, 0)),
                 pl.BlockSpec((1, lanes), lambda i: (0, 0))]
    args += [g.reshape(1, d), sh_all, sc_all, w_r, b_r]
    return pl.pallas_call(
        functools.partial(_router_kernel, nx, has_ctx),
        grid=(nx + nc,),
        in_specs=in_specs,
        out_specs=[pl.BlockSpec((tm * (d // LANES_V7X), LANES_V7X), lambda i: (i, 0)),
                   pl.BlockSpec((tm, lanes), lambda i: (i, 0))],
        out_shape=[jax.ShapeDtypeStruct((n_all * (d // LANES_V7X), LANES_V7X), F32),
                   jax.ShapeDtypeStruct((n_all, lanes), F32)],
        compiler_params=_cparams(("arbitrary",), 8 * tm * d * 4 + 2 * d * lanes * 4),
        name="moe_router",
    )(*args)


def _expert_kernel(rows, n_all, col, be_ref, nu_ref, code_ref, fx_hbm, wgu_ref, wdn_ref, out_hbm,
                   xbuf, ybuf, gsem, ssem):
    i = pl.program_id(0)
    n_used = nu_ref[0]
    slot = i % 2
    other = 1 - slot
    n_slots = n_all * 2
    last_blk = code_ref.shape[0] // rows - 1
    f, d = wdn_ref.shape[2], wdn_ref.shape[3]
    n_sub = d // LANES_V7X

    def slab(ref, row, count=1):
        return ref.at[pl.ds(pl.multiple_of(row * n_sub, n_sub), count * n_sub), :]

    def gather_row(blk, sl, r):
        v = code_ref[blk * rows + r]
        src = jnp.where(v >= n_slots, 0, v >> 1)
        return pltpu.make_async_copy(slab(fx_hbm, src), slab(xbuf.at[sl], r), gsem.at[sl])

    def scatter_row(blk, sl, r, none):
        v = jnp.where(none, n_slots, code_ref[blk * rows + r])
        dst = jnp.where(v >= n_slots, n_slots + sl * rows + r, (v & 1) * n_all + (v >> 1))
        return pltpu.make_async_copy(slab(ybuf.at[sl], r), slab(out_hbm, dst), ssem.at[sl])

    def wait_gather(sl):
        pltpu.make_async_copy(slab(fx_hbm, 0, rows), xbuf.at[sl], gsem.at[sl]).wait()

    def wait_scatter(sl):
        pltpu.make_async_copy(ybuf.at[sl], slab(out_hbm, 0, rows), ssem.at[sl]).wait()

    @pl.when(i == 0)
    def _():
        ybuf[...] = jnp.zeros(ybuf.shape, F32)
        for sl in range(2):
            pltpu.make_async_copy(ybuf.at[sl], slab(out_hbm, n_slots + sl * rows, rows), ssem.at[sl]).start()
        wait_scatter(1)

        @pl.when(n_used > 0)
        def _():
            def body(r, c):
                gather_row(0, 0, r).start()
                return c
            lax.fori_loop(0, rows, body, 0)

        @pl.when(n_used == 0)
        def _():
            wait_scatter(0)

    @pl.when(i < n_used)
    def _():
        wait_gather(slot)
        nxt = jnp.minimum(i + 1, last_blk)
        prev = jnp.maximum(i - 1, 0)
        no_prev = i == 0
        xb = _load_slabs(xbuf.at[slot], rows).astype(BF16)
        n1 = f // col
        per1 = rows // n1
        acts = []
        for j in range(n1):
            g = jnp.dot(xb, wgu_ref[0, 0, :, j * col:(j + 1) * col].astype(BF16), preferred_element_type=F32)
            u = jnp.dot(xb, wgu_ref[0, 0, :, f + j * col:f + (j + 1) * col].astype(BF16),
                        preferred_element_type=F32)
            acts.append((_silu(g) * u).astype(BF16))
            for r in range(j * per1, (j + 1) * per1):
                scatter_row(prev, other, r, no_prev).start()
                gather_row(nxt, other, r).start()
        act = jnp.concatenate(acts, axis=-1)
        wait_scatter(slot)
        yslot = ybuf.at[slot]
        per_col = col // LANES_V7X
        for j in range(d // col):
            y = jnp.dot(act, wdn_ref[0, 0, :, j * col:(j + 1) * col].astype(BF16), preferred_element_type=F32)
            for q in range(per_col):
                yslot[pl.ds(j * per_col + q, rows, stride=n_sub), :] = y[:, q * LANES_V7X:(q + 1) * LANES_V7X]

    @pl.when(jnp.logical_and(i == n_used, i >= 1))
    def _():
        wait_gather(slot)

        def body(r, c):
            scatter_row(i - 1, other, r, False).start()
            return c
        lax.fori_loop(0, rows, body, 0)
        wait_scatter(other)
        wait_scatter(slot)


def _experts(cfg, fx, block_e, n_used, row_code, w_gu, w_dn, layer):
    d, f2 = w_gu.shape[2], w_gu.shape[3]
    n_sub = d // LANES_V7X
    n_all = fx.shape[0] // n_sub
    f = f2 // 2
    rows = cfg.moe_rows
    col = min(256, f, d)
    n_steps = block_e.shape[0]
    grid_spec = pltpu.PrefetchScalarGridSpec(
        num_scalar_prefetch=3,
        grid=(n_steps,),
        in_specs=[pl.BlockSpec(memory_space=pl.ANY),
                  pl.BlockSpec((1, 1, d, f2), lambda i, be, nu, cd: (layer, be[i], 0, 0),
                               pipeline_mode=pl.Buffered(1)),
                  pl.BlockSpec((1, 1, f, d), lambda i, be, nu, cd: (layer, be[i], 0, 0))],
        out_specs=pl.BlockSpec(memory_space=pl.ANY),
        scratch_shapes=[pltpu.VMEM((2, rows * n_sub, LANES_V7X), F32), pltpu.VMEM((2, rows * n_sub, LANES_V7X), F32),
                        pltpu.SemaphoreType.DMA((2,)), pltpu.SemaphoreType.DMA((2,))])
    vmem = (d * f2 + 2 * f * d) * 4 + 4 * rows * d * 4 + 3 * rows * f2 * 4 + 4 * d * col * 4
    return pl.pallas_call(
        functools.partial(_expert_kernel, rows, n_all, col),
        grid_spec=grid_spec,
        out_shape=jax.ShapeDtypeStruct(((cfg.top_k * n_all + 2 * rows) * n_sub, LANES_V7X), F32),
        compiler_params=_cparams(("arbitrary",), vmem),
        name="moe_experts",
    )(block_e, n_used, row_code, fx, w_gu, w_dn)


def _combine_kernel(final, *refs):
    if final:
        x_ref, a_ref, b_ref, w_ref, gt_ref, gf_ref, o_ref = refs
    else:
        x_ref, a_ref, b_ref, w_ref, gt_ref, o_ref = refs
    w = w_ref[...]
    tm = x_ref.shape[1]
    y = x_ref[0] + gt_ref[0] * (w[:, 0:1] * _load_slabs(a_ref, tm) + w[:, 1:2] * _load_slabs(b_ref, tm))
    if final:
        y = y * lax.rsqrt(jnp.mean(y * y, axis=-1, keepdims=True) + EPS) * gf_ref[...]
    o_ref[0] = y


def _combine(x, slots, wts, tok_off, n_all, gate, tm, norm_final=None):
    b, t, d = x.shape
    per_b = t // tm
    off0 = tok_off // tm
    off1 = (n_all + tok_off) // tm
    final = norm_final is not None
    in_specs = [pl.BlockSpec((1, tm, d), lambda i, j: (i, j, 0)),
                pl.BlockSpec((tm * (d // LANES_V7X), LANES_V7X), lambda i, j: (off0 + i * per_b + j, 0)),
                pl.BlockSpec((tm * (d // LANES_V7X), LANES_V7X), lambda i, j: (off1 + i * per_b + j, 0)),
                pl.BlockSpec((tm, wts.shape[1]), lambda i, j: (off0 + i * per_b + j, 0)),
                pl.BlockSpec((1, 1, d), lambda i, j: (i, 0, 0))]
    args = [x, slots, slots, wts, gate]
    if final:
        in_specs.append(pl.BlockSpec((1, d), lambda i, j: (0, 0)))
        args.append(norm_final.reshape(1, d))
    return pl.pallas_call(
        functools.partial(_combine_kernel, final),
        grid=(b, per_b),
        in_specs=in_specs,
        out_specs=pl.BlockSpec((1, tm, d), lambda i, j: (i, j, 0)),
        out_shape=jax.ShapeDtypeStruct((b, t, d), F32),
        compiler_params=_cparams(("parallel", "parallel"), 10 * tm * d * 4),
        name="moe_combine",
    )(*args)


def _route(cfg, logits):
    n_all = logits.shape[0]
    ng, eg, k = cfg.n_groups, cfg.experts_per_group, cfg.top_k
    assert k == 2, "row codes pack (token, choice) as 2 * token + choice"
    n_exp = ng * eg
    rows = cfg.moe_rows
    p_grp = jax.nn.softmax(logits[:, :ng], axis=-1)
    g_idx = jnp.argmax(p_grp, axis=-1)
    p_g = jnp.max(p_grp, axis=-1)
    le = logits[:, ng:ng + n_exp].reshape(n_all, ng, eg)
    le = jnp.take_along_axis(le, g_idx[:, None, None], axis=1)[:, 0]
    vals, e_in = lax.top_k(jax.nn.softmax(le, axis=-1), k)
    wts = p_g[:, None] * vals / jnp.sum(vals, axis=-1, keepdims=True)
    eid = (g_idx[:, None] * eg + e_in).astype(jnp.int32)

    s = n_all * k
    e_flat = eid.reshape(-1)
    counts = jnp.sum((e_flat[:, None] == jnp.arange(n_exp, dtype=jnp.int32)[None, :]).astype(jnp.int32), axis=0)
    order = jnp.argsort(e_flat, stable=True).astype(jnp.int32)
    starts = jnp.cumsum(counts) - counts
    padded = ((counts + rows - 1) // rows) * rows
    pends = jnp.cumsum(padded)
    pstarts = pends - padded
    n_blocks = -(-(s + n_exp * (rows - 1)) // rows)
    blk_start = jnp.arange(n_blocks + 1, dtype=jnp.int32) * rows
    block_e = jnp.minimum(jnp.sum((pends[None, :] <= blk_start[:, None]).astype(jnp.int32), axis=1), n_exp - 1)
    pos = jnp.arange(n_blocks * rows, dtype=jnp.int32)
    e_row = jnp.repeat(block_e[:n_blocks], rows)
    rank = pos - pstarts[e_row]
    routed = jnp.logical_and(rank < counts[e_row], pos < pends[-1])
    row_code = jnp.where(routed, order[jnp.clip(starts[e_row] + rank, 0, s - 1)], s).astype(jnp.int32)
    n_used = (pends[-1] // rows).astype(jnp.int32).reshape(1)
    return (block_e, n_used, row_code), wts


def _moe(cfg, x, ctx, g, sh_all, sc_all, w_r, b_r, w_gu, w_dn, layer):
    b, t, d = x.shape
    tm = _row_tile(cfg, t) if ctx is None else min(cfg.tok_tile, t, ctx.shape[1])
    fx, logits = _router(x, ctx, g, sh_all, sc_all, w_r, b_r, tm)
    n_all = logits.shape[0]
    plan, wts = _route(cfg, logits)
    slots = _experts(cfg, fx, *plan, w_gu, w_dn, layer)
    return slots, wts, n_all


def _cast_weight_once(w_ref, wbf_ref):
    @pl.when(jnp.logical_and(pl.program_id(0) == 0, pl.program_id(1) == 0))
    def _():
        k = wbf_ref.shape[0]
        step = min(k, 256)

        def body(c, carry):
            r0 = pl.multiple_of(c * step, step)
            wbf_ref[pl.ds(r0, step), :] = w_ref[0, pl.ds(r0, step), :].astype(BF16)
            return carry
        lax.fori_loop(0, k // step, body, 0)


def _proj_kernel(kind, n_heads, hd, *refs):
    if kind == "gate":
        h_ref, w_ref, la_ref, lc_ref, o_ref, wbf_ref = refs
    else:
        h_ref, w_ref, o_ref, wbf_ref = refs
    _cast_weight_once(w_ref, wbf_ref)
    z = jnp.dot(h_ref[0], wbf_ref[...], preferred_element_type=F32)
    if kind == "silu":
        z = _silu(z)
    elif kind == "gate":
        ls = jnp.minimum(z, 0.0) - jnp.log(1.0 + jnp.exp(-jnp.abs(z)))
        c = lc_ref[...] + ls
        a = jnp.broadcast_to(la_ref[...], c.shape)
        m = jnp.maximum(a, c)
        z = m + jnp.log(1.0 + jnp.exp(-jnp.abs(a - c)))
    for h in range(n_heads):
        o_ref[0, h] = z[:, h * hd:(h + 1) * hd].astype(o_ref.dtype)


def _proj(cfg, h_all, w_all, layer, section, kind, out_dtype, log_lb=None, log1m_lb=None):
    b, s, d = h_all.shape
    n = d
    hd = cfg.head_dim
    nh = n // hd
    tm = cfg.tok_tile
    in_specs = [pl.BlockSpec((1, tm, d), lambda i, j: (i, j, 0)),
                pl.BlockSpec((1, d, n), lambda i, j: (layer, 0, section), pipeline_mode=pl.Buffered(1))]
    args = [h_all, w_all]
    if kind == "gate":
        in_specs += [pl.BlockSpec((1, n), lambda i, j: (0, 0)), pl.BlockSpec((1, n), lambda i, j: (0, 0))]
        args += [log_lb.reshape(1, n), log1m_lb.reshape(1, n)]
    return pl.pallas_call(
        functools.partial(_proj_kernel, kind, nh, hd),
        grid=(b, s // tm),
        in_specs=in_specs,
        out_specs=pl.BlockSpec((1, nh, tm, hd), lambda i, j: (i, 0, j, 0)),
        out_shape=jax.ShapeDtypeStruct((b, nh, s, hd), out_dtype),
        scratch_shapes=[pltpu.VMEM((d, n), BF16)],
        compiler_params=_cparams(("arbitrary", "arbitrary"), d * n * 6 + 2 * tm * d * 2 + 6 * tm * n * 4),
        name=f"hgrn_proj_{kind}",
    )(*args)


def _gla_kernel(chunk, reverse, readout, limit, *refs):
    if readout:
        q_ref, lf_ref, v_ref, of_ref, gs_ref, ng_ref, o_ref, st_ref, b_ref, q32_ref, at_ref = refs
    else:
        q_ref, lf_ref, v_ref, o_ref, st_ref, b_ref, q32_ref, at_ref = refs
    hg, tb, hd = q_ref.shape[1], q_ref.shape[2], q_ref.shape[3]
    n_chunks = tb // chunk
    shift = chunk.bit_length() - 1

    @pl.when(pl.program_id(2) == 0)
    def _():
        st_ref[...] = jnp.zeros(st_ref.shape, F32)

    rr = lax.broadcasted_iota(jnp.int32, (tb, tb), 0)
    cc = lax.broadcasted_iota(jnp.int32, (tb, tb), 1)
    same = (rr >> shift) == (cc >> shift)
    tri = jnp.where(jnp.logical_and(same, (cc >= rr) if reverse else (cc <= rr)), 1.0, 0.0).astype(BF16)
    ar = lax.broadcasted_iota(jnp.int32, (chunk, chunk), 0)
    ac = lax.broadcasted_iota(jnp.int32, (chunk, chunk), 1)
    at_mask = (ar >= ac) if reverse else (ar <= ac)
    row_id = lax.broadcasted_iota(jnp.int32, (chunk, hd), 0)
    order = list(range(n_chunks))[::-1] if reverse else list(range(n_chunks))
    edge = 0 if reverse else chunk - 1
    grp = b_ref.shape[0]

    def group(hp, carry):
        heads = [hp * grp + u for u in range(grp)]
        lf_all = jnp.concatenate([lf_ref[0, h] for h in heads], axis=-1)
        hi = lf_all.astype(BF16)
        r1 = lf_all - hi.astype(F32)
        mid = r1.astype(BF16)
        lo = (r1 - mid.astype(F32)).astype(BF16)
        b_all = (jnp.dot(tri, hi, preferred_element_type=F32) + jnp.dot(tri, mid, preferred_element_type=F32)
                 + jnp.dot(tri, lo, preferred_element_type=F32))
        bs, kks, qfs, btots = [], [], [], []
        worst = None
        for u, h in enumerate(heads):
            b = b_all[:, u * hd:(u + 1) * hd]
            b_ref[u] = b
            btot = [b[c * chunk + edge:c * chunk + edge + 1, :] for c in range(n_chunks)]
            for bt in btot:
                worst = bt if worst is None else jnp.minimum(worst, bt)
            bs.append(b)
            btots.append(btot)
            kks.append(1.0 - jnp.exp(lf_all[:, u * hd:(u + 1) * hd]))
            qfs.append(q_ref[0, h].astype(F32))
        fast = jnp.min(worst) >= -limit

        @pl.when(fast)
        def _():
            for u in range(grp):
                for c in range(n_chunks):
                    sl = slice(c * chunk, (c + 1) * chunk)
                    d = bs[u][sl] - 0.5 * btots[u][c]
                    qh = (qfs[u][sl] * jnp.exp(d)).astype(BF16)
                    kh = (kks[u][sl] * jnp.exp(-d)).astype(BF16)
                    at = lax.dot_general(kh, qh, (((1,), (1,)), ((), ())), preferred_element_type=F32)
                    at_ref[u, c] = jnp.where(at_mask, at, 0.0)

        @pl.when(jnp.logical_not(fast))
        def _():
            for u in range(grp):
                q32_ref[u] = qfs[u]
                for c in range(n_chunks):
                    c0 = c * chunk
                    bc = bs[u][c0:c0 + chunk]
                    kc = kks[u][c0:c0 + chunk]

                    def tstep(t, at):
                        bt = b_ref[u, pl.ds(c0 + t, 1), :]
                        qrow = q32_ref[u, pl.ds(c0 + t, 1), :]
                        valid = (row_id >= t) if reverse else (row_id <= t)
                        e = jnp.exp(jnp.where(valid, bt - bc, -jnp.inf))
                        col = jnp.sum(e * kc * qrow, axis=-1, keepdims=True)
                        return at + col * jnp.where(ac == t, 1.0, 0.0)

                    at_ref[u, c] = lax.fori_loop(0, chunk, tstep, jnp.zeros((chunk, chunk), F32))

        for u, h in enumerate(heads):
            b, kk, btot = bs[u], kks[u], btots[u]
            qb = (qfs[u] * jnp.exp(b)).astype(BF16)
            vb = v_ref[0, h]
            st = st_ref[h]
            for c in order:
                sl = slice(c * chunk, (c + 1) * chunk)
                k2 = (kk[sl] * jnp.exp(btot[c] - b[sl])).astype(BF16)
                o = lax.dot_general(at_ref[u, c].astype(BF16), vb[sl], (((0,), (0,)), ((), ())),
                                    preferred_element_type=F32)
                o = o + lax.dot_general(qb[sl], st.astype(BF16), (((1,), (1,)), ((), ())),
                                        preferred_element_type=F32)
                st = st * jnp.exp(btot[c]) + lax.dot_general(vb[sl], k2, (((0,), (0,)), ((), ())),
                                                             preferred_element_type=F32)
                if readout:
                    o = o + of_ref[0, h, sl, :]
                    o = o * lax.rsqrt(jnp.mean(o * o, axis=-1, keepdims=True) + EPS) * ng_ref[...]
                    o_ref[0, h, sl, :] = (o * gs_ref[0, h, sl, :].astype(F32)).astype(o_ref.dtype)
                else:
                    o_ref[0, h, sl, :] = o
            st_ref[h] = st
        return carry

    lax.fori_loop(0, hg // grp, group, 0)


def _gla(cfg, qs, lf, v, reverse, n_ctx_blocks, of=None, gs=None, norm_g=None):
    b, nh, s, hd = qs.shape
    tb = cfg.tok_tile
    hg = nh
    grp = cfg.gla_group if hg % cfg.gla_group == 0 else 1
    nblk = s // tb
    readout = of is not None

    if reverse:
        def blk(j):
            return jnp.where(j < n_ctx_blocks, n_ctx_blocks - 1 - j, nblk - 1 - (j - n_ctx_blocks))
    else:
        def blk(j):
            return j

    spec = pl.BlockSpec((1, hg, tb, hd), lambda i, g, j: (i, g, blk(j), 0))
    in_specs = [spec, spec, spec]
    args = [qs, lf, v]
    if readout:
        in_specs += [spec, spec, pl.BlockSpec((1, hd), lambda i, g, j: (0, 0))]
        args += [of, gs, norm_g.reshape(1, hd)]
    out_dtype = BF16 if readout else F32
    vmem = 2 * hg * tb * hd * (2 + 4 + 2 + 4 + 2 + 4) + hg * hd * hd * 4 + 8 * tb * tb * 4 + 16 * tb * hd * 4
    return pl.pallas_call(
        functools.partial(_gla_kernel, cfg.gla_chunk, reverse, readout, cfg.fast_decay_limit),
        grid=(b, nh // hg, nblk),
        in_specs=in_specs,
        out_specs=spec,
        out_shape=jax.ShapeDtypeStruct((b, nh, s, hd), out_dtype),
        scratch_shapes=[pltpu.VMEM((hg, hd, hd), F32), pltpu.VMEM((grp, tb, hd), F32),
                        pltpu.VMEM((grp, tb, hd), F32),
                        pltpu.VMEM((grp, tb // cfg.gla_chunk, cfg.gla_chunk, cfg.gla_chunk), F32)],
        compiler_params=_cparams(("parallel", "parallel", "arbitrary"), vmem),
        name="hgrn_scan_bwd" if reverse else "hgrn_scan_fwd",
    )(*args)


def _out_proj_kernel(n_heads, y_ref, x_ref, w_ref, gt_ref, o_ref, wbf_ref):
    _cast_weight_once(w_ref, wbf_ref)
    y = jnp.concatenate([y_ref[0, h] for h in range(n_heads)], axis=-1)
    o_ref[0] = x_ref[0] + gt_ref[0] * jnp.dot(y, wbf_ref[...], preferred_element_type=F32)


def _out_proj(cfg, y_heads, x, w_out_all, layer, gate, ctx_tiles):
    b, t, d = x.shape
    nh, hd = y_heads.shape[1], y_heads.shape[3]
    tm = cfg.tok_tile
    return pl.pallas_call(
        functools.partial(_out_proj_kernel, nh),
        grid=(b, t // tm),
        in_specs=[pl.BlockSpec((1, nh, tm, hd), lambda i, j: (i, 0, j + ctx_tiles, 0)),
                  pl.BlockSpec((1, tm, d), lambda i, j: (i, j, 0)),
                  pl.BlockSpec((1, d, d), lambda i, j: (layer, 0, 0), pipeline_mode=pl.Buffered(1)),
                  pl.BlockSpec((1, 1, d), lambda i, j: (i, 0, 0))],
        out_specs=pl.BlockSpec((1, tm, d), lambda i, j: (i, j, 0)),
        out_shape=jax.ShapeDtypeStruct((b, t, d), F32),
        scratch_shapes=[pltpu.VMEM((d, d), BF16)],
        compiler_params=_cparams(("arbitrary", "arbitrary"), d * d * 6 + 8 * tm * d * 4),
        name="hgrn_out_proj",
    )(y_heads, x, w_out_all, gate)


def _hgrn_mixer(cfg, x, ctx, g, sh_all, sc_all, gate, w_in_all, norm_g, w_out_all, layer, lb):
    b, t, d = x.shape
    tm = cfg.tok_tile
    h_all = _norm_mod_seq(ctx, x, g, sh_all, sc_all, BF16, tm)
    log_lb = jnp.log(lb)
    log1m_lb = jnp.log1p(-lb)
    qs = _proj(cfg, h_all, w_in_all, layer, 0, "silu", BF16)
    gs = _proj(cfg, h_all, w_in_all, layer, 1, "silu", BF16)
    lf = _proj(cfg, h_all, w_in_all, layer, 2, "gate", F32, log_lb[0], log1m_lb[0])
    lr = _proj(cfg, h_all, w_in_all, layer, 3, "gate", F32, log_lb[1], log1m_lb[1])
    v = _proj(cfg, h_all, w_in_all, layer, 4, "none", BF16)
    n_ctx_blocks = ctx.shape[1] // tm
    of = _gla(cfg, qs, lf, v, False, n_ctx_blocks)
    y = _gla(cfg, qs, lr, v, True, n_ctx_blocks, of, gs, norm_g)
    return _out_proj(cfg, y, x, w_out_all, layer, gate, n_ctx_blocks)


def _forward(cfg, x, c, ctx, c_ctx, norm_mix, norm_ffn, w_ada, b_ada, pool_w, pool_scale, hgrn_w_in, hgrn_norm,
             hgrn_w_out, hgrn_lb_logits, router_w_group, router_b_group, router_w_expert, router_b_expert,
             moe_w_gate_up, moe_w_down, norm_final):
    b, t, d = x.shape
    depth = w_ada.shape[0]
    n_mixers = 2
    p_lb = jax.nn.softmax(hgrn_lb_logits.astype(F32), axis=0)
    lb_all = jnp.cumsum(p_lb, axis=0) - p_lb[0]

    rows = -(-(b + 1) // SUBLANES_V7X) * SUBLANES_V7X
    cond = jnp.zeros((rows, d), F32).at[:b].set(c).at[b].set(c_ctx)
    mods = _adaln(cond, w_ada, b_ada)

    ng, ne = cfg.n_groups, cfg.n_groups * cfg.experts_per_group
    for i in range(depth):
        ctx_needed = i < depth - 1
        j = i // n_mixers
        m = [mods[i, :, q * d:(q + 1) * d].reshape(rows, 1, d) for q in range(6)]
        sh_m, sc_m, gt_m, sh_f, sc_f, gt_f = m
        ctx_rows = lambda a: jnp.broadcast_to(a[b:b + 1], (b, 1, d))
        if i % n_mixers == 0:
            wp = pool_w[j].astype(BF16)
            x_new = _pool_mixer(cfg, x, norm_mix[i], sh_m, sc_m, gt_m, wp, pool_scale[j], True)
            if ctx_needed:
                ctx = _pool_mixer(cfg, ctx, norm_mix[i], ctx_rows(sh_m), ctx_rows(sc_m), ctx_rows(gt_m), wp,
                                  pool_scale[j], False)
            x = x_new
        else:
            x = _hgrn_mixer(cfg, x, ctx, norm_mix[i], sh_m, sc_m, gt_m, hgrn_w_in, hgrn_norm[j], hgrn_w_out, j,
                            lb_all[i])
        lanes = -(-(ng + ne) // LANES_V7X) * LANES_V7X
        w_r = jnp.zeros((d, lanes), F32).at[:, :ng].set(router_w_group[i]).at[:, ng:ng + ne].set(router_w_expert[i])
        b_r = jnp.zeros((1, lanes), F32).at[0, :ng].set(router_b_group[i]).at[0, ng:ng + ne].set(router_b_expert[i])
        slots, wts, n_all = _moe(cfg, x, ctx if ctx_needed else None, norm_ffn[i], sh_f, sc_f, w_r, b_r,
                                 moe_w_gate_up, moe_w_down, i)
        tm = _row_tile(cfg, t)
        last = i == depth - 1
        x_next = _combine(x, slots, wts, 0, n_all, gt_f, tm, norm_final if last else None)
        if ctx_needed:
            ctx = _combine(ctx, slots, wts, b * t, n_all, ctx_rows(gt_f), min(cfg.tok_tile, ctx.shape[1]))
        x = x_next
    return x


def kernel(x, c, ctx, c_ctx, norm_mix, norm_ffn, w_ada, b_ada, pool_w, pool_scale, hgrn_w_in, hgrn_norm, hgrn_w_out, hgrn_lb_logits, router_w_group, router_b_group, router_w_expert, router_b_expert, moe_w_gate_up, moe_w_down, norm_final):
    return _forward(Cfg(), x, c, ctx, c_ctx, norm_mix, norm_ffn, w_ada, b_ada, pool_w, pool_scale, hgrn_w_in,
                    hgrn_norm, hgrn_w_out, hgrn_lb_logits, router_w_group, router_b_group, router_w_expert,
                    router_b_expert, moe_w_gate_up, moe_w_down, norm_final)
```

```python
import functools
from typing import NamedTuple

import numpy as np
import jax
import jax.numpy as jnp
from jax import lax
from jax.experimental import pallas as pl
from jax.experimental.pallas import tpu as pltpu

F32 = jnp.float32
BF16 = jnp.bfloat16
HIGHEST = lax.Precision.HIGHEST
EPS = 1e-6

LANES_V7X = 128
SUBLANES_V7X = 8
VMEM_BYTES_V7X = 64 * 1024 * 1024


class Cfg(NamedTuple):
    grid_w: int = 64
    pool_windows: tuple = (2, 4, 8, 16)
    head_dim: int = LANES_V7X
    n_groups: int = 4
    experts_per_group: int = 8
    top_k: int = 2
    moe_rows: int = 256
    gla_chunk: int = 64
    gla_group: int = 8
    tok_tile: int = 256
    row_tile: int = 512
    fast_decay_limit: float = 120.0


def _vmem_limit(nbytes):
    return int(min(max(nbytes * 5 // 4, 16 * 1024 * 1024), VMEM_BYTES_V7X - 6 * 1024 * 1024))


def _cparams(sem, vmem_bytes):
    return pltpu.CompilerParams(dimension_semantics=sem, vmem_limit_bytes=_vmem_limit(vmem_bytes))


def _rms_mod(x, g, shift, scale):
    y = x * lax.rsqrt(jnp.mean(x * x, axis=-1, keepdims=True) + EPS) * g
    return y * (1.0 + scale) + shift


def _silu(x):
    return x * jax.nn.sigmoid(x)


def _store_slabs(ref, val):
    rows, d = val.shape
    n_sub = d // LANES_V7X
    for s in range(n_sub):
        ref[pl.ds(s, rows, stride=n_sub), :] = val[:, s * LANES_V7X:(s + 1) * LANES_V7X]


def _load_slabs(ref, rows):
    n_sub = ref.shape[0] // rows
    return jnp.concatenate([ref[pl.ds(s, rows, stride=n_sub), :] for s in range(n_sub)], axis=-1)


def _adaln_kernel(cond_ref, w_ref, b_ref, o_ref):
    s = _silu(cond_ref[...])
    o_ref[0] = jnp.dot(s, w_ref[0], precision=HIGHEST, preferred_element_type=F32) + b_ref[0]


def _adaln(cond, w_ada, b_ada):
    depth, d, n = w_ada.shape
    rows = cond.shape[0]
    tn = 1024 if n % 1024 == 0 else n
    return pl.pallas_call(
        _adaln_kernel,
        grid=(depth, n // tn),
        in_specs=[pl.BlockSpec((rows, d), lambda l, j: (0, 0)),
                  pl.BlockSpec((1, d, tn), lambda l, j: (l, 0, j)),
                  pl.BlockSpec((1, 1, tn), lambda l, j: (l, 0, j))],
        out_specs=pl.BlockSpec((1, rows, tn), lambda l, j: (l, 0, j)),
        out_shape=jax.ShapeDtypeStruct((depth, rows, n), F32),
        compiler_params=_cparams(("arbitrary", "arbitrary"), 2 * d * tn * 4 + 4 * rows * (d + tn) * 4),
        name="adaln",
    )(cond, w_ada, b_ada.reshape(depth, 1, n))


def _norm_mod_kernel(x_ref, g_ref, sh_ref, sc_ref, o_ref):
    o_ref[0] = _rms_mod(x_ref[0], g_ref[...], sh_ref[0], sc_ref[0]).astype(o_ref.dtype)


def _norm_mod(x, g, sh, sc, out_dtype, tm):
    b, t, d = x.shape
    return pl.pallas_call(
        _norm_mod_kernel,
        grid=(b, t // tm),
        in_specs=[pl.BlockSpec((1, tm, d), lambda i, j: (i, j, 0)),
                  pl.BlockSpec((1, d), lambda i, j: (0, 0)),
                  pl.BlockSpec((1, 1, d), lambda i, j: (i, 0, 0)),
                  pl.BlockSpec((1, 1, d), lambda i, j: (i, 0, 0))],
        out_specs=pl.BlockSpec((1, tm, d), lambda i, j: (i, j, 0)),
        out_shape=jax.ShapeDtypeStruct((b, t, d), out_dtype),
        compiler_params=_cparams(("parallel", "parallel"), 6 * tm * d * 4),
        name="norm_mod",
    )(x, g.reshape(1, d), sh, sc)


def _norm_mod_seq_kernel(n_ctx_tiles, c_ref, x_ref, g_ref, sh_ref, sc_ref, o_ref):
    j = pl.program_id(1)
    src = jnp.where(j < n_ctx_tiles, c_ref[0], x_ref[0])
    o_ref[0] = _rms_mod(src, g_ref[...], sh_ref[0], sc_ref[0]).astype(o_ref.dtype)


def _norm_mod_seq(ctx, x, g, sh_all, sc_all, out_dtype, tm):
    b, t, d = x.shape
    l = ctx.shape[1]
    nc, nx = l // tm, t // tm
    return pl.pallas_call(
        functools.partial(_norm_mod_seq_kernel, nc),
        grid=(b, nc + nx),
        in_specs=[pl.BlockSpec((1, tm, d), lambda i, j: (i, jnp.minimum(j, nc - 1), 0)),
                  pl.BlockSpec((1, tm, d), lambda i, j: (i, jnp.maximum(j - nc, 0), 0)),
                  pl.BlockSpec((1, d), lambda i, j: (0, 0)),
                  pl.BlockSpec((1, 1, d), lambda i, j: (jnp.where(j < nc, b, i), 0, 0)),
                  pl.BlockSpec((1, 1, d), lambda i, j: (jnp.where(j < nc, b, i), 0, 0))],
        out_specs=pl.BlockSpec((1, tm, d), lambda i, j: (i, j, 0)),
        out_shape=jax.ShapeDtypeStruct((b, l + t, d), out_dtype),
        compiler_params=_cparams(("parallel", "arbitrary"), 8 * tm * d * 4),
        name="norm_mod_seq",
    )(ctx, x, g.reshape(1, d), sh_all, sc_all)


def _win_matrix(length, k):
    pos = np.arange(length)
    lo = np.clip(pos - k // 2, 0, length - 1)
    hi = np.clip(pos + (k - k // 2 - 1), 0, length - 1)
    m = np.zeros((length, length), np.float32)
    for p in range(length):
        m[p, lo[p]:hi[p] + 1] = 1.0 / float(hi[p] - lo[p] + 1)
    return m, (1.0 / (hi - lo + 1)).astype(np.float32)


def _pool_tokens_kernel(k, w, slab, vertical, inv_ref, ph_ref, h_ref, o_ref, pad_ref):
    t, tc = h_ref.shape[1], h_ref.shape[2]
    if not vertical:
        for s in range(t // slab):
            hs = h_ref[0, s * slab:(s + 1) * slab, :]
            o_ref[0, s * slab:(s + 1) * slab, :] = (
                jnp.dot(ph_ref[...], hs, precision=HIGHEST, preferred_element_type=F32) - hs)
        return
    top = (k // 2) * w
    bot = (k - k // 2 - 1) * w
    pad_ref[0:top, :] = jnp.zeros((top, tc), F32)
    if bot:
        pad_ref[top + t:top + t + bot, :] = jnp.zeros((bot, tc), F32)
    for s in range(t // slab):
        hs = h_ref[0, s * slab:(s + 1) * slab, :]
        pad_ref[top + s * slab:top + (s + 1) * slab, :] = jnp.dot(
            ph_ref[...], hs, precision=HIGHEST, preferred_element_type=F32)

    def row(r, carry):
        base = pl.multiple_of(r * w, w)
        acc = pad_ref[pl.ds(base, w), :]
        for dr in range(1, k):
            acc = acc + pad_ref[pl.ds(pl.multiple_of(base + dr * w, w), w), :]
        o_ref[0, pl.ds(base, w), :] = acc * inv_ref[r] - h_ref[0, pl.ds(base, w), :]
        return carry

    lax.fori_loop(0, t // w, row, 0)


def _pool_tokens(h, group, gdim, k, grid_w, vertical):
    b, t, d = h.shape
    tc = min(gdim, 256)
    per = gdim // tc
    if vertical:
        slab = max(grid_w, min(256, t))
        mh, _ = _win_matrix(grid_w, k)
        ph = np.kron(np.eye(slab // grid_w, dtype=np.float32), mh)
        _, inv_v = _win_matrix(t // grid_w, k)
    else:
        slab = t
        ph, _ = _win_matrix(t, k)
        inv_v = np.ones((1,), np.float32)
    pad_rows = t + (k - 1) * grid_w if vertical else SUBLANES_V7X
    return pl.pallas_call(
        functools.partial(_pool_tokens_kernel, k, grid_w, slab, vertical),
        grid=(b, per),
        in_specs=[pl.BlockSpec(memory_space=pltpu.SMEM),
                  pl.BlockSpec((slab, slab), lambda i, j: (0, 0)),
                  pl.BlockSpec((1, t, tc), lambda i, j: (i, 0, group * per + j))],
        out_specs=pl.BlockSpec((1, t, tc), lambda i, j: (i, 0, j)),
        out_shape=jax.ShapeDtypeStruct((b, t, gdim), F32),
        scratch_shapes=[pltpu.VMEM((pad_rows, tc), F32)],
        compiler_params=_cparams(("parallel", "parallel"), (4 * t + pad_rows) * tc * 4 + 2 * slab * slab * 4),
        name=f"pool_tokens_k{k}",
    )(jnp.asarray(inv_v), jnp.asarray(ph), h)


def _pool_out_kernel(n_groups, *refs):
    d_refs = refs[:n_groups]
    x_ref, w_ref, ps_ref, gt_ref, o_ref = refs[n_groups:]
    ys = [jnp.dot(d_refs[j][0].astype(BF16), w_ref[j], preferred_element_type=F32) for j in range(n_groups)]
    y = jnp.concatenate(ys, axis=-1) * ps_ref[...]
    o_ref[0] = x_ref[0] + gt_ref[0] * y


def _pool_out(ds, x, w_pool_bf16, pool_scale, gate, tm):
    b, t, d = x.shape
    ng, gdim, _ = w_pool_bf16.shape
    return pl.pallas_call(
        functools.partial(_pool_out_kernel, ng),
        grid=(b, t // tm),
        in_specs=[pl.BlockSpec((1, tm, gdim), lambda i, j: (i, j, 0)) for _ in range(ng)] + [
            pl.BlockSpec((1, tm, d), lambda i, j: (i, j, 0)),
            pl.BlockSpec((ng, gdim, gdim), lambda i, j: (0, 0, 0)),
            pl.BlockSpec((1, d), lambda i, j: (0, 0)),
            pl.BlockSpec((1, 1, d), lambda i, j: (i, 0, 0))],
        out_specs=pl.BlockSpec((1, tm, d), lambda i, j: (i, j, 0)),
        out_shape=jax.ShapeDtypeStruct((b, t, d), F32),
        compiler_params=_cparams(("parallel", "parallel"), 8 * tm * d * 4 + 2 * ng * gdim * gdim * 2),
        name="pool_out",
    )(*ds, x, w_pool_bf16, pool_scale.reshape(1, d), gate)


def _row_tile(cfg, t):
    return cfg.row_tile if t % cfg.row_tile == 0 else min(cfg.tok_tile, t)


def _pool_mixer(cfg, x, g, sh, sc, gate, w_pool_bf16, pool_scale, on_grid):
    b, t, d = x.shape
    tm = _row_tile(cfg, t)
    h = _norm_mod(x, g, sh, sc, F32, tm)
    gdim = d // len(cfg.pool_windows)
    ds = [_pool_tokens(h, j, gdim, k, cfg.grid_w, on_grid) for j, k in enumerate(cfg.pool_windows)]
    return _pool_out(ds, x, w_pool_bf16, pool_scale, gate, tm)


def _route_tile(lg, ng, eg):
    lane = lax.broadcasted_iota(jnp.int32, lg.shape, 1)
    big = jnp.int32(lg.shape[1])
    shift = eg.bit_length() - 1

    def first_max(vals):
        m = jnp.max(vals, axis=-1, keepdims=True)
        return m, jnp.min(jnp.where(vals == m, lane, big), axis=-1, keepdims=True)

    is_g = lane < ng
    gmax, g_idx = first_max(jnp.where(is_g, lg, -jnp.inf))
    p_g = 1.0 / jnp.sum(jnp.where(is_g, jnp.exp(lg - gmax), 0.0), axis=-1, keepdims=True)
    in_grp = jnp.logical_and(jnp.logical_and(lane >= ng, lane < ng + ng * eg), ((lane - ng) >> shift) == g_idx)
    el = jnp.where(in_grp, lg, -jnp.inf)
    m1, i1 = first_max(el)
    m2, i2 = first_max(jnp.where(lane == i1, -jnp.inf, el))
    t = jnp.exp(m2 - m1)
    w1 = p_g / (1.0 + t)
    w2 = w1 * t
    out = jnp.where(lane == 0, w1, jnp.where(lane == 1, w2, 0.0))
    out = jnp.where(lane == 2, (i1 - ng).astype(F32), out)
    return jnp.where(lane == 3, (i2 - ng).astype(F32), out)


def _router_kernel(nx_tiles, has_ctx, ng, eg, *refs):
    if has_ctx:
        x_ref, c_ref, g_ref, sh_ref, sc_ref, wr_ref, br_ref, fx_ref, rt_ref = refs
        src = jnp.where(pl.program_id(0) < nx_tiles, x_ref[0], c_ref[0])
    else:
        x_ref, g_ref, sh_ref, sc_ref, wr_ref, br_ref, fx_ref, rt_ref = refs
        src = x_ref[0]
    h = _rms_mod(src, g_ref[...], sh_ref[0], sc_ref[0])
    _store_slabs(fx_ref, h)
    lg = jnp.dot(h, wr_ref[...], precision=HIGHEST, preferred_element_type=F32) + br_ref[...]
    rt_ref[...] = _route_tile(lg, ng, eg)


def _router(cfg, x, ctx, g, sh_all, sc_all, w_r, b_r, tm):
    b, t, d = x.shape
    per_b = t // tm
    nx = b * per_b
    has_ctx = ctx is not None
    nc = b * (ctx.shape[1] // tm) if has_ctx else 0
    per_c = (ctx.shape[1] // tm) if has_ctx else 1
    n_all = (nx + nc) * tm
    lanes = w_r.shape[1]

    def mod_row(i):
        return jnp.where(i < nx, jnp.minimum(i, nx - 1) // per_b, b) if has_ctx else i // per_b

    in_specs = [pl.BlockSpec((1, tm, d), lambda i: (jnp.minimum(i, nx - 1) // per_b, jnp.minimum(i, nx - 1) % per_b, 0))]
    args = [x]
    if has_ctx:
        in_specs.append(pl.BlockSpec(
            (1, tm, d), lambda i: (jnp.maximum(i - nx, 0) // per_c, jnp.maximum(i - nx, 0) % per_c, 0)))
        args.append(ctx)
    in_specs += [pl.BlockSpec((1, d), lambda i: (0, 0)),
                 pl.BlockSpec((1, 1, d), lambda i: (mod_row(i), 0, 0)),
                 pl.BlockSpec((1, 1, d), lambda i: (mod_row(i), 0, 0)),
                 pl.BlockSpec((d, lanes), lambda i: (0, 0)),
                 pl.BlockSpec((1, lanes), lambda i: (0, 0))]
    args += [g.reshape(1, d), sh_all, sc_all, w_r, b_r]
    return pl.pallas_call(
        functools.partial(_router_kernel, nx, has_ctx, cfg.n_groups, cfg.experts_per_group),
        grid=(nx + nc,),
        in_specs=in_specs,
        out_specs=[pl.BlockSpec((tm * (d // LANES_V7X), LANES_V7X), lambda i: (i, 0)),
                   pl.BlockSpec((tm, lanes), lambda i: (i, 0))],
        out_shape=[jax.ShapeDtypeStruct((n_all * (d // LANES_V7X), LANES_V7X), F32),
                   jax.ShapeDtypeStruct((n_all, lanes), F32)],
        compiler_params=_cparams(("arbitrary",), 8 * tm * d * 4 + 2 * d * lanes * 4),
        name="moe_router",
    )(*args)


def _expert_kernel(rows, n_all, col, be_ref, nu_ref, code_ref, fx_hbm, wgu_ref, wdn_ref, out_hbm,
                   xbuf, ybuf, gsem, ssem):
    i = pl.program_id(0)
    n_used = nu_ref[0]
    slot = i % 2
    other = 1 - slot
    n_slots = n_all * 2
    last_blk = code_ref.shape[0] // rows - 1
    f, d = wdn_ref.shape[2], wdn_ref.shape[3]
    n_sub = d // LANES_V7X

    def slab(ref, row, count=1):
        return ref.at[pl.ds(pl.multiple_of(row * n_sub, n_sub), count * n_sub), :]

    def gather_row(blk, sl, r):
        v = code_ref[blk * rows + r]
        src = jnp.where(v >= n_slots, 0, v >> 1)
        return pltpu.make_async_copy(slab(fx_hbm, src), slab(xbuf.at[sl], r), gsem.at[sl])

    def scatter_row(blk, sl, r, none):
        v = jnp.where(none, n_slots, code_ref[blk * rows + r])
        dst = jnp.where(v >= n_slots, n_slots + sl * rows + r, (v & 1) * n_all + (v >> 1))
        return pltpu.make_async_copy(slab(ybuf.at[sl], r), slab(out_hbm, dst), ssem.at[sl])

    def wait_gather(sl):
        pltpu.make_async_copy(slab(fx_hbm, 0, rows), xbuf.at[sl], gsem.at[sl]).wait()

    def wait_scatter(sl):
        pltpu.make_async_copy(ybuf.at[sl], slab(out_hbm, 0, rows), ssem.at[sl]).wait()

    @pl.when(i == 0)
    def _():
        ybuf[...] = jnp.zeros(ybuf.shape, F32)
        for sl in range(2):
            pltpu.make_async_copy(ybuf.at[sl], slab(out_hbm, n_slots + sl * rows, rows), ssem.at[sl]).start()
        wait_scatter(1)

        @pl.when(n_used > 0)
        def _():
            def body(r, c):
                gather_row(0, 0, r).start()
                return c
            lax.fori_loop(0, rows, body, 0)

        @pl.when(n_used == 0)
        def _():
            wait_scatter(0)

    @pl.when(i < n_used)
    def _():
        wait_gather(slot)
        nxt = jnp.minimum(i + 1, last_blk)
        prev = jnp.maximum(i - 1, 0)
        no_prev = i == 0
        xb = _load_slabs(xbuf.at[slot], rows).astype(BF16)
        n1 = f // col
        per1 = rows // n1
        acts = []
        for j in range(n1):
            g = jnp.dot(xb, wgu_ref[0, 0, :, j * col:(j + 1) * col].astype(BF16), preferred_element_type=F32)
            u = jnp.dot(xb, wgu_ref[0, 0, :, f + j * col:f + (j + 1) * col].astype(BF16),
                        preferred_element_type=F32)
            acts.append((_silu(g) * u).astype(BF16))
            for r in range(j * per1, (j + 1) * per1):
                scatter_row(prev, other, r, no_prev).start()
                gather_row(nxt, other, r).start()
        act = jnp.concatenate(acts, axis=-1)
        wait_scatter(slot)
        yslot = ybuf.at[slot]
        per_col = col // LANES_V7X
        for j in range(d // col):
            y = jnp.dot(act, wdn_ref[0, 0, :, j * col:(j + 1) * col].astype(BF16), preferred_element_type=F32)
            for q in range(per_col):
                yslot[pl.ds(j * per_col + q, rows, stride=n_sub), :] = y[:, q * LANES_V7X:(q + 1) * LANES_V7X]

    @pl.when(jnp.logical_and(i == n_used, i >= 1))
    def _():
        wait_gather(slot)

        def body(r, c):
            scatter_row(i - 1, other, r, False).start()
            return c
        lax.fori_loop(0, rows, body, 0)
        wait_scatter(other)
        wait_scatter(slot)


def _experts(cfg, fx, block_e, n_used, row_code, w_gu, w_dn, layer):
    d, f2 = w_gu.shape[2], w_gu.shape[3]
    n_sub = d // LANES_V7X
    n_all = fx.shape[0] // n_sub
    f = f2 // 2
    rows = cfg.moe_rows
    col = min(256, f, d)
    n_steps = block_e.shape[0]
    grid_spec = pltpu.PrefetchScalarGridSpec(
        num_scalar_prefetch=3,
        grid=(n_steps,),
        in_specs=[pl.BlockSpec(memory_space=pl.ANY),
                  pl.BlockSpec((1, 1, d, f2), lambda i, be, nu, cd: (layer, be[i], 0, 0),
                               pipeline_mode=pl.Buffered(1)),
                  pl.BlockSpec((1, 1, f, d), lambda i, be, nu, cd: (layer, be[i], 0, 0))],
        out_specs=pl.BlockSpec(memory_space=pl.ANY),
        scratch_shapes=[pltpu.VMEM((2, rows * n_sub, LANES_V7X), F32), pltpu.VMEM((2, rows * n_sub, LANES_V7X), F32),
                        pltpu.SemaphoreType.DMA((2,)), pltpu.SemaphoreType.DMA((2,))])
    vmem = (d * f2 + 2 * f * d) * 4 + 4 * rows * d * 4 + 3 * rows * f2 * 4 + 4 * d * col * 4
    return pl.pallas_call(
        functools.partial(_expert_kernel, rows, n_all, col),
        grid_spec=grid_spec,
        out_shape=jax.ShapeDtypeStruct(((cfg.top_k * n_all + 2 * rows) * n_sub, LANES_V7X), F32),
        compiler_params=_cparams(("arbitrary",), vmem),
        name="moe_experts",
    )(block_e, n_used, row_code, fx, w_gu, w_dn)


def _combine_kernel(final, *refs):
    if final:
        x_ref, a_ref, b_ref, w_ref, gt_ref, gf_ref, o_ref = refs
    else:
        x_ref, a_ref, b_ref, w_ref, gt_ref, o_ref = refs
    w = w_ref[...]
    tm = x_ref.shape[1]
    y = x_ref[0] + gt_ref[0] * (w[:, 0:1] * _load_slabs(a_ref, tm) + w[:, 1:2] * _load_slabs(b_ref, tm))
    if final:
        y = y * lax.rsqrt(jnp.mean(y * y, axis=-1, keepdims=True) + EPS) * gf_ref[...]
    o_ref[0] = y


def _combine(x, slots, wts, tok_off, n_all, gate, tm, norm_final=None):
    b, t, d = x.shape
    per_b = t // tm
    off0 = tok_off // tm
    off1 = (n_all + tok_off) // tm
    final = norm_final is not None
    in_specs = [pl.BlockSpec((1, tm, d), lambda i, j: (i, j, 0)),
                pl.BlockSpec((tm * (d // LANES_V7X), LANES_V7X), lambda i, j: (off0 + i * per_b + j, 0)),
                pl.BlockSpec((tm * (d // LANES_V7X), LANES_V7X), lambda i, j: (off1 + i * per_b + j, 0)),
                pl.BlockSpec((tm, wts.shape[1]), lambda i, j: (off0 + i * per_b + j, 0)),
                pl.BlockSpec((1, 1, d), lambda i, j: (i, 0, 0))]
    args = [x, slots, slots, wts, gate]
    if final:
        in_specs.append(pl.BlockSpec((1, d), lambda i, j: (0, 0)))
        args.append(norm_final.reshape(1, d))
    return pl.pallas_call(
        functools.partial(_combine_kernel, final),
        grid=(b, per_b),
        in_specs=in_specs,
        out_specs=pl.BlockSpec((1, tm, d), lambda i, j: (i, j, 0)),
        out_shape=jax.ShapeDtypeStruct((b, t, d), F32),
        compiler_params=_cparams(("parallel", "parallel"), 10 * tm * d * 4),
        name="moe_combine",
    )(*args)


def _route(cfg, routed):
    n_all = routed.shape[0]
    ng, eg, k = cfg.n_groups, cfg.experts_per_group, cfg.top_k
    assert k == 2, "row codes pack (token, choice) as 2 * token + choice"
    n_exp = ng * eg
    rows = cfg.moe_rows
    wts = routed[:, 0:k]
    eid = routed[:, k:2 * k].astype(jnp.int32)

    s = n_all * k
    e_flat = eid.reshape(-1)
    counts = jnp.sum((e_flat[:, None] == jnp.arange(n_exp, dtype=jnp.int32)[None, :]).astype(jnp.int32), axis=0)
    order = jnp.argsort(e_flat, stable=True).astype(jnp.int32)
    starts = jnp.cumsum(counts) - counts
    padded = ((counts + rows - 1) // rows) * rows
    pends = jnp.cumsum(padded)
    pstarts = pends - padded
    n_blocks = -(-(s + n_exp * (rows - 1)) // rows)
    blk_start = jnp.arange(n_blocks + 1, dtype=jnp.int32) * rows
    block_e = jnp.minimum(jnp.sum((pends[None, :] <= blk_start[:, None]).astype(jnp.int32), axis=1), n_exp - 1)
    pos = jnp.arange(n_blocks * rows, dtype=jnp.int32)
    e_row = jnp.repeat(block_e[:n_blocks], rows)
    rank = pos - pstarts[e_row]
    routed = jnp.logical_and(rank < counts[e_row], pos < pends[-1])
    row_code = jnp.where(routed, order[jnp.clip(starts[e_row] + rank, 0, s - 1)], s).astype(jnp.int32)
    n_used = (pends[-1] // rows).astype(jnp.int32).reshape(1)
    return (block_e, n_used, row_code), wts


def _moe(cfg, x, ctx, g, sh_all, sc_all, w_r, b_r, w_gu, w_dn, layer):
    b, t, d = x.shape
    tm = _row_tile(cfg, t) if ctx is None else min(cfg.tok_tile, t, ctx.shape[1])
    fx, routed = _router(cfg, x, ctx, g, sh_all, sc_all, w_r, b_r, tm)
    n_all = routed.shape[0]
    plan, wts = _route(cfg, routed)
    slots = _experts(cfg, fx, *plan, w_gu, w_dn, layer)
    return slots, wts, n_all


def _cast_weight_once(w_ref, wbf_ref):
    @pl.when(jnp.logical_and(pl.program_id(0) == 0, pl.program_id(1) == 0))
    def _():
        k = wbf_ref.shape[0]
        step = min(k, 256)

        def body(c, carry):
            r0 = pl.multiple_of(c * step, step)
            wbf_ref[pl.ds(r0, step), :] = w_ref[0, pl.ds(r0, step), :].astype(BF16)
            return carry
        lax.fori_loop(0, k // step, body, 0)


def _proj_kernel(kind, n_heads, hd, *refs):
    if kind == "gate":
        h_ref, w_ref, la_ref, lc_ref, o_ref, wbf_ref = refs
    else:
        h_ref, w_ref, o_ref, wbf_ref = refs
    _cast_weight_once(w_ref, wbf_ref)
    z = jnp.dot(h_ref[0], wbf_ref[...], preferred_element_type=F32)
    if kind == "silu":
        z = _silu(z)
    elif kind == "gate":
        ls = jnp.minimum(z, 0.0) - jnp.log(1.0 + jnp.exp(-jnp.abs(z)))
        c = lc_ref[...] + ls
        a = jnp.broadcast_to(la_ref[...], c.shape)
        m = jnp.maximum(a, c)
        z = m + jnp.log(1.0 + jnp.exp(-jnp.abs(a - c)))
    for h in range(n_heads):
        o_ref[0, h] = z[:, h * hd:(h + 1) * hd].astype(o_ref.dtype)


def _proj(cfg, h_all, w_all, layer, section, kind, out_dtype, log_lb=None, log1m_lb=None):
    b, s, d = h_all.shape
    n = d
    hd = cfg.head_dim
    nh = n // hd
    tm = cfg.tok_tile
    in_specs = [pl.BlockSpec((1, tm, d), lambda i, j: (i, j, 0)),
                pl.BlockSpec((1, d, n), lambda i, j: (layer, 0, section), pipeline_mode=pl.Buffered(1))]
    args = [h_all, w_all]
    if kind == "gate":
        in_specs += [pl.BlockSpec((1, n), lambda i, j: (0, 0)), pl.BlockSpec((1, n), lambda i, j: (0, 0))]
        args += [log_lb.reshape(1, n), log1m_lb.reshape(1, n)]
    return pl.pallas_call(
        functools.partial(_proj_kernel, kind, nh, hd),
        grid=(b, s // tm),
        in_specs=in_specs,
        out_specs=pl.BlockSpec((1, nh, tm, hd), lambda i, j: (i, 0, j, 0)),
        out_shape=jax.ShapeDtypeStruct((b, nh, s, hd), out_dtype),
        scratch_shapes=[pltpu.VMEM((d, n), BF16)],
        compiler_params=_cparams(("arbitrary", "arbitrary"), d * n * 6 + 2 * tm * d * 2 + 6 * tm * n * 4),
        name=f"hgrn_proj_{kind}",
    )(*args)


def _gla_kernel(chunk, reverse, readout, limit, *refs):
    if readout:
        q_ref, lf_ref, v_ref, of_ref, gs_ref, ng_ref, o_ref, st_ref, b_ref, q32_ref, at_ref = refs
    else:
        q_ref, lf_ref, v_ref, o_ref, st_ref, b_ref, q32_ref, at_ref = refs
    hg, tb, hd = q_ref.shape[1], q_ref.shape[2], q_ref.shape[3]
    n_chunks = tb // chunk
    shift = chunk.bit_length() - 1

    @pl.when(pl.program_id(2) == 0)
    def _():
        st_ref[...] = jnp.zeros(st_ref.shape, F32)

    rr = lax.broadcasted_iota(jnp.int32, (tb, tb), 0)
    cc = lax.broadcasted_iota(jnp.int32, (tb, tb), 1)
    same = (rr >> shift) == (cc >> shift)
    tri = jnp.where(jnp.logical_and(same, (cc >= rr) if reverse else (cc <= rr)), 1.0, 0.0).astype(BF16)
    ar = lax.broadcasted_iota(jnp.int32, (chunk, chunk), 0)
    ac = lax.broadcasted_iota(jnp.int32, (chunk, chunk), 1)
    at_mask = (ar >= ac) if reverse else (ar <= ac)
    row_id = lax.broadcasted_iota(jnp.int32, (chunk, hd), 0)
    order = list(range(n_chunks))[::-1] if reverse else list(range(n_chunks))
    edge = 0 if reverse else chunk - 1
    grp = b_ref.shape[0]

    def group(hp, carry):
        heads = [hp * grp + u for u in range(grp)]
        lf_all = jnp.concatenate([lf_ref[0, h] for h in heads], axis=-1)
        hi = lf_all.astype(BF16)
        r1 = lf_all - hi.astype(F32)
        mid = r1.astype(BF16)
        lo = (r1 - mid.astype(F32)).astype(BF16)
        b_all = (jnp.dot(tri, hi, preferred_element_type=F32) + jnp.dot(tri, mid, preferred_element_type=F32)
                 + jnp.dot(tri, lo, preferred_element_type=F32))
        bs, kks, qfs, btots = [], [], [], []
        worst = None
        for u, h in enumerate(heads):
            b = b_all[:, u * hd:(u + 1) * hd]
            b_ref[u] = b
            btot = [b[c * chunk + edge:c * chunk + edge + 1, :] for c in range(n_chunks)]
            for bt in btot:
                worst = bt if worst is None else jnp.minimum(worst, bt)
            bs.append(b)
            btots.append(btot)
            kks.append(1.0 - jnp.exp(lf_all[:, u * hd:(u + 1) * hd]))
            qfs.append(q_ref[0, h].astype(F32))
        fast = jnp.min(worst) >= -limit

        @pl.when(fast)
        def _():
            for u in range(grp):
                for c in range(n_chunks):
                    sl = slice(c * chunk, (c + 1) * chunk)
                    d = bs[u][sl] - 0.5 * btots[u][c]
                    qh = (qfs[u][sl] * jnp.exp(d)).astype(BF16)
                    kh = (kks[u][sl] * jnp.exp(-d)).astype(BF16)
                    at = lax.dot_general(kh, qh, (((1,), (1,)), ((), ())), preferred_element_type=F32)
                    at_ref[u, c] = jnp.where(at_mask, at, 0.0)

        @pl.when(jnp.logical_not(fast))
        def _():
            for u in range(grp):
                q32_ref[u] = qfs[u]
                for c in range(n_chunks):
                    c0 = c * chunk
                    bc = bs[u][c0:c0 + chunk]
                    kc = kks[u][c0:c0 + chunk]

                    def tstep(t, at):
                        bt = b_ref[u, pl.ds(c0 + t, 1), :]
                        qrow = q32_ref[u, pl.ds(c0 + t, 1), :]
                        valid = (row_id >= t) if reverse else (row_id <= t)
                        e = jnp.exp(jnp.where(valid, bt - bc, -jnp.inf))
                        col = jnp.sum(e * kc * qrow, axis=-1, keepdims=True)
                        return at + col * jnp.where(ac == t, 1.0, 0.0)

                    at_ref[u, c] = lax.fori_loop(0, chunk, tstep, jnp.zeros((chunk, chunk), F32))

        for u, h in enumerate(heads):
            b, kk, btot = bs[u], kks[u], btots[u]
            qb = (qfs[u] * jnp.exp(b)).astype(BF16)
            vb = v_ref[0, h]
            st = st_ref[h]
            for c in order:
                sl = slice(c * chunk, (c + 1) * chunk)
                k2 = (kk[sl] * jnp.exp(btot[c] - b[sl])).astype(BF16)
                o = lax.dot_general(at_ref[u, c].astype(BF16), vb[sl], (((0,), (0,)), ((), ())),
                                    preferred_element_type=F32)
                o = o + lax.dot_general(qb[sl], st.astype(BF16), (((1,), (1,)), ((), ())),
                                        preferred_element_type=F32)
                st = st * jnp.exp(btot[c]) + lax.dot_general(vb[sl], k2, (((0,), (0,)), ((), ())),
                                                             preferred_element_type=F32)
                if readout:
                    o = o + of_ref[0, h, sl, :]
                    o = o * lax.rsqrt(jnp.mean(o * o, axis=-1, keepdims=True) + EPS) * ng_ref[...]
                    o_ref[0, h, sl, :] = (o * gs_ref[0, h, sl, :].astype(F32)).astype(o_ref.dtype)
                else:
                    o_ref[0, h, sl, :] = o
            st_ref[h] = st
        return carry

    lax.fori_loop(0, hg // grp, group, 0)


def _gla(cfg, qs, lf, v, reverse, n_ctx_blocks, of=None, gs=None, norm_g=None):
    b, nh, s, hd = qs.shape
    tb = cfg.tok_tile
    hg = nh
    grp = cfg.gla_group if hg % cfg.gla_group == 0 else 1
    nblk = s // tb
    readout = of is not None

    if reverse:
        def blk(j):
            return jnp.where(j < n_ctx_blocks, n_ctx_blocks - 1 - j, nblk - 1 - (j - n_ctx_blocks))
    else:
        def blk(j):
            return j

    spec = pl.BlockSpec((1, hg, tb, hd), lambda i, g, j: (i, g, blk(j), 0))
    in_specs = [spec, spec, spec]
    args = [qs, lf, v]
    if readout:
        in_specs += [spec, spec, pl.BlockSpec((1, hd), lambda i, g, j: (0, 0))]
        args += [of, gs, norm_g.reshape(1, hd)]
    out_dtype = BF16 if readout else F32
    vmem = 2 * hg * tb * hd * (2 + 4 + 2 + 4 + 2 + 4) + hg * hd * hd * 4 + 8 * tb * tb * 4 + 16 * tb * hd * 4
    return pl.pallas_call(
        functools.partial(_gla_kernel, cfg.gla_chunk, reverse, readout, cfg.fast_decay_limit),
        grid=(b, nh // hg, nblk),
        in_specs=in_specs,
        out_specs=spec,
        out_shape=jax.ShapeDtypeStruct((b, nh, s, hd), out_dtype),
        scratch_shapes=[pltpu.VMEM((hg, hd, hd), F32), pltpu.VMEM((grp, tb, hd), F32),
                        pltpu.VMEM((grp, tb, hd), F32),
                        pltpu.VMEM((grp, tb // cfg.gla_chunk, cfg.gla_chunk, cfg.gla_chunk), F32)],
        compiler_params=_cparams(("parallel", "parallel", "arbitrary"), vmem),
        name="hgrn_scan_bwd" if reverse else "hgrn_scan_fwd",
    )(*args)


def _out_proj_kernel(n_heads, y_ref, x_ref, w_ref, gt_ref, o_ref, wbf_ref):
    _cast_weight_once(w_ref, wbf_ref)
    y = jnp.concatenate([y_ref[0, h] for h in range(n_heads)], axis=-1)
    o_ref[0] = x_ref[0] + gt_ref[0] * jnp.dot(y, wbf_ref[...], preferred_element_type=F32)


def _out_proj(cfg, y_heads, x, w_out_all, layer, gate, ctx_tiles):
    b, t, d = x.shape
    nh, hd = y_heads.shape[1], y_heads.shape[3]
    tm = cfg.tok_tile
    return pl.pallas_call(
        functools.partial(_out_proj_kernel, nh),
        grid=(b, t // tm),
        in_specs=[pl.BlockSpec((1, nh, tm, hd), lambda i, j: (i, 0, j + ctx_tiles, 0)),
                  pl.BlockSpec((1, tm, d), lambda i, j: (i, j, 0)),
                  pl.BlockSpec((1, d, d), lambda i, j: (layer, 0, 0), pipeline_mode=pl.Buffered(1)),
                  pl.BlockSpec((1, 1, d), lambda i, j: (i, 0, 0))],
        out_specs=pl.BlockSpec((1, tm, d), lambda i, j: (i, j, 0)),
        out_shape=jax.ShapeDtypeStruct((b, t, d), F32),
        scratch_shapes=[pltpu.VMEM((d, d), BF16)],
        compiler_params=_cparams(("arbitrary", "arbitrary"), d * d * 6 + 8 * tm * d * 4),
        name="hgrn_out_proj",
    )(y_heads, x, w_out_all, gate)


def _hgrn_mixer(cfg, x, ctx, g, sh_all, sc_all, gate, w_in_all, norm_g, w_out_all, layer, lb):
    b, t, d = x.shape
    tm = cfg.tok_tile
    h_all = _norm_mod_seq(ctx, x, g, sh_all, sc_all, BF16, tm)
    log_lb = jnp.log(lb)
    log1m_lb = jnp.log1p(-lb)
    qs = _proj(cfg, h_all, w_in_all, layer, 0, "silu", BF16)
    gs = _proj(cfg, h_all, w_in_all, layer, 1, "silu", BF16)
    lf = _proj(cfg, h_all, w_in_all, layer, 2, "gate", F32, log_lb[0], log1m_lb[0])
    lr = _proj(cfg, h_all, w_in_all, layer, 3, "gate", F32, log_lb[1], log1m_lb[1])
    v = _proj(cfg, h_all, w_in_all, layer, 4, "none", BF16)
    n_ctx_blocks = ctx.shape[1] // tm
    of = _gla(cfg, qs, lf, v, False, n_ctx_blocks)
    y = _gla(cfg, qs, lr, v, True, n_ctx_blocks, of, gs, norm_g)
    return _out_proj(cfg, y, x, w_out_all, layer, gate, n_ctx_blocks)


def _forward(cfg, x, c, ctx, c_ctx, norm_mix, norm_ffn, w_ada, b_ada, pool_w, pool_scale, hgrn_w_in, hgrn_norm,
             hgrn_w_out, hgrn_lb_logits, router_w_group, router_b_group, router_w_expert, router_b_expert,
             moe_w_gate_up, moe_w_down, norm_final):
    b, t, d = x.shape
    depth = w_ada.shape[0]
    n_mixers = 2
    p_lb = jax.nn.softmax(hgrn_lb_logits.astype(F32), axis=0)
    lb_all = jnp.cumsum(p_lb, axis=0) - p_lb[0]

    rows = -(-(b + 1) // SUBLANES_V7X) * SUBLANES_V7X
    cond = jnp.zeros((rows, d), F32).at[:b].set(c).at[b].set(c_ctx)
    mods = _adaln(cond, w_ada, b_ada)

    ng, ne = cfg.n_groups, cfg.n_groups * cfg.experts_per_group
    for i in range(depth):
        ctx_needed = i < depth - 1
        j = i // n_mixers
        m = [mods[i, :, q * d:(q + 1) * d].reshape(rows, 1, d) for q in range(6)]
        sh_m, sc_m, gt_m, sh_f, sc_f, gt_f = m
        ctx_rows = lambda a: jnp.broadcast_to(a[b:b + 1], (b, 1, d))
        if i % n_mixers == 0:
            wp = pool_w[j].astype(BF16)
            x_new = _pool_mixer(cfg, x, norm_mix[i], sh_m, sc_m, gt_m, wp, pool_scale[j], True)
            if ctx_needed:
                ctx = _pool_mixer(cfg, ctx, norm_mix[i], ctx_rows(sh_m), ctx_rows(sc_m), ctx_rows(gt_m), wp,
                                  pool_scale[j], False)
            x = x_new
        else:
            x = _hgrn_mixer(cfg, x, ctx, norm_mix[i], sh_m, sc_m, gt_m, hgrn_w_in, hgrn_norm[j], hgrn_w_out, j,
                            lb_all[i])
        lanes = -(-(ng + ne) // LANES_V7X) * LANES_V7X
        w_r = jnp.zeros((d, lanes), F32).at[:, :ng].set(router_w_group[i]).at[:, ng:ng + ne].set(router_w_expert[i])
        b_r = jnp.zeros((1, lanes), F32).at[0, :ng].set(router_b_group[i]).at[0, ng:ng + ne].set(router_b_expert[i])
        slots, wts, n_all = _moe(cfg, x, ctx if ctx_needed else None, norm_ffn[i], sh_f, sc_f, w_r, b_r,
                                 moe_w_gate_up, moe_w_down, i)
        tm = _row_tile(cfg, t)
        last = i == depth - 1
        x_next = _combine(x, slots, wts, 0, n_all, gt_f, tm, norm_final if last else None)
        if ctx_needed:
            ctx = _combine(ctx, slots, wts, b * t, n_all, ctx_rows(gt_f), min(cfg.tok_tile, ctx.shape[1]))
        x = x_next
    return x


def kernel(x, c, ctx, c_ctx, norm_mix, norm_ffn, w_ada, b_ada, pool_w, pool_scale, hgrn_w_in, hgrn_norm, hgrn_w_out, hgrn_lb_logits, router_w_group, router_b_group, router_w_expert, router_b_expert, moe_w_gate_up, moe_w_down, norm_final):
    return _forward(Cfg(), x, c, ctx, c_ctx, norm_mix, norm_ffn, w_ada, b_ada, pool_w, pool_scale, hgrn_w_in,
                    hgrn_norm, hgrn_w_out, hgrn_lb_logits, router_w_group, router_b_group, router_w_expert,
                    router_b_expert, moe_w_gate_up, moe_w_down, norm_final)
```

```python
import functools
from typing import NamedTuple

import numpy as np
import jax
import jax.numpy as jnp
from jax import lax
from jax.experimental import pallas as pl
from jax.experimental.pallas import tpu as pltpu

F32 = jnp.float32
BF16 = jnp.bfloat16
HIGHEST = lax.Precision.HIGHEST
EPS = 1e-6

LANES_V7X = 128
SUBLANES_V7X = 8
VMEM_BYTES_V7X = 64 * 1024 * 1024


class Cfg(NamedTuple):
    grid_w: int = 64
    pool_windows: tuple = (2, 4, 8, 16)
    head_dim: int = LANES_V7X
    n_groups: int = 4
    experts_per_group: int = 8
    top_k: int = 2
    moe_rows: int = 256
    gla_chunk: int = 64
    gla_group: int = 8
    tok_tile: int = 256
    row_tile: int = 512
    fast_decay_limit: float = 120.0


def _vmem_limit(nbytes):
    return int(min(max(nbytes * 5 // 4, 16 * 1024 * 1024), VMEM_BYTES_V7X - 6 * 1024 * 1024))


def _cparams(sem, vmem_bytes):
    return pltpu.CompilerParams(dimension_semantics=sem, vmem_limit_bytes=_vmem_limit(vmem_bytes))


def _rms_mod(x, g, shift, scale):
    y = x * lax.rsqrt(jnp.mean(x * x, axis=-1, keepdims=True) + EPS) * g
    return y * (1.0 + scale) + shift


def _silu(x):
    return x * jax.nn.sigmoid(x)


def _store_slabs(ref, val):
    rows, d = val.shape
    n_sub = d // LANES_V7X
    for s in range(n_sub):
        ref[pl.ds(s, rows, stride=n_sub), :] = val[:, s * LANES_V7X:(s + 1) * LANES_V7X]


def _load_slabs(ref, rows):
    n_sub = ref.shape[0] // rows
    return jnp.concatenate([ref[pl.ds(s, rows, stride=n_sub), :] for s in range(n_sub)], axis=-1)


def _adaln_kernel(cond_ref, w_ref, b_ref, o_ref):
    s = _silu(cond_ref[...])
    o_ref[0] = jnp.dot(s, w_ref[0], precision=HIGHEST, preferred_element_type=F32) + b_ref[0]


def _adaln(cond, w_ada, b_ada):
    depth, d, n = w_ada.shape
    rows = cond.shape[0]
    tn = 1024 if n % 1024 == 0 else n
    return pl.pallas_call(
        _adaln_kernel,
        grid=(depth, n // tn),
        in_specs=[pl.BlockSpec((rows, d), lambda l, j: (0, 0)),
                  pl.BlockSpec((1, d, tn), lambda l, j: (l, 0, j)),
                  pl.BlockSpec((1, 1, tn), lambda l, j: (l, 0, j))],
        out_specs=pl.BlockSpec((1, rows, tn), lambda l, j: (l, 0, j)),
        out_shape=jax.ShapeDtypeStruct((depth, rows, n), F32),
        compiler_params=_cparams(("arbitrary", "arbitrary"), 2 * d * tn * 4 + 4 * rows * (d + tn) * 4),
        name="adaln",
    )(cond, w_ada, b_ada.reshape(depth, 1, n))


def _norm_mod_kernel(x_ref, g_ref, sh_ref, sc_ref, o_ref):
    o_ref[0] = _rms_mod(x_ref[0], g_ref[...], sh_ref[0], sc_ref[0]).astype(o_ref.dtype)


def _norm_mod(x, g, sh, sc, out_dtype, tm):
    b, t, d = x.shape
    return pl.pallas_call(
        _norm_mod_kernel,
        grid=(b, t // tm),
        in_specs=[pl.BlockSpec((1, tm, d), lambda i, j: (i, j, 0)),
                  pl.BlockSpec((1, d), lambda i, j: (0, 0)),
                  pl.BlockSpec((1, 1, d), lambda i, j: (i, 0, 0)),
                  pl.BlockSpec((1, 1, d), lambda i, j: (i, 0, 0))],
        out_specs=pl.BlockSpec((1, tm, d), lambda i, j: (i, j, 0)),
        out_shape=jax.ShapeDtypeStruct((b, t, d), out_dtype),
        compiler_params=_cparams(("parallel", "parallel"), 6 * tm * d * 4),
        name="norm_mod",
    )(x, g.reshape(1, d), sh, sc)


def _norm_mod_seq_kernel(n_ctx_tiles, c_ref, x_ref, g_ref, sh_ref, sc_ref, o_ref):
    j = pl.program_id(1)
    src = jnp.where(j < n_ctx_tiles, c_ref[0], x_ref[0])
    o_ref[0] = _rms_mod(src, g_ref[...], sh_ref[0], sc_ref[0]).astype(o_ref.dtype)


def _norm_mod_seq(ctx, x, g, sh_all, sc_all, out_dtype, tm):
    b, t, d = x.shape
    l = ctx.shape[1]
    nc, nx = l // tm, t // tm
    return pl.pallas_call(
        functools.partial(_norm_mod_seq_kernel, nc),
        grid=(b, nc + nx),
        in_specs=[pl.BlockSpec((1, tm, d), lambda i, j: (i, jnp.minimum(j, nc - 1), 0)),
                  pl.BlockSpec((1, tm, d), lambda i, j: (i, jnp.maximum(j - nc, 0), 0)),
                  pl.BlockSpec((1, d), lambda i, j: (0, 0)),
                  pl.BlockSpec((1, 1, d), lambda i, j: (jnp.where(j < nc, b, i), 0, 0)),
                  pl.BlockSpec((1, 1, d), lambda i, j: (jnp.where(j < nc, b, i), 0, 0))],
        out_specs=pl.BlockSpec((1, tm, d), lambda i, j: (i, j, 0)),
        out_shape=jax.ShapeDtypeStruct((b, l + t, d), out_dtype),
        compiler_params=_cparams(("parallel", "arbitrary"), 8 * tm * d * 4),
        name="norm_mod_seq",
    )(ctx, x, g.reshape(1, d), sh_all, sc_all)


def _win_matrix(length, k):
    pos = np.arange(length)
    lo = np.clip(pos - k // 2, 0, length - 1)
    hi = np.clip(pos + (k - k // 2 - 1), 0, length - 1)
    m = np.zeros((length, length), np.float32)
    for p in range(length):
        m[p, lo[p]:hi[p] + 1] = 1.0 / float(hi[p] - lo[p] + 1)
    return m, (1.0 / (hi - lo + 1)).astype(np.float32)


def _pool_tokens_kernel(k, w, slab, vertical, inv_ref, ph_ref, h_ref, o_ref, pad_ref):
    t, tc = h_ref.shape[1], h_ref.shape[2]
    if not vertical:
        for s in range(t // slab):
            hs = h_ref[0, s * slab:(s + 1) * slab, :]
            o_ref[0, s * slab:(s + 1) * slab, :] = (
                jnp.dot(ph_ref[...], hs, precision=HIGHEST, preferred_element_type=F32) - hs)
        return
    top = (k // 2) * w
    bot = (k - k // 2 - 1) * w
    pad_ref[0:top, :] = jnp.zeros((top, tc), F32)
    if bot:
        pad_ref[top + t:top + t + bot, :] = jnp.zeros((bot, tc), F32)
    for s in range(t // slab):
        hs = h_ref[0, s * slab:(s + 1) * slab, :]
        pad_ref[top + s * slab:top + (s + 1) * slab, :] = jnp.dot(
            ph_ref[...], hs, precision=HIGHEST, preferred_element_type=F32)

    def row(r, carry):
        base = pl.multiple_of(r * w, w)
        acc = pad_ref[pl.ds(base, w), :]
        for dr in range(1, k):
            acc = acc + pad_ref[pl.ds(pl.multiple_of(base + dr * w, w), w), :]
        o_ref[0, pl.ds(base, w), :] = acc * inv_ref[r] - h_ref[0, pl.ds(base, w), :]
        return carry

    lax.fori_loop(0, t // w, row, 0)


def _pool_tokens(h, group, gdim, k, grid_w, vertical):
    b, t, d = h.shape
    tc = min(gdim, 256)
    per = gdim // tc
    if vertical:
        slab = max(grid_w, min(256, t))
        mh, _ = _win_matrix(grid_w, k)
        ph = np.kron(np.eye(slab // grid_w, dtype=np.float32), mh)
        _, inv_v = _win_matrix(t // grid_w, k)
    else:
        slab = t
        ph, _ = _win_matrix(t, k)
        inv_v = np.ones((1,), np.float32)
    pad_rows = t + (k - 1) * grid_w if vertical else SUBLANES_V7X
    return pl.pallas_call(
        functools.partial(_pool_tokens_kernel, k, grid_w, slab, vertical),
        grid=(b, per),
        in_specs=[pl.BlockSpec(memory_space=pltpu.SMEM),
                  pl.BlockSpec((slab, slab), lambda i, j: (0, 0)),
                  pl.BlockSpec((1, t, tc), lambda i, j: (i, 0, group * per + j))],
        out_specs=pl.BlockSpec((1, t, tc), lambda i, j: (i, 0, j)),
        out_shape=jax.ShapeDtypeStruct((b, t, gdim), F32),
        scratch_shapes=[pltpu.VMEM((pad_rows, tc), F32)],
        compiler_params=_cparams(("parallel", "parallel"), (4 * t + pad_rows) * tc * 4 + 2 * slab * slab * 4),
        name=f"pool_tokens_k{k}",
    )(jnp.asarray(inv_v), jnp.asarray(ph), h)


def _pool_out_kernel(n_groups, *refs):
    d_refs = refs[:n_groups]
    x_ref, w_ref, ps_ref, gt_ref, o_ref = refs[n_groups:]
    ys = [jnp.dot(d_refs[j][0].astype(BF16), w_ref[j], preferred_element_type=F32) for j in range(n_groups)]
    y = jnp.concatenate(ys, axis=-1) * ps_ref[...]
    o_ref[0] = x_ref[0] + gt_ref[0] * y


def _pool_out(ds, x, w_pool_bf16, pool_scale, gate, tm):
    b, t, d = x.shape
    ng, gdim, _ = w_pool_bf16.shape
    return pl.pallas_call(
        functools.partial(_pool_out_kernel, ng),
        grid=(b, t // tm),
        in_specs=[pl.BlockSpec((1, tm, gdim), lambda i, j: (i, j, 0)) for _ in range(ng)] + [
            pl.BlockSpec((1, tm, d), lambda i, j: (i, j, 0)),
            pl.BlockSpec((ng, gdim, gdim), lambda i, j: (0, 0, 0)),
            pl.BlockSpec((1, d), lambda i, j: (0, 0)),
            pl.BlockSpec((1, 1, d), lambda i, j: (i, 0, 0))],
        out_specs=pl.BlockSpec((1, tm, d), lambda i, j: (i, j, 0)),
        out_shape=jax.ShapeDtypeStruct((b, t, d), F32),
        compiler_params=_cparams(("parallel", "parallel"), 8 * tm * d * 4 + 2 * ng * gdim * gdim * 2),
        name="pool_out",
    )(*ds, x, w_pool_bf16, pool_scale.reshape(1, d), gate)


def _row_tile(cfg, t):
    return cfg.row_tile if t % cfg.row_tile == 0 else min(cfg.tok_tile, t)


def _pool_mixer(cfg, x, g, sh, sc, gate, w_pool_bf16, pool_scale, on_grid):
    b, t, d = x.shape
    tm = _row_tile(cfg, t)
    h = _norm_mod(x, g, sh, sc, F32, tm)
    gdim = d // len(cfg.pool_windows)
    ds = [_pool_tokens(h, j, gdim, k, cfg.grid_w, on_grid) for j, k in enumerate(cfg.pool_windows)]
    return _pool_out(ds, x, w_pool_bf16, pool_scale, gate, tm)


def _route_tile(lg, ng, eg):
    lane = lax.broadcasted_iota(jnp.int32, lg.shape, 1)
    big = jnp.int32(lg.shape[1])
    shift = eg.bit_length() - 1

    def first_max(vals):
        m = jnp.max(vals, axis=-1, keepdims=True)
        return m, jnp.min(jnp.where(vals == m, lane, big), axis=-1, keepdims=True)

    is_g = lane < ng
    gmax, g_idx = first_max(jnp.where(is_g, lg, -jnp.inf))
    p_g = 1.0 / jnp.sum(jnp.where(is_g, jnp.exp(lg - gmax), 0.0), axis=-1, keepdims=True)
    in_grp = jnp.logical_and(jnp.logical_and(lane >= ng, lane < ng + ng * eg), ((lane - ng) >> shift) == g_idx)
    el = jnp.where(in_grp, lg, -jnp.inf)
    m1, i1 = first_max(el)
    m2, i2 = first_max(jnp.where(lane == i1, -jnp.inf, el))
    t = jnp.exp(m2 - m1)
    w1 = p_g / (1.0 + t)
    w2 = w1 * t
    out = jnp.where(lane == 0, w1, jnp.where(lane == 1, w2, 0.0))
    out = jnp.where(lane == 2, (i1 - ng).astype(F32), out)
    return jnp.where(lane == 3, (i2 - ng).astype(F32), out)


def _router_kernel(nx_tiles, has_ctx, ng, eg, *refs):
    if has_ctx:
        x_ref, c_ref, g_ref, sh_ref, sc_ref, wr_ref, br_ref, fx_ref, rt_ref = refs
        src = jnp.where(pl.program_id(0) < nx_tiles, x_ref[0], c_ref[0])
    else:
        x_ref, g_ref, sh_ref, sc_ref, wr_ref, br_ref, fx_ref, rt_ref = refs
        src = x_ref[0]
    h = _rms_mod(src, g_ref[...], sh_ref[0], sc_ref[0])
    _store_slabs(fx_ref, h)
    lg = jnp.dot(h, wr_ref[...], precision=HIGHEST, preferred_element_type=F32) + br_ref[...]
    rt_ref[...] = _route_tile(lg, ng, eg)


def _router(cfg, x, ctx, g, sh_all, sc_all, w_r, b_r, tm):
    b, t, d = x.shape
    per_b = t // tm
    nx = b * per_b
    has_ctx = ctx is not None
    nc = b * (ctx.shape[1] // tm) if has_ctx else 0
    per_c = (ctx.shape[1] // tm) if has_ctx else 1
    n_all = (nx + nc) * tm
    lanes = w_r.shape[1]

    def mod_row(i):
        return jnp.where(i < nx, jnp.minimum(i, nx - 1) // per_b, b) if has_ctx else i // per_b

    in_specs = [pl.BlockSpec((1, tm, d), lambda i: (jnp.minimum(i, nx - 1) // per_b, jnp.minimum(i, nx - 1) % per_b, 0))]
    args = [x]
    if has_ctx:
        in_specs.append(pl.BlockSpec(
            (1, tm, d), lambda i: (jnp.maximum(i - nx, 0) // per_c, jnp.maximum(i - nx, 0) % per_c, 0)))
        args.append(ctx)
    in_specs += [pl.BlockSpec((1, d), lambda i: (0, 0)),
                 pl.BlockSpec((1, 1, d), lambda i: (mod_row(i), 0, 0)),
                 pl.BlockSpec((1, 1, d), lambda i: (mod_row(i), 0, 0)),
                 pl.BlockSpec((d, lanes), lambda i: (0, 0)),
                 pl.BlockSpec((1, lanes), lambda i: (0, 0))]
    args += [g.reshape(1, d), sh_all, sc_all, w_r, b_r]
    return pl.pallas_call(
        functools.partial(_router_kernel, nx, has_ctx, cfg.n_groups, cfg.experts_per_group),
        grid=(nx + nc,),
        in_specs=in_specs,
        out_specs=[pl.BlockSpec((tm * (d // LANES_V7X), LANES_V7X), lambda i: (i, 0)),
                   pl.BlockSpec((tm, lanes), lambda i: (i, 0))],
        out_shape=[jax.ShapeDtypeStruct((n_all * (d // LANES_V7X), LANES_V7X), F32),
                   jax.ShapeDtypeStruct((n_all, lanes), F32)],
        compiler_params=_cparams(("arbitrary",), 8 * tm * d * 4 + 2 * d * lanes * 4),
        name="moe_router",
    )(*args)


def _expert_kernel(rows, n_all, col, be_ref, nu_ref, code_ref, fx_hbm, wgu_ref, wdn_ref, out_hbm,
                   xbuf, ybuf, gsem, ssem):
    i = pl.program_id(0)
    n_used = nu_ref[0]
    slot = i % 2
    other = 1 - slot
    n_slots = n_all * 2
    last_blk = code_ref.shape[0] // rows - 1
    f, d = wdn_ref.shape[2], wdn_ref.shape[3]
    n_sub = d // LANES_V7X

    def slab(ref, row, count=1):
        return ref.at[pl.ds(pl.multiple_of(row * n_sub, n_sub), count * n_sub), :]

    def gather_row(blk, sl, r):
        v = code_ref[blk * rows + r]
        src = jnp.where(v >= n_slots, 0, v >> 1)
        return pltpu.make_async_copy(slab(fx_hbm, src), slab(xbuf.at[sl], r), gsem.at[sl])

    def scatter_row(blk, sl, r, none):
        v = jnp.where(none, n_slots, code_ref[blk * rows + r])
        dst = jnp.where(v >= n_slots, n_slots + sl * rows + r, (v & 1) * n_all + (v >> 1))
        return pltpu.make_async_copy(slab(ybuf.at[sl], r), slab(out_hbm, dst), ssem.at[sl])

    def wait_gather(sl):
        pltpu.make_async_copy(slab(fx_hbm, 0, rows), xbuf.at[sl], gsem.at[sl]).wait()

    def wait_scatter(sl):
        pltpu.make_async_copy(ybuf.at[sl], slab(out_hbm, 0, rows), ssem.at[sl]).wait()

    @pl.when(i == 0)
    def _():
        ybuf[...] = jnp.zeros(ybuf.shape, F32)
        for sl in range(2):
            pltpu.make_async_copy(ybuf.at[sl], slab(out_hbm, n_slots + sl * rows, rows), ssem.at[sl]).start()
        wait_scatter(1)

        @pl.when(n_used > 0)
        def _():
            def body(r, c):
                gather_row(0, 0, r).start()
                return c
            lax.fori_loop(0, rows, body, 0)

        @pl.when(n_used == 0)
        def _():
            wait_scatter(0)

    @pl.when(i < n_used)
    def _():
        wait_gather(slot)
        nxt = jnp.minimum(i + 1, last_blk)
        prev = jnp.maximum(i - 1, 0)
        no_prev = i == 0
        xb = _load_slabs(xbuf.at[slot], rows).astype(BF16)
        n1 = f // col
        per1 = rows // n1
        acts = []
        for j in range(n1):
            g = jnp.dot(xb, wgu_ref[0, 0, :, j * col:(j + 1) * col].astype(BF16), preferred_element_type=F32)
            u = jnp.dot(xb, wgu_ref[0, 0, :, f + j * col:f + (j + 1) * col].astype(BF16),
                        preferred_element_type=F32)
            acts.append((_silu(g) * u).astype(BF16))
            for r in range(j * per1, (j + 1) * per1):
                gather_row(nxt, other, r).start()
        act = jnp.concatenate(acts, axis=-1)
        wait_scatter(slot)
        yslot = ybuf.at[slot]
        per_col = col // LANES_V7X
        n2 = d // col
        per2 = rows // n2
        for j in range(n2):
            y = jnp.dot(act, wdn_ref[0, 0, :, j * col:(j + 1) * col].astype(BF16), preferred_element_type=F32)
            for q in range(per_col):
                yslot[pl.ds(j * per_col + q, rows, stride=n_sub), :] = y[:, q * LANES_V7X:(q + 1) * LANES_V7X]
            for r in range(j * per2, (j + 1) * per2):
                scatter_row(prev, other, r, no_prev).start()

    @pl.when(jnp.logical_and(i == n_used, i >= 1))
    def _():
        wait_gather(slot)

        def body(r, c):
            scatter_row(i - 1, other, r, False).start()
            return c
        lax.fori_loop(0, rows, body, 0)
        wait_scatter(other)
        wait_scatter(slot)


def _experts(cfg, fx, block_e, n_used, row_code, w_gu, w_dn, layer):
    d, f2 = w_gu.shape[2], w_gu.shape[3]
    n_sub = d // LANES_V7X
    n_all = fx.shape[0] // n_sub
    f = f2 // 2
    rows = cfg.moe_rows
    col = min(256, f, d)
    n_steps = block_e.shape[0]
    grid_spec = pltpu.PrefetchScalarGridSpec(
        num_scalar_prefetch=3,
        grid=(n_steps,),
        in_specs=[pl.BlockSpec(memory_space=pl.ANY),
                  pl.BlockSpec((1, 1, d, f2), lambda i, be, nu, cd: (layer, be[i], 0, 0),
                               pipeline_mode=pl.Buffered(1)),
                  pl.BlockSpec((1, 1, f, d), lambda i, be, nu, cd: (layer, be[i], 0, 0))],
        out_specs=pl.BlockSpec(memory_space=pl.ANY),
        scratch_shapes=[pltpu.VMEM((2, rows * n_sub, LANES_V7X), F32), pltpu.VMEM((2, rows * n_sub, LANES_V7X), F32),
                        pltpu.SemaphoreType.DMA((2,)), pltpu.SemaphoreType.DMA((2,))])
    vmem = (d * f2 + 2 * f * d) * 4 + 4 * rows * d * 4 + 3 * rows * f2 * 4 + 4 * d * col * 4
    return pl.pallas_call(
        functools.partial(_expert_kernel, rows, n_all, col),
        grid_spec=grid_spec,
        out_shape=jax.ShapeDtypeStruct(((cfg.top_k * n_all + 2 * rows) * n_sub, LANES_V7X), F32),
        compiler_params=_cparams(("arbitrary",), vmem),
        name="moe_experts",
    )(block_e, n_used, row_code, fx, w_gu, w_dn)


def _combine_kernel(final, *refs):
    if final:
        x_ref, a_ref, b_ref, w_ref, gt_ref, gf_ref, o_ref = refs
    else:
        x_ref, a_ref, b_ref, w_ref, gt_ref, o_ref = refs
    w = w_ref[...]
    tm = x_ref.shape[1]
    y = x_ref[0] + gt_ref[0] * (w[:, 0:1] * _load_slabs(a_ref, tm) + w[:, 1:2] * _load_slabs(b_ref, tm))
    if final:
        y = y * lax.rsqrt(jnp.mean(y * y, axis=-1, keepdims=True) + EPS) * gf_ref[...]
    o_ref[0] = y


def _combine(x, slots, wts, tok_off, n_all, gate, tm, norm_final=None):
    b, t, d = x.shape
    per_b = t // tm
    off0 = tok_off // tm
    off1 = (n_all + tok_off) // tm
    final = norm_final is not None
    in_specs = [pl.BlockSpec((1, tm, d), lambda i, j: (i, j, 0)),
                pl.BlockSpec((tm * (d // LANES_V7X), LANES_V7X), lambda i, j: (off0 + i * per_b + j, 0)),
                pl.BlockSpec((tm * (d // LANES_V7X), LANES_V7X), lambda i, j: (off1 + i * per_b + j, 0)),
                pl.BlockSpec((tm, wts.shape[1]), lambda i, j: (off0 + i * per_b + j, 0)),
                pl.BlockSpec((1, 1, d), lambda i, j: (i, 0, 0))]
    args = [x, slots, slots, wts, gate]
    if final:
        in_specs.append(pl.BlockSpec((1, d), lambda i, j: (0, 0)))
        args.append(norm_final.reshape(1, d))
    return pl.pallas_call(
        functools.partial(_combine_kernel, final),
        grid=(b, per_b),
        in_specs=in_specs,
        out_specs=pl.BlockSpec((1, tm, d), lambda i, j: (i, j, 0)),
        out_shape=jax.ShapeDtypeStruct((b, t, d), F32),
        compiler_params=_cparams(("parallel", "parallel"), 10 * tm * d * 4),
        name="moe_combine",
    )(*args)


def _route(cfg, routed):
    n_all = routed.shape[0]
    ng, eg, k = cfg.n_groups, cfg.experts_per_group, cfg.top_k
    assert k == 2, "row codes pack (token, choice) as 2 * token + choice"
    n_exp = ng * eg
    rows = cfg.moe_rows
    wts = routed[:, 0:k]
    eid = routed[:, k:2 * k].astype(jnp.int32)

    s = n_all * k
    e_flat = eid.reshape(-1)
    counts = jnp.sum((e_flat[:, None] == jnp.arange(n_exp, dtype=jnp.int32)[None, :]).astype(jnp.int32), axis=0)
    order = jnp.argsort(e_flat, stable=True).astype(jnp.int32)
    starts = jnp.cumsum(counts) - counts
    padded = ((counts + rows - 1) // rows) * rows
    pends = jnp.cumsum(padded)
    pstarts = pends - padded
    n_blocks = -(-(s + n_exp * (rows - 1)) // rows)
    blk_start = jnp.arange(n_blocks + 1, dtype=jnp.int32) * rows
    block_e = jnp.minimum(jnp.sum((pends[None, :] <= blk_start[:, None]).astype(jnp.int32), axis=1), n_exp - 1)
    pos = jnp.arange(n_blocks * rows, dtype=jnp.int32)
    e_row = jnp.repeat(block_e[:n_blocks], rows)
    rank = pos - pstarts[e_row]
    routed = jnp.logical_and(rank < counts[e_row], pos < pends[-1])
    row_code = jnp.where(routed, order[jnp.clip(starts[e_row] + rank, 0, s - 1)], s).astype(jnp.int32)
    n_used = (pends[-1] // rows).astype(jnp.int32).reshape(1)
    return (block_e, n_used, row_code), wts


def _moe(cfg, x, ctx, g, sh_all, sc_all, w_r, b_r, w_gu, w_dn, layer):
    b, t, d = x.shape
    tm = _row_tile(cfg, t) if ctx is None else min(cfg.tok_tile, t, ctx.shape[1])
    fx, routed = _router(cfg, x, ctx, g, sh_all, sc_all, w_r, b_r, tm)
    n_all = routed.shape[0]
    plan, wts = _route(cfg, routed)
    slots = _experts(cfg, fx, *plan, w_gu, w_dn, layer)
    return slots, wts, n_all


def _cast_weight_once(w_ref, wbf_ref):
    @pl.when(jnp.logical_and(pl.program_id(0) == 0, pl.program_id(1) == 0))
    def _():
        k = wbf_ref.shape[0]
        step = min(k, 256)

        def body(c, carry):
            r0 = pl.multiple_of(c * step, step)
            wbf_ref[pl.ds(r0, step), :] = w_ref[0, pl.ds(r0, step), :].astype(BF16)
            return carry
        lax.fori_loop(0, k // step, body, 0)


def _proj_kernel(kind, n_heads, hd, *refs):
    if kind == "gate":
        h_ref, w_ref, la_ref, lc_ref, o_ref, wbf_ref = refs
    else:
        h_ref, w_ref, o_ref, wbf_ref = refs
    _cast_weight_once(w_ref, wbf_ref)
    z = jnp.dot(h_ref[0], wbf_ref[...], preferred_element_type=F32)
    if kind == "silu":
        z = _silu(z)
    elif kind == "gate":
        ls = jnp.minimum(z, 0.0) - jnp.log(1.0 + jnp.exp(-jnp.abs(z)))
        c = lc_ref[...] + ls
        a = jnp.broadcast_to(la_ref[...], c.shape)
        m = jnp.maximum(a, c)
        z = m + jnp.log(1.0 + jnp.exp(-jnp.abs(a - c)))
    for h in range(n_heads):
        o_ref[0, h] = z[:, h * hd:(h + 1) * hd].astype(o_ref.dtype)


def _proj(cfg, h_all, w_all, layer, section, kind, out_dtype, log_lb=None, log1m_lb=None):
    b, s, d = h_all.shape
    n = d
    hd = cfg.head_dim
    nh = n // hd
    tm = cfg.tok_tile
    in_specs = [pl.BlockSpec((1, tm, d), lambda i, j: (i, j, 0)),
                pl.BlockSpec((1, d, n), lambda i, j: (layer, 0, section), pipeline_mode=pl.Buffered(1))]
    args = [h_all, w_all]
    if kind == "gate":
        in_specs += [pl.BlockSpec((1, n), lambda i, j: (0, 0)), pl.BlockSpec((1, n), lambda i, j: (0, 0))]
        args += [log_lb.reshape(1, n), log1m_lb.reshape(1, n)]
    return pl.pallas_call(
        functools.partial(_proj_kernel, kind, nh, hd),
        grid=(b, s // tm),
        in_specs=in_specs,
        out_specs=pl.BlockSpec((1, nh, tm, hd), lambda i, j: (i, 0, j, 0)),
        out_shape=jax.ShapeDtypeStruct((b, nh, s, hd), out_dtype),
        scratch_shapes=[pltpu.VMEM((d, n), BF16)],
        compiler_params=_cparams(("arbitrary", "arbitrary"), d * n * 6 + 2 * tm * d * 2 + 6 * tm * n * 4),
        name=f"hgrn_proj_{kind}",
    )(*args)


def _gla_kernel(chunk, reverse, readout, limit, *refs):
    if readout:
        q_ref, lf_ref, v_ref, of_ref, gs_ref, ng_ref, o_ref, st_ref, b_ref, q32_ref, at_ref = refs
    else:
        q_ref, lf_ref, v_ref, o_ref, st_ref, b_ref, q32_ref, at_ref = refs
    hg, tb, hd = q_ref.shape[1], q_ref.shape[2], q_ref.shape[3]
    n_chunks = tb // chunk
    shift = chunk.bit_length() - 1

    @pl.when(pl.program_id(2) == 0)
    def _():
        st_ref[...] = jnp.zeros(st_ref.shape, F32)

    rr = lax.broadcasted_iota(jnp.int32, (tb, tb), 0)
    cc = lax.broadcasted_iota(jnp.int32, (tb, tb), 1)
    same = (rr >> shift) == (cc >> shift)
    tri = jnp.where(jnp.logical_and(same, (cc >= rr) if reverse else (cc <= rr)), 1.0, 0.0).astype(BF16)
    ar = lax.broadcasted_iota(jnp.int32, (chunk, chunk), 0)
    ac = lax.broadcasted_iota(jnp.int32, (chunk, chunk), 1)
    at_mask = (ar >= ac) if reverse else (ar <= ac)
    row_id = lax.broadcasted_iota(jnp.int32, (chunk, hd), 0)
    order = list(range(n_chunks))[::-1] if reverse else list(range(n_chunks))
    edge = 0 if reverse else chunk - 1
    grp = b_ref.shape[0]

    def group(hp, carry):
        heads = [hp * grp + u for u in range(grp)]
        lf_all = jnp.concatenate([lf_ref[0, h] for h in heads], axis=-1)
        hi = lf_all.astype(BF16)
        r1 = lf_all - hi.astype(F32)
        mid = r1.astype(BF16)
        lo = (r1 - mid.astype(F32)).astype(BF16)
        b_all = (jnp.dot(tri, hi, preferred_element_type=F32) + jnp.dot(tri, mid, preferred_element_type=F32)
                 + jnp.dot(tri, lo, preferred_element_type=F32))
        bs, kks, qfs, btots = [], [], [], []
        worst = None
        for u, h in enumerate(heads):
            b = b_all[:, u * hd:(u + 1) * hd]
            b_ref[u] = b
            btot = [b[c * chunk + edge:c * chunk + edge + 1, :] for c in range(n_chunks)]
            for bt in btot:
                worst = bt if worst is None else jnp.minimum(worst, bt)
            bs.append(b)
            btots.append(btot)
            kks.append(1.0 - jnp.exp(lf_all[:, u * hd:(u + 1) * hd]))
            qfs.append(q_ref[0, h].astype(F32))
        fast = jnp.min(worst) >= -limit

        @pl.when(fast)
        def _():
            for u in range(grp):
                for c in range(n_chunks):
                    sl = slice(c * chunk, (c + 1) * chunk)
                    d = bs[u][sl] - 0.5 * btots[u][c]
                    qh = (qfs[u][sl] * jnp.exp(d)).astype(BF16)
                    kh = (kks[u][sl] * jnp.exp(-d)).astype(BF16)
                    at = lax.dot_general(kh, qh, (((1,), (1,)), ((), ())), preferred_element_type=F32)
                    at_ref[u, c] = jnp.where(at_mask, at, 0.0)

        @pl.when(jnp.logical_not(fast))
        def _():
            for u in range(grp):
                q32_ref[u] = qfs[u]
                for c in range(n_chunks):
                    c0 = c * chunk
                    bc = bs[u][c0:c0 + chunk]
                    kc = kks[u][c0:c0 + chunk]

                    def tstep(t, at):
                        bt = b_ref[u, pl.ds(c0 + t, 1), :]
                        qrow = q32_ref[u, pl.ds(c0 + t, 1), :]
                        valid = (row_id >= t) if reverse else (row_id <= t)
                        e = jnp.exp(jnp.where(valid, bt - bc, -jnp.inf))
                        col = jnp.sum(e * kc * qrow, axis=-1, keepdims=True)
                        return at + col * jnp.where(ac == t, 1.0, 0.0)

                    at_ref[u, c] = lax.fori_loop(0, chunk, tstep, jnp.zeros((chunk, chunk), F32))

        for u, h in enumerate(heads):
            b, kk, btot = bs[u], kks[u], btots[u]
            qb = (qfs[u] * jnp.exp(b)).astype(BF16)
            vb = v_ref[0, h]
            st = st_ref[h]
            for c in order:
                sl = slice(c * chunk, (c + 1) * chunk)
                k2 = (kk[sl] * jnp.exp(btot[c] - b[sl])).astype(BF16)
                o = lax.dot_general(at_ref[u, c].astype(BF16), vb[sl], (((0,), (0,)), ((), ())),
                                    preferred_element_type=F32)
                o = o + lax.dot_general(qb[sl], st.astype(BF16), (((1,), (1,)), ((), ())),
                                        preferred_element_type=F32)
                st = st * jnp.exp(btot[c]) + lax.dot_general(vb[sl], k2, (((0,), (0,)), ((), ())),
                                                             preferred_element_type=F32)
                if readout:
                    o = o + of_ref[0, h, sl, :]
                    o = o * lax.rsqrt(jnp.mean(o * o, axis=-1, keepdims=True) + EPS) * ng_ref[...]
                    o_ref[0, h, sl, :] = (o * gs_ref[0, h, sl, :].astype(F32)).astype(o_ref.dtype)
                else:
                    o_ref[0, h, sl, :] = o
            st_ref[h] = st
        return carry

    lax.fori_loop(0, hg // grp, group, 0)


def _gla(cfg, qs, lf, v, reverse, n_ctx_blocks, of=None, gs=None, norm_g=None):
    b, nh, s, hd = qs.shape
    tb = cfg.tok_tile
    hg = nh
    grp = cfg.gla_group if hg % cfg.gla_group == 0 else 1
    nblk = s // tb
    readout = of is not None

    if reverse:
        def blk(j):
            return jnp.where(j < n_ctx_blocks, n_ctx_blocks - 1 - j, nblk - 1 - (j - n_ctx_blocks))
    else:
        def blk(j):
            return j

    spec = pl.BlockSpec((1, hg, tb, hd), lambda i, g, j: (i, g, blk(j), 0))
    in_specs = [spec, spec, spec]
    args = [qs, lf, v]
    if readout:
        in_specs += [spec, spec, pl.BlockSpec((1, hd), lambda i, g, j: (0, 0))]
        args += [of, gs, norm_g.reshape(1, hd)]
    out_dtype = BF16 if readout else F32
    vmem = 2 * hg * tb * hd * (2 + 4 + 2 + 4 + 2 + 4) + hg * hd * hd * 4 + 8 * tb * tb * 4 + 16 * tb * hd * 4
    return pl.pallas_call(
        functools.partial(_gla_kernel, cfg.gla_chunk, reverse, readout, cfg.fast_decay_limit),
        grid=(b, nh // hg, nblk),
        in_specs=in_specs,
        out_specs=spec,
        out_shape=jax.ShapeDtypeStruct((b, nh, s, hd), out_dtype),
        scratch_shapes=[pltpu.VMEM((hg, hd, hd), F32), pltpu.VMEM((grp, tb, hd), F32),
                        pltpu.VMEM((grp, tb, hd), F32),
                        pltpu.VMEM((grp, tb // cfg.gla_chunk, cfg.gla_chunk, cfg.gla_chunk), F32)],
        compiler_params=_cparams(("parallel", "parallel", "arbitrary"), vmem),
        name="hgrn_scan_bwd" if reverse else "hgrn_scan_fwd",
    )(*args)


def _out_proj_kernel(n_heads, y_ref, x_ref, w_ref, gt_ref, o_ref, wbf_ref):
    _cast_weight_once(w_ref, wbf_ref)
    y = jnp.concatenate([y_ref[0, h] for h in range(n_heads)], axis=-1)
    o_ref[0] = x_ref[0] + gt_ref[0] * jnp.dot(y, wbf_ref[...], preferred_element_type=F32)


def _out_proj(cfg, y_heads, x, w_out_all, layer, gate, ctx_tiles):
    b, t, d = x.shape
    nh, hd = y_heads.shape[1], y_heads.shape[3]
    tm = cfg.tok_tile
    return pl.pallas_call(
        functools.partial(_out_proj_kernel, nh),
        grid=(b, t // tm),
        in_specs=[pl.BlockSpec((1, nh, tm, hd), lambda i, j: (i, 0, j + ctx_tiles, 0)),
                  pl.BlockSpec((1, tm, d), lambda i, j: (i, j, 0)),
                  pl.BlockSpec((1, d, d), lambda i, j: (layer, 0, 0), pipeline_mode=pl.Buffered(1)),
                  pl.BlockSpec((1, 1, d), lambda i, j: (i, 0, 0))],
        out_specs=pl.BlockSpec((1, tm, d), lambda i, j: (i, j, 0)),
        out_shape=jax.ShapeDtypeStruct((b, t, d), F32),
        scratch_shapes=[pltpu.VMEM((d, d), BF16)],
        compiler_params=_cparams(("arbitrary", "arbitrary"), d * d * 6 + 8 * tm * d * 4),
        name="hgrn_out_proj",
    )(y_heads, x, w_out_all, gate)


def _hgrn_mixer(cfg, x, ctx, g, sh_all, sc_all, gate, w_in_all, norm_g, w_out_all, layer, lb):
    b, t, d = x.shape
    tm = cfg.tok_tile
    h_all = _norm_mod_seq(ctx, x, g, sh_all, sc_all, BF16, tm)
    log_lb = jnp.log(lb)
    log1m_lb = jnp.log1p(-lb)
    qs = _proj(cfg, h_all, w_in_all, layer, 0, "silu", BF16)
    gs = _proj(cfg, h_all, w_in_all, layer, 1, "silu", BF16)
    lf = _proj(cfg, h_all, w_in_all, layer, 2, "gate", F32, log_lb[0], log1m_lb[0])
    lr = _proj(cfg, h_all, w_in_all, layer, 3, "gate", F32, log_lb[1], log1m_lb[1])
    v = _proj(cfg, h_all, w_in_all, layer, 4, "none", BF16)
    n_ctx_blocks = ctx.shape[1] // tm
    of = _gla(cfg, qs, lf, v, False, n_ctx_blocks)
    y = _gla(cfg, qs, lr, v, True, n_ctx_blocks, of, gs, norm_g)
    return _out_proj(cfg, y, x, w_out_all, layer, gate, n_ctx_blocks)


def _forward(cfg, x, c, ctx, c_ctx, norm_mix, norm_ffn, w_ada, b_ada, pool_w, pool_scale, hgrn_w_in, hgrn_norm,
             hgrn_w_out, hgrn_lb_logits, router_w_group, router_b_group, router_w_expert, router_b_expert,
             moe_w_gate_up, moe_w_down, norm_final):
    b, t, d = x.shape
    depth = w_ada.shape[0]
    n_mixers = 2
    p_lb = jax.nn.softmax(hgrn_lb_logits.astype(F32), axis=0)
    lb_all = jnp.cumsum(p_lb, axis=0) - p_lb[0]

    rows = -(-(b + 1) // SUBLANES_V7X) * SUBLANES_V7X
    cond = jnp.zeros((rows, d), F32).at[:b].set(c).at[b].set(c_ctx)
    mods = _adaln(cond, w_ada, b_ada)

    ng, ne = cfg.n_groups, cfg.n_groups * cfg.experts_per_group
    for i in range(depth):
        ctx_needed = i < depth - 1
        j = i // n_mixers
        m = [mods[i, :, q * d:(q + 1) * d].reshape(rows, 1, d) for q in range(6)]
        sh_m, sc_m, gt_m, sh_f, sc_f, gt_f = m
        ctx_rows = lambda a: jnp.broadcast_to(a[b:b + 1], (b, 1, d))
        if i % n_mixers == 0:
            wp = pool_w[j].astype(BF16)
            x_new = _pool_mixer(cfg, x, norm_mix[i], sh_m, sc_m, gt_m, wp, pool_scale[j], True)
            if ctx_needed:
                ctx = _pool_mixer(cfg, ctx, norm_mix[i], ctx_rows(sh_m), ctx_rows(sc_m), ctx_rows(gt_m), wp,
                                  pool_scale[j], False)
            x = x_new
        else:
            x = _hgrn_mixer(cfg, x, ctx, norm_mix[i], sh_m, sc_m, gt_m, hgrn_w_in, hgrn_norm[j], hgrn_w_out, j,
                            lb_all[i])
        lanes = -(-(ng + ne) // LANES_V7X) * LANES_V7X
        w_r = jnp.zeros((d, lanes), F32).at[:, :ng].set(router_w_group[i]).at[:, ng:ng + ne].set(router_w_expert[i])
        b_r = jnp.zeros((1, lanes), F32).at[0, :ng].set(router_b_group[i]).at[0, ng:ng + ne].set(router_b_expert[i])
        slots, wts, n_all = _moe(cfg, x, ctx if ctx_needed else None, norm_ffn[i], sh_f, sc_f, w_r, b_r,
                                 moe_w_gate_up, moe_w_down, i)
        tm = _row_tile(cfg, t)
        last = i == depth - 1
        x_next = _combine(x, slots, wts, 0, n_all, gt_f, tm, norm_final if last else None)
        if ctx_needed:
            ctx = _combine(ctx, slots, wts, b * t, n_all, ctx_rows(gt_f), min(cfg.tok_tile, ctx.shape[1]))
        x = x_next
    return x


def kernel(x, c, ctx, c_ctx, norm_mix, norm_ffn, w_ada, b_ada, pool_w, pool_scale, hgrn_w_in, hgrn_norm, hgrn_w_out, hgrn_lb_logits, router_w_group, router_b_group, router_w_expert, router_b_expert, moe_w_gate_up, moe_w_down, norm_final):
    return _forward(Cfg(), x, c, ctx, c_ctx, norm_mix, norm_ffn, w_ada, b_ada, pool_w, pool_scale, hgrn_w_in,
                    hgrn_norm, hgrn_w_out, hgrn_lb_logits, router_w_group, router_b_group, router_w_expert,
                    router_b_expert, moe_w_gate_up, moe_w_down, norm_final)
```

```python
import functools
from typing import NamedTuple

import numpy as np
import jax
import jax.numpy as jnp
from jax import lax
from jax.experimental import pallas as pl
from jax.experimental.pallas import tpu as pltpu

F32 = jnp.float32
BF16 = jnp.bfloat16
HIGHEST = lax.Precision.HIGHEST
EPS = 1e-6

LANES_V7X = 128
SUBLANES_V7X = 8
VMEM_BYTES_V7X = 64 * 1024 * 1024


class Cfg(NamedTuple):
    grid_w: int = 64
    pool_windows: tuple = (2, 4, 8, 16)
    head_dim: int = LANES_V7X
    n_groups: int = 4
    experts_per_group: int = 8
    top_k: int = 2
    moe_rows: int = 256
    gla_chunk: int = 64
    gla_group: int = 8
    tok_tile: int = 256
    row_tile: int = 512
    fast_decay_limit: float = 120.0


def _vmem_limit(nbytes):
    return int(min(max(nbytes * 5 // 4, 16 * 1024 * 1024), VMEM_BYTES_V7X - 6 * 1024 * 1024))


def _cparams(sem, vmem_bytes):
    return pltpu.CompilerParams(dimension_semantics=sem, vmem_limit_bytes=_vmem_limit(vmem_bytes))


def _rms_mod(x, g, shift, scale):
    y = x * lax.rsqrt(jnp.mean(x * x, axis=-1, keepdims=True) + EPS) * g
    return y * (1.0 + scale) + shift


def _silu(x):
    return x * jax.nn.sigmoid(x)


def _store_slabs(ref, val):
    rows, d = val.shape
    n_sub = d // LANES_V7X
    for s in range(n_sub):
        ref[pl.ds(s, rows, stride=n_sub), :] = val[:, s * LANES_V7X:(s + 1) * LANES_V7X]


def _load_slabs(ref, rows):
    n_sub = ref.shape[0] // rows
    return jnp.concatenate([ref[pl.ds(s, rows, stride=n_sub), :] for s in range(n_sub)], axis=-1)


def _adaln_kernel(cond_ref, w_ref, b_ref, o_ref):
    s = _silu(cond_ref[...])
    o_ref[0] = jnp.dot(s, w_ref[0], precision=HIGHEST, preferred_element_type=F32) + b_ref[0]


def _adaln(cond, w_ada, b_ada):
    depth, d, n = w_ada.shape
    rows = cond.shape[0]
    tn = 1024 if n % 1024 == 0 else n
    return pl.pallas_call(
        _adaln_kernel,
        grid=(depth, n // tn),
        in_specs=[pl.BlockSpec((rows, d), lambda l, j: (0, 0)),
                  pl.BlockSpec((1, d, tn), lambda l, j: (l, 0, j)),
                  pl.BlockSpec((1, 1, tn), lambda l, j: (l, 0, j))],
        out_specs=pl.BlockSpec((1, rows, tn), lambda l, j: (l, 0, j)),
        out_shape=jax.ShapeDtypeStruct((depth, rows, n), F32),
        compiler_params=_cparams(("arbitrary", "arbitrary"), 2 * d * tn * 4 + 4 * rows * (d + tn) * 4),
        name="adaln",
    )(cond, w_ada, b_ada.reshape(depth, 1, n))


def _norm_mod_kernel(x_ref, g_ref, sh_ref, sc_ref, o_ref):
    o_ref[0] = _rms_mod(x_ref[0], g_ref[...], sh_ref[0], sc_ref[0]).astype(o_ref.dtype)


def _norm_mod(x, g, sh, sc, out_dtype, tm):
    b, t, d = x.shape
    return pl.pallas_call(
        _norm_mod_kernel,
        grid=(b, t // tm),
        in_specs=[pl.BlockSpec((1, tm, d), lambda i, j: (i, j, 0)),
                  pl.BlockSpec((1, d), lambda i, j: (0, 0)),
                  pl.BlockSpec((1, 1, d), lambda i, j: (i, 0, 0)),
                  pl.BlockSpec((1, 1, d), lambda i, j: (i, 0, 0))],
        out_specs=pl.BlockSpec((1, tm, d), lambda i, j: (i, j, 0)),
        out_shape=jax.ShapeDtypeStruct((b, t, d), out_dtype),
        compiler_params=_cparams(("parallel", "parallel"), 6 * tm * d * 4),
        name="norm_mod",
    )(x, g.reshape(1, d), sh, sc)


def _norm_mod_seq_kernel(n_ctx_tiles, c_ref, x_ref, g_ref, sh_ref, sc_ref, o_ref):
    j = pl.program_id(1)
    src = jnp.where(j < n_ctx_tiles, c_ref[0], x_ref[0])
    o_ref[0] = _rms_mod(src, g_ref[...], sh_ref[0], sc_ref[0]).astype(o_ref.dtype)


def _norm_mod_seq(ctx, x, g, sh_all, sc_all, out_dtype, tm):
    b, t, d = x.shape
    l = ctx.shape[1]
    nc, nx = l // tm, t // tm
    return pl.pallas_call(
        functools.partial(_norm_mod_seq_kernel, nc),
        grid=(b, nc + nx),
        in_specs=[pl.BlockSpec((1, tm, d), lambda i, j: (i, jnp.minimum(j, nc - 1), 0)),
                  pl.BlockSpec((1, tm, d), lambda i, j: (i, jnp.maximum(j - nc, 0), 0)),
                  pl.BlockSpec((1, d), lambda i, j: (0, 0)),
                  pl.BlockSpec((1, 1, d), lambda i, j: (jnp.where(j < nc, b, i), 0, 0)),
                  pl.BlockSpec((1, 1, d), lambda i, j: (jnp.where(j < nc, b, i), 0, 0))],
        out_specs=pl.BlockSpec((1, tm, d), lambda i, j: (i, j, 0)),
        out_shape=jax.ShapeDtypeStruct((b, l + t, d), out_dtype),
        compiler_params=_cparams(("parallel", "arbitrary"), 8 * tm * d * 4),
        name="norm_mod_seq",
    )(ctx, x, g.reshape(1, d), sh_all, sc_all)


def _win_matrix(length, k):
    pos = np.arange(length)
    lo = np.clip(pos - k // 2, 0, length - 1)
    hi = np.clip(pos + (k - k // 2 - 1), 0, length - 1)
    m = np.zeros((length, length), np.float32)
    for p in range(length):
        m[p, lo[p]:hi[p] + 1] = 1.0 / float(hi[p] - lo[p] + 1)
    return m, (1.0 / (hi - lo + 1)).astype(np.float32)


def _pool_tokens_kernel(k, w, slab, vertical, inv_ref, ph_ref, h_ref, o_ref, pad_ref):
    t, tc = h_ref.shape[1], h_ref.shape[2]
    if not vertical:
        for s in range(t // slab):
            hs = h_ref[0, s * slab:(s + 1) * slab, :]
            o_ref[0, s * slab:(s + 1) * slab, :] = (
                jnp.dot(ph_ref[...], hs, precision=HIGHEST, preferred_element_type=F32) - hs)
        return
    top = (k // 2) * w
    bot = (k - k // 2 - 1) * w
    pad_ref[0:top, :] = jnp.zeros((top, tc), F32)
    if bot:
        pad_ref[top + t:top + t + bot, :] = jnp.zeros((bot, tc), F32)
    for s in range(t // slab):
        hs = h_ref[0, s * slab:(s + 1) * slab, :]
        pad_ref[top + s * slab:top + (s + 1) * slab, :] = jnp.dot(
            ph_ref[...], hs, precision=HIGHEST, preferred_element_type=F32)

    def row(r, carry):
        base = pl.multiple_of(r * w, w)
        acc = pad_ref[pl.ds(base, w), :]
        for dr in range(1, k):
            acc = acc + pad_ref[pl.ds(pl.multiple_of(base + dr * w, w), w), :]
        o_ref[0, pl.ds(base, w), :] = acc * inv_ref[r] - h_ref[0, pl.ds(base, w), :]
        return carry

    lax.fori_loop(0, t // w, row, 0)


def _pool_tokens(h, group, gdim, k, grid_w, vertical):
    b, t, d = h.shape
    tc = min(gdim, 256)
    per = gdim // tc
    if vertical:
        slab = max(grid_w, min(256, t))
        mh, _ = _win_matrix(grid_w, k)
        ph = np.kron(np.eye(slab // grid_w, dtype=np.float32), mh)
        _, inv_v = _win_matrix(t // grid_w, k)
    else:
        slab = t
        ph, _ = _win_matrix(t, k)
        inv_v = np.ones((1,), np.float32)
    pad_rows = t + (k - 1) * grid_w if vertical else SUBLANES_V7X
    return pl.pallas_call(
        functools.partial(_pool_tokens_kernel, k, grid_w, slab, vertical),
        grid=(b, per),
        in_specs=[pl.BlockSpec(memory_space=pltpu.SMEM),
                  pl.BlockSpec((slab, slab), lambda i, j: (0, 0)),
                  pl.BlockSpec((1, t, tc), lambda i, j: (i, 0, group * per + j))],
        out_specs=pl.BlockSpec((1, t, tc), lambda i, j: (i, 0, j)),
        out_shape=jax.ShapeDtypeStruct((b, t, gdim), F32),
        scratch_shapes=[pltpu.VMEM((pad_rows, tc), F32)],
        compiler_params=_cparams(("parallel", "parallel"), (4 * t + pad_rows) * tc * 4 + 2 * slab * slab * 4),
        name=f"pool_tokens_k{k}",
    )(jnp.asarray(inv_v), jnp.asarray(ph), h)


def _pool_out_kernel(n_groups, *refs):
    d_refs = refs[:n_groups]
    x_ref, w_ref, ps_ref, gt_ref, o_ref = refs[n_groups:]
    ys = [jnp.dot(d_refs[j][0].astype(BF16), w_ref[j], preferred_element_type=F32) for j in range(n_groups)]
    y = jnp.concatenate(ys, axis=-1) * ps_ref[...]
    o_ref[0] = x_ref[0] + gt_ref[0] * y


def _pool_out(ds, x, w_pool_bf16, pool_scale, gate, tm):
    b, t, d = x.shape
    ng, gdim, _ = w_pool_bf16.shape
    return pl.pallas_call(
        functools.partial(_pool_out_kernel, ng),
        grid=(b, t // tm),
        in_specs=[pl.BlockSpec((1, tm, gdim), lambda i, j: (i, j, 0)) for _ in range(ng)] + [
            pl.BlockSpec((1, tm, d), lambda i, j: (i, j, 0)),
            pl.BlockSpec((ng, gdim, gdim), lambda i, j: (0, 0, 0)),
            pl.BlockSpec((1, d), lambda i, j: (0, 0)),
            pl.BlockSpec((1, 1, d), lambda i, j: (i, 0, 0))],
        out_specs=pl.BlockSpec((1, tm, d), lambda i, j: (i, j, 0)),
        out_shape=jax.ShapeDtypeStruct((b, t, d), F32),
        compiler_params=_cparams(("parallel", "parallel"), 8 * tm * d * 4 + 2 * ng * gdim * gdim * 2),
        name="pool_out",
    )(*ds, x, w_pool_bf16, pool_scale.reshape(1, d), gate)


def _row_tile(cfg, t):
    return cfg.row_tile if t % cfg.row_tile == 0 else min(cfg.tok_tile, t)


def _pool_mixer(cfg, x, g, sh, sc, gate, w_pool_bf16, pool_scale, on_grid):
    b, t, d = x.shape
    tm = _row_tile(cfg, t)
    h = _norm_mod(x, g, sh, sc, F32, tm)
    gdim = d // len(cfg.pool_windows)
    ds = [_pool_tokens(h, j, gdim, k, cfg.grid_w, on_grid) for j, k in enumerate(cfg.pool_windows)]
    return _pool_out(ds, x, w_pool_bf16, pool_scale, gate, tm)


def _route_tile(lg, ng, eg):
    lane = lax.broadcasted_iota(jnp.int32, lg.shape, 1)
    big = jnp.int32(lg.shape[1])
    shift = eg.bit_length() - 1

    def first_max(vals):
        m = jnp.max(vals, axis=-1, keepdims=True)
        return m, jnp.min(jnp.where(vals == m, lane, big), axis=-1, keepdims=True)

    is_g = lane < ng
    gmax, g_idx = first_max(jnp.where(is_g, lg, -jnp.inf))
    p_g = 1.0 / jnp.sum(jnp.where(is_g, jnp.exp(lg - gmax), 0.0), axis=-1, keepdims=True)
    in_grp = jnp.logical_and(jnp.logical_and(lane >= ng, lane < ng + ng * eg), ((lane - ng) >> shift) == g_idx)
    el = jnp.where(in_grp, lg, -jnp.inf)
    m1, i1 = first_max(el)
    m2, i2 = first_max(jnp.where(lane == i1, -jnp.inf, el))
    t = jnp.exp(m2 - m1)
    w1 = p_g / (1.0 + t)
    w2 = w1 * t
    out = jnp.where(lane == 0, w1, jnp.where(lane == 1, w2, 0.0))
    out = jnp.where(lane == 2, (i1 - ng).astype(F32), out)
    return jnp.where(lane == 3, (i2 - ng).astype(F32), out)


def _router_kernel(nx_tiles, has_ctx, ng, eg, *refs):
    if has_ctx:
        x_ref, c_ref, g_ref, sh_ref, sc_ref, wr_ref, br_ref, fx_ref, rt_ref = refs
        src = jnp.where(pl.program_id(0) < nx_tiles, x_ref[0], c_ref[0])
    else:
        x_ref, g_ref, sh_ref, sc_ref, wr_ref, br_ref, fx_ref, rt_ref = refs
        src = x_ref[0]
    h = _rms_mod(src, g_ref[...], sh_ref[0], sc_ref[0])
    _store_slabs(fx_ref, h)
    lg = jnp.dot(h, wr_ref[...], precision=HIGHEST, preferred_element_type=F32) + br_ref[...]
    rt_ref[...] = _route_tile(lg, ng, eg)


def _router(cfg, x, ctx, g, sh_all, sc_all, w_r, b_r, tm):
    b, t, d = x.shape
    per_b = t // tm
    nx = b * per_b
    has_ctx = ctx is not None
    nc = b * (ctx.shape[1] // tm) if has_ctx else 0
    per_c = (ctx.shape[1] // tm) if has_ctx else 1
    n_all = (nx + nc) * tm
    lanes = w_r.shape[1]

    def mod_row(i):
        return jnp.where(i < nx, jnp.minimum(i, nx - 1) // per_b, b) if has_ctx else i // per_b

    in_specs = [pl.BlockSpec((1, tm, d), lambda i: (jnp.minimum(i, nx - 1) // per_b, jnp.minimum(i, nx - 1) % per_b, 0))]
    args = [x]
    if has_ctx:
        in_specs.append(pl.BlockSpec(
            (1, tm, d), lambda i: (jnp.maximum(i - nx, 0) // per_c, jnp.maximum(i - nx, 0) % per_c, 0)))
        args.append(ctx)
    in_specs += [pl.BlockSpec((1, d), lambda i: (0, 0)),
                 pl.BlockSpec((1, 1, d), lambda i: (mod_row(i), 0, 0)),
                 pl.BlockSpec((1, 1, d), lambda i: (mod_row(i), 0, 0)),
                 pl.BlockSpec((d, lanes), lambda i: (0, 0)),
                 pl.BlockSpec((1, lanes), lambda i: (0, 0))]
    args += [g.reshape(1, d), sh_all, sc_all, w_r, b_r]
    return pl.pallas_call(
        functools.partial(_router_kernel, nx, has_ctx, cfg.n_groups, cfg.experts_per_group),
        grid=(nx + nc,),
        in_specs=in_specs,
        out_specs=[pl.BlockSpec((tm * (d // LANES_V7X), LANES_V7X), lambda i: (i, 0)),
                   pl.BlockSpec((tm, lanes), lambda i: (i, 0))],
        out_shape=[jax.ShapeDtypeStruct((n_all * (d // LANES_V7X), LANES_V7X), F32),
                   jax.ShapeDtypeStruct((n_all, lanes), F32)],
        compiler_params=_cparams(("arbitrary",), 8 * tm * d * 4 + 2 * d * lanes * 4),
        name="moe_router",
    )(*args)


def _expert_kernel(rows, n_all, col, be_ref, nu_ref, code_ref, fx_hbm, wgu_ref, wdn_ref, out_hbm,
                   xbuf, ybuf, gsem, ssem):
    i = pl.program_id(0)
    n_used = nu_ref[0]
    slot = i % 2
    other = 1 - slot
    n_slots = n_all * 2
    last_blk = code_ref.shape[0] // rows - 1
    f, d = wdn_ref.shape[2], wdn_ref.shape[3]
    n_sub = d // LANES_V7X

    def slab(ref, row, count=1):
        return ref.at[pl.ds(pl.multiple_of(row * n_sub, n_sub), count * n_sub), :]

    def gather_row(blk, sl, r):
        v = code_ref[blk * rows + r]
        src = jnp.where(v >= n_slots, 0, v >> 1)
        return pltpu.make_async_copy(slab(fx_hbm, src), slab(xbuf.at[sl], r), gsem.at[sl])

    def scatter_row(blk, sl, r, none):
        v = jnp.where(none, n_slots, code_ref[blk * rows + r])
        dst = jnp.where(v >= n_slots, n_slots + sl * rows + r, (v & 1) * n_all + (v >> 1))
        return pltpu.make_async_copy(slab(ybuf.at[sl], r), slab(out_hbm, dst), ssem.at[sl])

    def wait_gather(sl):
        pltpu.make_async_copy(slab(fx_hbm, 0, rows), xbuf.at[sl], gsem.at[sl]).wait()

    def wait_scatter(sl):
        pltpu.make_async_copy(ybuf.at[sl], slab(out_hbm, 0, rows), ssem.at[sl]).wait()

    @pl.when(i == 0)
    def _():
        ybuf[...] = jnp.zeros(ybuf.shape, F32)
        for sl in range(2):
            pltpu.make_async_copy(ybuf.at[sl], slab(out_hbm, n_slots + sl * rows, rows), ssem.at[sl]).start()
        wait_scatter(1)

        @pl.when(n_used > 0)
        def _():
            def body(r, c):
                gather_row(0, 0, r).start()
                return c
            lax.fori_loop(0, rows, body, 0)

        @pl.when(n_used == 0)
        def _():
            wait_scatter(0)

    @pl.when(i < n_used)
    def _():
        wait_gather(slot)
        nxt = jnp.minimum(i + 1, last_blk)
        prev = jnp.maximum(i - 1, 0)
        no_prev = i == 0
        xb = _load_slabs(xbuf.at[slot], rows).astype(BF16)
        n1 = f // col
        per1 = rows // n1
        acts = []
        for j in range(n1):
            g = jnp.dot(xb, wgu_ref[0, 0, :, j * col:(j + 1) * col].astype(BF16), preferred_element_type=F32)
            u = jnp.dot(xb, wgu_ref[0, 0, :, f + j * col:f + (j + 1) * col].astype(BF16),
                        preferred_element_type=F32)
            acts.append((_silu(g) * u).astype(BF16))
            for r in range(j * per1, (j + 1) * per1):
                scatter_row(prev, other, r, no_prev).start()
                gather_row(nxt, other, r).start()
        act = jnp.concatenate(acts, axis=-1)
        wait_scatter(slot)
        yslot = ybuf.at[slot]
        per_col = col // LANES_V7X
        for j in range(d // col):
            y = jnp.dot(act, wdn_ref[0, 0, :, j * col:(j + 1) * col].astype(BF16), preferred_element_type=F32)
            for q in range(per_col):
                yslot[pl.ds(j * per_col + q, rows, stride=n_sub), :] = y[:, q * LANES_V7X:(q + 1) * LANES_V7X]

    @pl.when(jnp.logical_and(i == n_used, i >= 1))
    def _():
        wait_gather(slot)

        def body(r, c):
            scatter_row(i - 1, other, r, False).start()
            return c
        lax.fori_loop(0, rows, body, 0)
        wait_scatter(other)
        wait_scatter(slot)


def _experts(cfg, fx, block_e, n_used, row_code, w_gu, w_dn, layer):
    d, f2 = w_gu.shape[2], w_gu.shape[3]
    n_sub = d // LANES_V7X
    n_all = fx.shape[0] // n_sub
    f = f2 // 2
    rows = cfg.moe_rows
    col = min(256, f, d)
    n_steps = block_e.shape[0]
    grid_spec = pltpu.PrefetchScalarGridSpec(
        num_scalar_prefetch=3,
        grid=(n_steps,),
        in_specs=[pl.BlockSpec(memory_space=pl.ANY),
                  pl.BlockSpec((1, 1, d, f2), lambda i, be, nu, cd: (layer, be[i], 0, 0),
                               pipeline_mode=pl.Buffered(1)),
                  pl.BlockSpec((1, 1, f, d), lambda i, be, nu, cd: (layer, be[i], 0, 0))],
        out_specs=pl.BlockSpec(memory_space=pl.ANY),
        scratch_shapes=[pltpu.VMEM((2, rows * n_sub, LANES_V7X), F32), pltpu.VMEM((2, rows * n_sub, LANES_V7X), F32),
                        pltpu.SemaphoreType.DMA((2,)), pltpu.SemaphoreType.DMA((2,))])
    vmem = (d * f2 + 2 * f * d) * 4 + 4 * rows * d * 4 + 3 * rows * f2 * 4 + 4 * d * col * 4
    return pl.pallas_call(
        functools.partial(_expert_kernel, rows, n_all, col),
        grid_spec=grid_spec,
        out_shape=jax.ShapeDtypeStruct(((cfg.top_k * n_all + 2 * rows) * n_sub, LANES_V7X), F32),
        compiler_params=_cparams(("arbitrary",), vmem),
        name="moe_experts",
    )(block_e, n_used, row_code, fx, w_gu, w_dn)


def _combine_kernel(final, *refs):
    if final:
        x_ref, a_ref, b_ref, w_ref, gt_ref, gf_ref, o_ref = refs
    else:
        x_ref, a_ref, b_ref, w_ref, gt_ref, o_ref = refs
    w = w_ref[...]
    tm = x_ref.shape[1]
    y = x_ref[0] + gt_ref[0] * (w[:, 0:1] * _load_slabs(a_ref, tm) + w[:, 1:2] * _load_slabs(b_ref, tm))
    if final:
        y = y * lax.rsqrt(jnp.mean(y * y, axis=-1, keepdims=True) + EPS) * gf_ref[...]
    o_ref[0] = y


def _combine(x, slots, wts, tok_off, n_all, gate, tm, norm_final=None):
    b, t, d = x.shape
    per_b = t // tm
    off0 = tok_off // tm
    off1 = (n_all + tok_off) // tm
    final = norm_final is not None
    in_specs = [pl.BlockSpec((1, tm, d), lambda i, j: (i, j, 0)),
                pl.BlockSpec((tm * (d // LANES_V7X), LANES_V7X), lambda i, j: (off0 + i * per_b + j, 0)),
                pl.BlockSpec((tm * (d // LANES_V7X), LANES_V7X), lambda i, j: (off1 + i * per_b + j, 0)),
                pl.BlockSpec((tm, wts.shape[1]), lambda i, j: (off0 + i * per_b + j, 0)),
                pl.BlockSpec((1, 1, d), lambda i, j: (i, 0, 0))]
    args = [x, slots, slots, wts, gate]
    if final:
        in_specs.append(pl.BlockSpec((1, d), lambda i, j: (0, 0)))
        args.append(norm_final.reshape(1, d))
    return pl.pallas_call(
        functools.partial(_combine_kernel, final),
        grid=(b, per_b),
        in_specs=in_specs,
        out_specs=pl.BlockSpec((1, tm, d), lambda i, j: (i, j, 0)),
        out_shape=jax.ShapeDtypeStruct((b, t, d), F32),
        compiler_params=_cparams(("parallel", "parallel"), 10 * tm * d * 4),
        name="moe_combine",
    )(*args)


def _route(cfg, routed):
    n_all = routed.shape[0]
    ng, eg, k = cfg.n_groups, cfg.experts_per_group, cfg.top_k
    assert k == 2, "row codes pack (token, choice) as 2 * token + choice"
    n_exp = ng * eg
    rows = cfg.moe_rows
    wts = routed[:, 0:k]
    eid = routed[:, k:2 * k].astype(jnp.int32)

    s = n_all * k
    e_flat = eid.reshape(-1)
    counts = jnp.sum((e_flat[:, None] == jnp.arange(n_exp, dtype=jnp.int32)[None, :]).astype(jnp.int32), axis=0)
    order = jnp.argsort(e_flat, stable=True).astype(jnp.int32)
    starts = jnp.cumsum(counts) - counts
    padded = ((counts + rows - 1) // rows) * rows
    pends = jnp.cumsum(padded)
    pstarts = pends - padded
    n_blocks = -(-(s + n_exp * (rows - 1)) // rows)
    blk_start = jnp.arange(n_blocks + 1, dtype=jnp.int32) * rows
    block_e = jnp.minimum(jnp.sum((pends[None, :] <= blk_start[:, None]).astype(jnp.int32), axis=1), n_exp - 1)
    pos = jnp.arange(n_blocks * rows, dtype=jnp.int32)
    e_row = jnp.repeat(block_e[:n_blocks], rows)
    rank = pos - pstarts[e_row]
    routed = jnp.logical_and(rank < counts[e_row], pos < pends[-1])
    row_code = jnp.where(routed, order[jnp.clip(starts[e_row] + rank, 0, s - 1)], s).astype(jnp.int32)
    n_used = (pends[-1] // rows).astype(jnp.int32).reshape(1)
    return (block_e, n_used, row_code), wts


def _moe(cfg, x, ctx, g, sh_all, sc_all, w_r, b_r, w_gu, w_dn, layer):
    b, t, d = x.shape
    tm = _row_tile(cfg, t) if ctx is None else min(cfg.tok_tile, t, ctx.shape[1])
    fx, routed = _router(cfg, x, ctx, g, sh_all, sc_all, w_r, b_r, tm)
    n_all = routed.shape[0]
    plan, wts = _route(cfg, routed)
    slots = _experts(cfg, fx, *plan, w_gu, w_dn, layer)
    return slots, wts, n_all


def _cast_weight_once(w_ref, wbf_ref):
    @pl.when(jnp.logical_and(pl.program_id(0) == 0, pl.program_id(1) == 0))
    def _():
        k = wbf_ref.shape[0]
        step = min(k, 256)

        def body(c, carry):
            r0 = pl.multiple_of(c * step, step)
            wbf_ref[pl.ds(r0, step), :] = w_ref[0, pl.ds(r0, step), :].astype(BF16)
            return carry
        lax.fori_loop(0, k // step, body, 0)


def _proj_kernel(kind, n_heads, hd, *refs):
    if kind == "gate":
        h_ref, w_ref, la_ref, lc_ref, o_ref, wbf_ref = refs
    else:
        h_ref, w_ref, o_ref, wbf_ref = refs
    _cast_weight_once(w_ref, wbf_ref)
    z = jnp.dot(h_ref[0], wbf_ref[...], preferred_element_type=F32)
    if kind == "silu":
        z = _silu(z)
    elif kind == "gate":
        ls = jnp.minimum(z, 0.0) - jnp.log(1.0 + jnp.exp(-jnp.abs(z)))
        c = lc_ref[...] + ls
        a = jnp.broadcast_to(la_ref[...], c.shape)
        m = jnp.maximum(a, c)
        z = m + jnp.log(1.0 + jnp.exp(-jnp.abs(a - c)))
    for h in range(n_heads):
        o_ref[0, h] = z[:, h * hd:(h + 1) * hd].astype(o_ref.dtype)


def _proj(cfg, h_all, w_all, layer, section, kind, out_dtype, log_lb=None, log1m_lb=None):
    b, s, d = h_all.shape
    n = d
    hd = cfg.head_dim
    nh = n // hd
    tm = max(t for t in range(SUBLANES_V7X, 2 * cfg.tok_tile + 129, SUBLANES_V7X) if s % t == 0)
    in_specs = [pl.BlockSpec((1, tm, d), lambda i, j: (i, j, 0)),
                pl.BlockSpec((1, d, n), lambda i, j: (layer, 0, section), pipeline_mode=pl.Buffered(1))]
    args = [h_all, w_all]
    if kind == "gate":
        in_specs += [pl.BlockSpec((1, n), lambda i, j: (0, 0)), pl.BlockSpec((1, n), lambda i, j: (0, 0))]
        args += [log_lb.reshape(1, n), log1m_lb.reshape(1, n)]
    return pl.pallas_call(
        functools.partial(_proj_kernel, kind, nh, hd),
        grid=(b, s // tm),
        in_specs=in_specs,
        out_specs=pl.BlockSpec((1, nh, tm, hd), lambda i, j: (i, 0, j, 0)),
        out_shape=jax.ShapeDtypeStruct((b, nh, s, hd), out_dtype),
        scratch_shapes=[pltpu.VMEM((d, n), BF16)],
        compiler_params=_cparams(("arbitrary", "arbitrary"), d * n * 6 + 2 * tm * d * 2 + 6 * tm * n * 4),
        name=f"hgrn_proj_{kind}",
    )(*args)


def _gla_kernel(chunk, reverse, readout, limit, *refs):
    if readout:
        q_ref, lf_ref, v_ref, of_ref, gs_ref, ng_ref, o_ref, st_ref, b_ref, q32_ref, at_ref = refs
    else:
        q_ref, lf_ref, v_ref, o_ref, st_ref, b_ref, q32_ref, at_ref = refs
    hg, tb, hd = q_ref.shape[1], q_ref.shape[2], q_ref.shape[3]
    n_chunks = tb // chunk
    shift = chunk.bit_length() - 1

    @pl.when(pl.program_id(2) == 0)
    def _():
        st_ref[...] = jnp.zeros(st_ref.shape, F32)

    rr = lax.broadcasted_iota(jnp.int32, (tb, tb), 0)
    cc = lax.broadcasted_iota(jnp.int32, (tb, tb), 1)
    same = (rr >> shift) == (cc >> shift)
    tri = jnp.where(jnp.logical_and(same, (cc >= rr) if reverse else (cc <= rr)), 1.0, 0.0).astype(BF16)
    ar = lax.broadcasted_iota(jnp.int32, (chunk, chunk), 0)
    ac = lax.broadcasted_iota(jnp.int32, (chunk, chunk), 1)
    at_mask = (ar >= ac) if reverse else (ar <= ac)
    row_id = lax.broadcasted_iota(jnp.int32, (chunk, hd), 0)
    order = list(range(n_chunks))[::-1] if reverse else list(range(n_chunks))
    edge = 0 if reverse else chunk - 1
    grp = b_ref.shape[0]

    def group(hp, carry):
        heads = [hp * grp + u for u in range(grp)]
        lf_all = jnp.concatenate([lf_ref[0, h] for h in heads], axis=-1)
        hi = lf_all.astype(BF16)
        r1 = lf_all - hi.astype(F32)
        mid = r1.astype(BF16)
        lo = (r1 - mid.astype(F32)).astype(BF16)
        b_all = (jnp.dot(tri, hi, preferred_element_type=F32) + jnp.dot(tri, mid, preferred_element_type=F32)
                 + jnp.dot(tri, lo, preferred_element_type=F32))
        bs, kks, qfs, btots = [], [], [], []
        worst = None
        for u, h in enumerate(heads):
            b = b_all[:, u * hd:(u + 1) * hd]
            b_ref[u] = b
            btot = [b[c * chunk + edge:c * chunk + edge + 1, :] for c in range(n_chunks)]
            for bt in btot:
                worst = bt if worst is None else jnp.minimum(worst, bt)
            bs.append(b)
            btots.append(btot)
            kks.append(1.0 - jnp.exp(lf_all[:, u * hd:(u + 1) * hd]))
            qfs.append(q_ref[0, h].astype(F32))
        fast = jnp.min(worst) >= -limit

        @pl.when(fast)
        def _():
            for u in range(grp):
                for c in range(n_chunks):
                    sl = slice(c * chunk, (c + 1) * chunk)
                    d = bs[u][sl] - 0.5 * btots[u][c]
                    qh = (qfs[u][sl] * jnp.exp(d)).astype(BF16)
                    kh = (kks[u][sl] * jnp.exp(-d)).astype(BF16)
                    at = lax.dot_general(kh, qh, (((1,), (1,)), ((), ())), preferred_element_type=F32)
                    at_ref[u, c] = jnp.where(at_mask, at, 0.0)

        @pl.when(jnp.logical_not(fast))
        def _():
            for u in range(grp):
                q32_ref[u] = qfs[u]
                for c in range(n_chunks):
                    c0 = c * chunk
                    bc = bs[u][c0:c0 + chunk]
                    kc = kks[u][c0:c0 + chunk]

                    def tstep(t, at):
                        bt = b_ref[u, pl.ds(c0 + t, 1), :]
                        qrow = q32_ref[u, pl.ds(c0 + t, 1), :]
                        valid = (row_id >= t) if reverse else (row_id <= t)
                        e = jnp.exp(jnp.where(valid, bt - bc, -jnp.inf))
                        col = jnp.sum(e * kc * qrow, axis=-1, keepdims=True)
                        return at + col * jnp.where(ac == t, 1.0, 0.0)

                    at_ref[u, c] = lax.fori_loop(0, chunk, tstep, jnp.zeros((chunk, chunk), F32))

        for u, h in enumerate(heads):
            b, kk, btot = bs[u], kks[u], btots[u]
            qb = (qfs[u] * jnp.exp(b)).astype(BF16)
            vb = v_ref[0, h]
            st = st_ref[h]
            for c in order:
                sl = slice(c * chunk, (c + 1) * chunk)
                k2 = (kk[sl] * jnp.exp(btot[c] - b[sl])).astype(BF16)
                o = lax.dot_general(at_ref[u, c].astype(BF16), vb[sl], (((0,), (0,)), ((), ())),
                                    preferred_element_type=F32)
                o = o + lax.dot_general(qb[sl], st.astype(BF16), (((1,), (1,)), ((), ())),
                                        preferred_element_type=F32)
                st = st * jnp.exp(btot[c]) + lax.dot_general(vb[sl], k2, (((0,), (0,)), ((), ())),
                                                             preferred_element_type=F32)
                if readout:
                    o = o + of_ref[0, h, sl, :]
                    o = o * lax.rsqrt(jnp.mean(o * o, axis=-1, keepdims=True) + EPS) * ng_ref[...]
                    o_ref[0, h, sl, :] = (o * gs_ref[0, h, sl, :].astype(F32)).astype(o_ref.dtype)
                else:
                    o_ref[0, h, sl, :] = o
            st_ref[h] = st
        return carry

    lax.fori_loop(0, hg // grp, group, 0)


def _gla(cfg, qs, lf, v, reverse, n_ctx_blocks, of=None, gs=None, norm_g=None):
    b, nh, s, hd = qs.shape
    tb = cfg.tok_tile
    hg = nh
    grp = cfg.gla_group if hg % cfg.gla_group == 0 else 1
    nblk = s // tb
    readout = of is not None

    if reverse:
        def blk(j):
            return jnp.where(j < n_ctx_blocks, n_ctx_blocks - 1 - j, nblk - 1 - (j - n_ctx_blocks))
    else:
        def blk(j):
            return j

    spec = pl.BlockSpec((1, hg, tb, hd), lambda i, g, j: (i, g, blk(j), 0))
    in_specs = [spec, spec, spec]
    args = [qs, lf, v]
    if readout:
        in_specs += [spec, spec, pl.BlockSpec((1, hd), lambda i, g, j: (0, 0))]
        args += [of, gs, norm_g.reshape(1, hd)]
    out_dtype = BF16 if readout else F32
    vmem = 2 * hg * tb * hd * (2 + 4 + 2 + 4 + 2 + 4) + hg * hd * hd * 4 + 8 * tb * tb * 4 + 16 * tb * hd * 4
    return pl.pallas_call(
        functools.partial(_gla_kernel, cfg.gla_chunk, reverse, readout, cfg.fast_decay_limit),
        grid=(b, nh // hg, nblk),
        in_specs=in_specs,
        out_specs=spec,
        out_shape=jax.ShapeDtypeStruct((b, nh, s, hd), out_dtype),
        scratch_shapes=[pltpu.VMEM((hg, hd, hd), F32), pltpu.VMEM((grp, tb, hd), F32),
                        pltpu.VMEM((grp, tb, hd), F32),
                        pltpu.VMEM((grp, tb // cfg.gla_chunk, cfg.gla_chunk, cfg.gla_chunk), F32)],
        compiler_params=_cparams(("parallel", "parallel", "arbitrary"), vmem),
        name="hgrn_scan_bwd" if reverse else "hgrn_scan_fwd",
    )(*args)


def _out_proj_kernel(n_heads, y_ref, x_ref, w_ref, gt_ref, o_ref, wbf_ref):
    _cast_weight_once(w_ref, wbf_ref)
    y = jnp.concatenate([y_ref[0, h] for h in range(n_heads)], axis=-1)
    o_ref[0] = x_ref[0] + gt_ref[0] * jnp.dot(y, wbf_ref[...], preferred_element_type=F32)


def _out_proj(cfg, y_heads, x, w_out_all, layer, gate, ctx_tiles):
    b, t, d = x.shape
    nh, hd = y_heads.shape[1], y_heads.shape[3]
    tm = cfg.tok_tile
    return pl.pallas_call(
        functools.partial(_out_proj_kernel, nh),
        grid=(b, t // tm),
        in_specs=[pl.BlockSpec((1, nh, tm, hd), lambda i, j: (i, 0, j + ctx_tiles, 0)),
                  pl.BlockSpec((1, tm, d), lambda i, j: (i, j, 0)),
                  pl.BlockSpec((1, d, d), lambda i, j: (layer, 0, 0), pipeline_mode=pl.Buffered(1)),
                  pl.BlockSpec((1, 1, d), lambda i, j: (i, 0, 0))],
        out_specs=pl.BlockSpec((1, tm, d), lambda i, j: (i, j, 0)),
        out_shape=jax.ShapeDtypeStruct((b, t, d), F32),
        scratch_shapes=[pltpu.VMEM((d, d), BF16)],
        compiler_params=_cparams(("arbitrary", "arbitrary"), d * d * 6 + 8 * tm * d * 4),
        name="hgrn_out_proj",
    )(y_heads, x, w_out_all, gate)


def _hgrn_mixer(cfg, x, ctx, g, sh_all, sc_all, gate, w_in_all, norm_g, w_out_all, layer, lb):
    b, t, d = x.shape
    tm = cfg.tok_tile
    h_all = _norm_mod_seq(ctx, x, g, sh_all, sc_all, BF16, tm)
    log_lb = jnp.log(lb)
    log1m_lb = jnp.log1p(-lb)
    qs = _proj(cfg, h_all, w_in_all, layer, 0, "silu", BF16)
    gs = _proj(cfg, h_all, w_in_all, layer, 1, "silu", BF16)
    lf = _proj(cfg, h_all, w_in_all, layer, 2, "gate", F32, log_lb[0], log1m_lb[0])
    lr = _proj(cfg, h_all, w_in_all, layer, 3, "gate", F32, log_lb[1], log1m_lb[1])
    v = _proj(cfg, h_all, w_in_all, layer, 4, "none", BF16)
    n_ctx_blocks = ctx.shape[1] // tm
    of = _gla(cfg, qs, lf, v, False, n_ctx_blocks)
    y = _gla(cfg, qs, lr, v, True, n_ctx_blocks, of, gs, norm_g)
    return _out_proj(cfg, y, x, w_out_all, layer, gate, n_ctx_blocks)


def _forward(cfg, x, c, ctx, c_ctx, norm_mix, norm_ffn, w_ada, b_ada, pool_w, pool_scale, hgrn_w_in, hgrn_norm,
             hgrn_w_out, hgrn_lb_logits, router_w_group, router_b_group, router_w_expert, router_b_expert,
             moe_w_gate_up, moe_w_down, norm_final):
    b, t, d = x.shape
    depth = w_ada.shape[0]
    n_mixers = 2
    p_lb = jax.nn.softmax(hgrn_lb_logits.astype(F32), axis=0)
    lb_all = jnp.cumsum(p_lb, axis=0) - p_lb[0]

    rows = -(-(b + 1) // SUBLANES_V7X) * SUBLANES_V7X
    cond = jnp.zeros((rows, d), F32).at[:b].set(c).at[b].set(c_ctx)
    mods = _adaln(cond, w_ada, b_ada)

    ng, ne = cfg.n_groups, cfg.n_groups * cfg.experts_per_group
    for i in range(depth):
        ctx_needed = i < depth - 1
        j = i // n_mixers
        m = [mods[i, :, q * d:(q + 1) * d].reshape(rows, 1, d) for q in range(6)]
        sh_m, sc_m, gt_m, sh_f, sc_f, gt_f = m
        ctx_rows = lambda a: jnp.broadcast_to(a[b:b + 1], (b, 1, d))
        if i % n_mixers == 0:
            wp = pool_w[j].astype(BF16)
            x_new = _pool_mixer(cfg, x, norm_mix[i], sh_m, sc_m, gt_m, wp, pool_scale[j], True)
            if ctx_needed:
                ctx = _pool_mixer(cfg, ctx, norm_mix[i], ctx_rows(sh_m), ctx_rows(sc_m), ctx_rows(gt_m), wp,
                                  pool_scale[j], False)
            x = x_new
        else:
            x = _hgrn_mixer(cfg, x, ctx, norm_mix[i], sh_m, sc_m, gt_m, hgrn_w_in, hgrn_norm[j], hgrn_w_out, j,
                            lb_all[i])
        lanes = -(-(ng + ne) // LANES_V7X) * LANES_V7X
        w_r = jnp.zeros((d, lanes), F32).at[:, :ng].set(router_w_group[i]).at[:, ng:ng + ne].set(router_w_expert[i])
        b_r = jnp.zeros((1, lanes), F32).at[0, :ng].set(router_b_group[i]).at[0, ng:ng + ne].set(router_b_expert[i])
        slots, wts, n_all = _moe(cfg, x, ctx if ctx_needed else None, norm_ffn[i], sh_f, sc_f, w_r, b_r,
                                 moe_w_gate_up, moe_w_down, i)
        tm = _row_tile(cfg, t)
        last = i == depth - 1
        x_next = _combine(x, slots, wts, 0, n_all, gt_f, tm, norm_final if last else None)
        if ctx_needed:
            ctx = _combine(ctx, slots, wts, b * t, n_all, ctx_rows(gt_f), min(cfg.tok_tile, ctx.shape[1]))
        x = x_next
    return x


def kernel(x, c, ctx, c_ctx, norm_mix, norm_ffn, w_ada, b_ada, pool_w, pool_scale, hgrn_w_in, hgrn_norm, hgrn_w_out, hgrn_lb_logits, router_w_group, router_b_group, router_w_expert, router_b_expert, moe_w_gate_up, moe_w_down, norm_final):
    return _forward(Cfg(), x, c, ctx, c_ctx, norm_mix, norm_ffn, w_ada, b_ada, pool_w, pool_scale, hgrn_w_in,
                    hgrn_norm, hgrn_w_out, hgrn_lb_logits, router_w_group, router_b_group, router_w_expert,
                    router_b_expert, moe_w_gate_up, moe_w_down, norm_final)
```
